```python
import math
import jax
import jax.numpy as jnp
from jax import lax
import numpy as np

D_MODEL = 2048
BATCH = 8
SEQ = 2048
DEPTH = 4

GRID_W = 64
CTX_LEN = 256
CHUNK = 64
N_DIR = 2
N_BRANCH = 3
EPS = 1e-6

GLA_HEADS = 4
GLA_DK = 128
GLA_DV = 256
GLA_RANK = 16
GLA_TAU = 16.0

GDN_HEADS = 8
GDN_DK = 128
GDN_DV = 128
GDN_CONV_K = 3

POOL_WINDOWS = (2, 4, 8, 16)
POOL_GROUPS = 4
POOL_GROUP = 256

D_FF = 5632
FFN_CONV_K = 3

GLA_QK = GLA_HEADS * GLA_DK
GLA_V = GLA_HEADS * GLA_DV
GDN_QK = GDN_HEADS * GDN_DK
GDN_V = GDN_HEADS * GDN_DV
POOL_WIDTH = POOL_GROUPS * POOL_GROUP
IN_SPLITS = (GLA_QK, GLA_QK, GLA_V, GLA_V, N_DIR * GLA_RANK,
             2 * GDN_QK + GDN_V, N_DIR * GDN_HEADS, N_DIR * GDN_HEADS, GDN_V,
             POOL_WIDTH, N_BRANCH * D_MODEL)
IN_WIDTH = (2 * GLA_QK + 2 * GLA_V + N_DIR * GLA_RANK + 2 * GDN_QK + 2 * GDN_V
            + 2 * N_DIR * GDN_HEADS + POOL_WIDTH + N_BRANCH * D_MODEL)
F32 = jnp.float32

kernel_name = 'hybrid_gla_deltanet_pool_dit'


def _split_cols(t, sizes):
    parts, start = [], 0
    for size in sizes:
        parts.append(t[..., start:start + size])
        start += size
    return parts


def rmsnorm(t, w):
    tf = t.astype(F32)
    y = tf * lax.rsqrt(jnp.mean(tf * tf, axis=-1, keepdims=True) + EPS)
    return (y * w.astype(F32)).astype(t.dtype)


def l2norm(t):
    return t * lax.rsqrt(jnp.sum(t * t, axis=-1, keepdims=True) + EPS)


def ada_mod(cond, w, b):
    m = jax.nn.silu(cond) @ w + b
    return m.reshape(cond.shape[:-1] + (6, D_MODEL))


def modulate(h, shift, scale):
    return h * (1.0 + scale) + shift


def dwconv1d_centred(t, w):
    k, T = w.shape[0], t.shape[1]
    half = k // 2
    tp = jnp.pad(t, ((0, 0), (half, half), (0, 0)))
    out = tp[:, 0:T] * w[0]
    for j in range(1, k):
        out = out + tp[:, j:j + T] * w[j]
    return out


def dwconv2d_grid(t, w, rows, cols):
    B, T, C = t.shape
    img = t.reshape(B, rows, cols, C)
    y = lax.conv_general_dilated(img, w[:, :, None, :].astype(t.dtype), (1, 1), 'SAME',
                                 dimension_numbers=('NHWC', 'HWIO', 'NHWC'),
                                 feature_group_count=C)
    return y.reshape(B, T, C)


def _to_chunks(t):
    B, T, H, d = t.shape
    return t.reshape(B, T // CHUNK, CHUNK, H, d).transpose(0, 3, 1, 2, 4)


def _from_chunks(t):
    B, H, N, C, d = t.shape
    return t.transpose(0, 2, 3, 1, 4).reshape(B, N * C, H, d)


def gla_chunked(q, k, v, log_a, s0):
    q, k, v, log_a = (_to_chunks(t) for t in (q, k, v, log_a))
    b = jnp.cumsum(log_a, axis=3)
    b_last = b[:, :, :, -1:, :]
    q_dec = q * jnp.exp(b)
    scores = jnp.einsum('bhncd,bhnsd->bhncs', q_dec, k * jnp.exp(-b))
    incl = jnp.tril(jnp.ones((CHUNK, CHUNK), bool))
    scores = jnp.where(incl, scores, 0.0)
    o_intra = jnp.einsum('bhncs,bhnsv->bhncv', scores, v)
    kv = jnp.einsum('bhncd,bhncv->bhndv', k * jnp.exp(b_last - b), v)
    decay = jnp.exp(b_last[:, :, :, 0, :])

    def step(s, inp):
        d_i, kv_i = inp
        return d_i[..., None] * s + kv_i, s

    s_final, s_start = lax.scan(step, s0, (jnp.moveaxis(decay, 2, 0), jnp.moveaxis(kv, 2, 0)))
    o_inter = jnp.einsum('bhncd,nbhdv->bhncv', q_dec, s_start)
    return _from_chunks(o_intra + o_inter), s_final


def gdn_chunked(q, k, v, g, beta, s0):
    q, k, v = (_to_chunks(t) for t in (q, k, v))
    g, beta = (_to_chunks(t[..., None])[..., 0] for t in (g, beta))
    b = jnp.cumsum(g, axis=-1)
    b_last = b[..., -1:]
    incl = jnp.tril(jnp.ones((CHUNK, CHUNK), bool))
    strict = jnp.tril(jnp.ones((CHUNK, CHUNK), bool), -1)
    diff = b[..., :, None] - b[..., None, :]
    decay = jnp.where(incl, jnp.exp(jnp.where(incl, diff, 0.0)), 0.0)
    kb = k * beta[..., None]
    L = jnp.where(strict, jnp.einsum('bhncd,bhnsd->bhncs', kb, k) * decay, 0.0)
    eye = jnp.eye(CHUNK, dtype=F32)
    tinv = lax.linalg.triangular_solve(eye + L, jnp.broadcast_to(eye, L.shape),
                                       left_side=True, lower=True, unit_diagonal=True)
    u = tinv @ (v * beta[..., None])
    w = tinv @ (kb * jnp.exp(b)[..., None])
    attn = jnp.einsum('bhncd,bhnsd->bhncs', q, k) * decay
    q_dec = q * jnp.exp(b)[..., None]
    k_dec = k * jnp.exp(b_last - b)[..., None]
    chunk_decay = jnp.exp(b_last)[..., 0]

    def step(s, inp):
        u_i, w_i, attn_i, q_i, k_i, d_i = inp
        v_new = u_i - jnp.einsum('bhck,bhkv->bhcv', w_i, s)
        o_i = (jnp.einsum('bhck,bhkv->bhcv', q_i, s)
               + jnp.einsum('bhcs,bhsv->bhcv', attn_i, v_new))
        s = d_i[..., None, None] * s + jnp.einsum('bhck,bhcv->bhkv', k_i, v_new)
        return s, o_i

    xs = tuple(jnp.moveaxis(t, 2, 0) for t in (u, w, attn, q_dec, k_dec, chunk_decay))
    s_final, o = lax.scan(step, s0, xs)
    return _from_chunks(jnp.moveaxis(o, 0, 2)), s_final


def _rev(t, d):
    return t if d == 0 else jnp.flip(t, axis=1)


def bidirectional_scan(scan_fn, lat_args, lat_dir, ctx_args, ctx_dir, s0):
    out_lat, out_ctx = None, None
    for d in range(N_DIR):
        o_c, s_c = scan_fn(*[_rev(t, d) for t in ctx_args + ctx_dir[d]], s0)
        o_x, _ = scan_fn(*[_rev(t, d) for t in lat_args + lat_dir[d]], s_c)
        o_c, o_x = _rev(o_c, d), _rev(o_x, d)
        out_lat = o_x if out_lat is None else out_lat + o_x
        out_ctx = o_c if out_ctx is None else out_ctx + o_c
    return out_lat, out_ctx


def gla_prep(q, k, v, lr, w2, b):
    B, T, _ = q.shape
    q = q.astype(F32).reshape(B, T, GLA_HEADS, GLA_DK) * GLA_DK ** -0.5
    k = k.astype(F32).reshape(B, T, GLA_HEADS, GLA_DK)
    v = v.astype(F32).reshape(B, T, GLA_HEADS, GLA_DV)
    z = jnp.einsum('btdr,drk->dbtk', lr.reshape(B, T, N_DIR, GLA_RANK), w2) + b[:, None, None, :]
    log_a = jax.nn.log_sigmoid(z.astype(F32)) / GLA_TAU
    log_a = log_a.reshape(N_DIR, B, T, GLA_HEADS, GLA_DK)
    return (q, k, v), tuple((log_a[d],) for d in range(N_DIR))


def gdn_prep(qkv, a, bt, conv_w, a_log, dt_bias):
    B, T, _ = qkv.shape
    qkv = jax.nn.silu(dwconv1d_centred(qkv, conv_w)).astype(F32)
    q, k, v = _split_cols(qkv, (GDN_QK, GDN_QK, GDN_V))
    q = l2norm(q.reshape(B, T, GDN_HEADS, GDN_DK)) * GDN_DK ** -0.5
    k = l2norm(k.reshape(B, T, GDN_HEADS, GDN_DK))
    v = v.reshape(B, T, GDN_HEADS, GDN_DV)
    a = a.astype(F32).reshape(B, T, N_DIR, GDN_HEADS)
    g = -jnp.exp(a_log.astype(F32)) * jax.nn.softplus(a + dt_bias.astype(F32))
    beta = jax.nn.sigmoid(bt.astype(F32).reshape(B, T, N_DIR, GDN_HEADS))
    return (q, k, v), tuple((g[:, :, d], beta[:, :, d]) for d in range(N_DIR))


def multiscale_pool(u, pool_w, pool_scale):
    B, T, _ = u.shape
    uf = u.astype(F32)
    cs = jnp.concatenate([jnp.zeros((B, 1, POOL_WIDTH), F32), jnp.cumsum(uf, axis=1)], axis=1)
    t = jnp.arange(T)
    means = []
    for gi, win in enumerate(POOL_WINDOWS):
        lo = jnp.clip(t - win // 2, 0, T)
        hi = jnp.clip(t - win // 2 + win, 0, T)
        csg = cs[..., gi * POOL_GROUP:(gi + 1) * POOL_GROUP]
        win_sum = jnp.take(csg, hi, axis=1) - jnp.take(csg, lo, axis=1)
        means.append(win_sum / (hi - lo).astype(F32)[None, :, None])
    p = jnp.concatenate(means, axis=-1) - uf
    p = jnp.einsum('btgc,gcd->btgd', p.reshape(B, T, POOL_GROUPS, POOL_GROUP), pool_w.astype(F32))
    return p.reshape(B, T, POOL_WIDTH) * pool_scale.astype(F32)


def hybrid_mixer(hx, hc, w_in, gla_w2, gla_b, gla_nw, gdn_cw, gdn_alog, gdn_dtb, gdn_nw,
                 pool_w, pool_scale, w_br_gla, w_br_gdn, w_br_pool, w_out, need_ctx):
    dt = hx.dtype
    B = hx.shape[0]
    px = _split_cols(hx @ w_in, IN_SPLITS)
    pc = _split_cols(hc @ w_in, IN_SPLITS)
    ax, ax_dir = gla_prep(px[0], px[1], px[2], px[4], gla_w2, gla_b)
    ac, ac_dir = gla_prep(pc[0], pc[1], pc[2], pc[4], gla_w2, gla_b)
    s0_gla = jnp.zeros((B, GLA_HEADS, GLA_DK, GLA_DV), F32)
    o_gla_x, o_gla_c = bidirectional_scan(gla_chunked, ax, ax_dir, ac, ac_dir, s0_gla)
    bx, bx_dir = gdn_prep(px[5], px[6], px[7], gdn_cw, gdn_alog, gdn_dtb)
    bc, bc_dir = gdn_prep(pc[5], pc[6], pc[7], gdn_cw, gdn_alog, gdn_dtb)
    s0_gdn = jnp.zeros((B, GDN_HEADS, GDN_DK, GDN_DV), F32)
    o_gdn_x, o_gdn_c = bidirectional_scan(gdn_chunked, bx, bx_dir, bc, bc_dir, s0_gdn)

    def merge(p, o_gla, o_gdn):
        Bq, T, _ = p[0].shape
        z_gla = jax.nn.silu(p[3].astype(F32)).reshape(Bq, T, GLA_HEADS, GLA_DV)
        y_gla = (rmsnorm(o_gla, gla_nw) * z_gla).reshape(Bq, T, GLA_V).astype(dt)
        z_gdn = jax.nn.silu(p[8].astype(F32)).reshape(Bq, T, GDN_HEADS, GDN_DV)
        y_gdn = (rmsnorm(o_gdn, gdn_nw) * z_gdn).reshape(Bq, T, GDN_V).astype(dt)
        y_pool = multiscale_pool(p[9], pool_w, pool_scale).astype(dt)
        gates = jax.nn.sigmoid(p[10].astype(F32)).reshape(Bq, T, N_BRANCH, D_MODEL)
        m = (gates[:, :, 0] * (y_gla @ w_br_gla) + gates[:, :, 1] * (y_gdn @ w_br_gdn)
             + gates[:, :, 2] * (y_pool @ w_br_pool))
        return m.astype(dt) @ w_out

    y_x = merge(px, o_gla_x, o_gdn_x)
    y_c = merge(pc, o_gla_c, o_gdn_c) if need_ctx else None
    return y_x, y_c


def conv_ffn(h, w_up, w_conv, w_down, rows, cols):
    a, v = _split_cols(h @ w_up, (D_FF, D_FF))
    a = dwconv2d_grid(a, w_conv, rows, cols)
    return (jax.nn.silu(a) * v) @ w_down


def setup_inputs(seed: int = 0) -> dict:
    key = jax.random.key(seed)
    ks = jax.random.split(key, 26)
    L, D = DEPTH, D_MODEL

    def nrm(k, shape, scale):
        return jax.random.normal(k, shape, F32) * scale

    def gain(k, shape):
        return 1.0 + 0.02 * jax.random.normal(k, shape, F32)

    dt = jnp.exp(jax.random.uniform(ks[13], (L, N_DIR, GDN_HEADS), F32,
                                    math.log(1e-3), math.log(1e-1)))
    return {
        'x': nrm(ks[0], (BATCH, SEQ, D), 1.0),
        'c': nrm(ks[1], (BATCH, D), 1.0),
        'ctx': nrm(ks[2], (BATCH, CTX_LEN, D), 1.0),
        'c_ctx': nrm(ks[3], (D,), 1.0),
        'ada_w': nrm(ks[4], (L, D, 6 * D), 0.5 * D ** -0.5),
        'ada_b': nrm(ks[5], (L, 6 * D), 0.01),
        'norm1_w': gain(ks[6], (L, D)),
        'norm2_w': gain(ks[7], (L, D)),
        'w_in': nrm(ks[8], (L, D, IN_WIDTH), D ** -0.5),
        'gla_lr_w2': nrm(ks[9], (L, N_DIR, GLA_RANK, GLA_QK), GLA_RANK ** -0.5),
        'gla_lr_b': nrm(ks[10], (L, N_DIR, GLA_QK), 0.1),
        'gla_norm_w': gain(ks[11], (L, GLA_DV)),
        'gdn_conv_w': nrm(ks[12], (L, GDN_CONV_K, 2 * GDN_QK + GDN_V), GDN_CONV_K ** -0.5),
        'gdn_a_log': jnp.log(jax.random.uniform(ks[14], (L, N_DIR, GDN_HEADS), F32, 1.0, 16.0)),
        'gdn_dt_bias': dt + jnp.log(-jnp.expm1(-dt)),
        'gdn_norm_w': gain(ks[15], (L, GDN_DV)),
        'pool_w': nrm(ks[16], (L, POOL_GROUPS, POOL_GROUP, POOL_GROUP), POOL_GROUP ** -0.5),
        'pool_scale': gain(ks[17], (L, POOL_WIDTH)),
        'w_br_gla': nrm(ks[18], (L, GLA_V, D), GLA_V ** -0.5),
        'w_br_gdn': nrm(ks[19], (L, GDN_V, D), GDN_V ** -0.5),
        'w_br_pool': nrm(ks[20], (L, POOL_WIDTH, D), POOL_WIDTH ** -0.5),
        'w_out': nrm(ks[21], (L, D, D), D ** -0.5),
        'ffn_up': nrm(ks[22], (L, D, 2 * D_FF), D ** -0.5),
        'ffn_conv': nrm(ks[23], (L, FFN_CONV_K, FFN_CONV_K, D_FF), 1.0 / FFN_CONV_K),
        'ffn_down': nrm(ks[24], (L, D_FF, D), D_FF ** -0.5),
        'final_norm_w': gain(ks[25], (D,)),
    }


def reference(x, c, ctx, c_ctx, ada_w, ada_b, norm1_w, norm2_w, w_in, gla_lr_w2, gla_lr_b,
              gla_norm_w, gdn_conv_w, gdn_a_log, gdn_dt_bias, gdn_norm_w, pool_w, pool_scale,
              w_br_gla, w_br_gdn, w_br_pool, w_out, ffn_up, ffn_conv, ffn_down, final_norm_w):
    T = x.shape[1]
    rows = T // GRID_W
    ctx_len = ctx.shape[1]
    h_ctx = ctx
    for l in range(DEPTH):
        last = l == DEPTH - 1
        mx = ada_mod(c, ada_w[l], ada_b[l])
        mc = ada_mod(c_ctx, ada_w[l], ada_b[l])
        hx = modulate(rmsnorm(x, norm1_w[l]), mx[:, None, 0], mx[:, None, 1])
        hc = modulate(rmsnorm(h_ctx, norm1_w[l]), mc[0], mc[1])
        y_x, y_c = hybrid_mixer(hx, hc, w_in[l], gla_lr_w2[l], gla_lr_b[l], gla_norm_w[l],
                                gdn_conv_w[l], gdn_a_log[l], gdn_dt_bias[l], gdn_norm_w[l],
                                pool_w[l], pool_scale[l], w_br_gla[l], w_br_gdn[l],
                                w_br_pool[l], w_out[l], not last)
        x = x + mx[:, None, 2] * y_x
        hx2 = modulate(rmsnorm(x, norm2_w[l]), mx[:, None, 3], mx[:, None, 4])
        x = x + mx[:, None, 5] * conv_ffn(hx2, ffn_up[l], ffn_conv[l], ffn_down[l], rows, GRID_W)
        if not last:
            h_ctx = h_ctx + mc[2] * y_c
            hc2 = modulate(rmsnorm(h_ctx, norm2_w[l]), mc[3], mc[4])
            h_ctx = h_ctx + mc[5] * conv_ffn(hc2, ffn_up[l], ffn_conv[l], ffn_down[l], 1, ctx_len)
    return rmsnorm(x, final_norm_w)
```

```python
import functools

import jax
import jax.numpy as jnp
from jax import lax
from jax.experimental import pallas as pl
from jax.experimental.pallas import tpu as pltpu

D_MODEL = 2048
DEPTH = 4
GRID_W = 64
CHUNK = 64
N_DIR = 2
N_BRANCH = 3
EPS = 1e-6

GLA_HEADS = 4
GLA_DK = 128
GLA_DV = 256
GLA_RANK = 16
GLA_TAU = 16.0

GDN_HEADS = 8
GDN_DK = 128
GDN_DV = 128

POOL_WINDOWS = (2, 4, 8, 16)
POOL_GROUPS = 4
POOL_GROUP = 256

D_FF = 5632

GLA_QK = GLA_HEADS * GLA_DK
GLA_V = GLA_HEADS * GLA_DV
GDN_QK = GDN_HEADS * GDN_DK
GDN_V = GDN_HEADS * GDN_DV
POOL_WIDTH = POOL_GROUPS * POOL_GROUP
IN_SPLITS = (GLA_QK, GLA_QK, GLA_V, GLA_V, N_DIR * GLA_RANK,
             2 * GDN_QK + GDN_V, N_DIR * GDN_HEADS, N_DIR * GDN_HEADS, GDN_V,
             POOL_WIDTH, N_BRANCH * D_MODEL)
F32 = jnp.float32
BF16 = jnp.bfloat16

V7X_VMEM_BYTES = 64 * 1024 * 1024
VMEM_LIMIT_BYTES = V7X_VMEM_BYTES * 3 // 4
LANES = 128


def _split_cols(t, sizes):
    parts, start = [], 0
    for size in sizes:
        parts.append(t[..., start:start + size])
        start += size
    return parts


def _mm_kernel(x_ref, w_ref, o_ref):
    o_ref[...] = jnp.dot(x_ref[...].astype(BF16), w_ref[...],
                         preferred_element_type=F32).astype(o_ref.dtype)


def _mm_tiles(m, k, n, x_bytes):
    tm = next(t for t in (512, 256, 128, 64, 32, 16, 8) if m % t == 0)
    for tn in (2048, 1024, 512, 256, 128):
        if n % tn:
            continue
        need = 2 * (tm * k * x_bytes + k * tn * 2 + tm * tn * 4)
        if need <= VMEM_LIMIT_BYTES * 2 // 3:
            return tm, tn
    raise ValueError(f"no matmul tiling for {(m, k, n)}")


def pmm(x, w, out_dtype=F32):
    m, k = x.shape
    n = w.shape[1]
    tm, tn = _mm_tiles(m, k, n, x.dtype.itemsize)
    return pl.pallas_call(
        _mm_kernel,
        grid=(m // tm, n // tn),
        in_specs=[pl.BlockSpec((tm, k), lambda i, j: (i, 0)),
                  pl.BlockSpec((k, tn), lambda i, j: (0, j))],
        out_specs=pl.BlockSpec((tm, tn), lambda i, j: (i, j)),
        out_shape=jax.ShapeDtypeStruct((m, n), out_dtype),
        compiler_params=pltpu.CompilerParams(
            dimension_semantics=("parallel", "arbitrary"),
            vmem_limit_bytes=VMEM_LIMIT_BYTES),
        name="mm",
    )(x, w)


def mm3(t, w):
    b, s, k = t.shape
    n = w.shape[1]
    pad = -n % 512
    if pad:
        w = jnp.pad(w, ((0, 0), (0, pad)))
    return pmm(t.reshape(b * s, k), w)[:, :n].reshape(b, s, n)


def rmsnorm(t, w):
    tf = t.astype(F32)
    y = tf * lax.rsqrt(jnp.mean(tf * tf, axis=-1, keepdims=True) + EPS)
    return (y * w.astype(F32)).astype(t.dtype)


def l2norm(t):
    return t * lax.rsqrt(jnp.sum(t * t, axis=-1, keepdims=True) + EPS)


def modulate(h, shift, scale):
    return h * (1.0 + scale) + shift


def dwconv1d_centred(t, w):
    k, T = w.shape[0], t.shape[1]
    half = k // 2
    tp = jnp.pad(t, ((0, 0), (half, half), (0, 0)))
    out = tp[:, 0:T] * w[0]
    for j in range(1, k):
        out = out + tp[:, j:j + T] * w[j]
    return out


def dwconv2d_grid(t, w, rows, cols):
    B, T, C = t.shape
    img = t.reshape(B, rows, cols, C)
    y = lax.conv_general_dilated(img, w[:, :, None, :].astype(t.dtype), (1, 1), 'SAME',
                                 dimension_numbers=('NHWC', 'HWIO', 'NHWC'),
                                 feature_group_count=C)
    return y.reshape(B, T, C)


def _to_chunks(t):
    B, T, H, d = t.shape
    return t.reshape(B, T // CHUNK, CHUNK, H, d).transpose(0, 3, 1, 2, 4)


def _from_chunks(t):
    B, H, N, C, d = t.shape
    return t.transpose(0, 2, 3, 1, 4).reshape(B, N * C, H, d)


def gla_chunked(q, k, v, log_a, s0):
    q, k, v, log_a = (_to_chunks(t) for t in (q, k, v, log_a))
    b = jnp.cumsum(log_a, axis=3)
    b_last = b[:, :, :, -1:, :]
    q_dec = q * jnp.exp(b)
    scores = jnp.einsum('bhncd,bhnsd->bhncs', q_dec, k * jnp.exp(-b))
    incl = jnp.tril(jnp.ones((CHUNK, CHUNK), bool))
    scores = jnp.where(incl, scores, 0.0)
    o_intra = jnp.einsum('bhncs,bhnsv->bhncv', scores, v)
    kv = jnp.einsum('bhncd,bhncv->bhndv', k * jnp.exp(b_last - b), v)
    decay = jnp.exp(b_last[:, :, :, 0, :])

    def step(s, inp):
        d_i, kv_i = inp
        return d_i[..., None] * s + kv_i, s

    s_final, s_start = lax.scan(step, s0, (jnp.moveaxis(decay, 2, 0), jnp.moveaxis(kv, 2, 0)))
    o_inter = jnp.einsum('bhncd,nbhdv->bhncv', q_dec, s_start)
    return _from_chunks(o_intra + o_inter), s_final


def gdn_chunked(q, k, v, g, beta, s0):
    q, k, v = (_to_chunks(t) for t in (q, k, v))
    g, beta = (_to_chunks(t[..., None])[..., 0] for t in (g, beta))
    b = jnp.cumsum(g, axis=-1)
    b_last = b[..., -1:]
    incl = jnp.tril(jnp.ones((CHUNK, CHUNK), bool))
    strict = jnp.tril(jnp.ones((CHUNK, CHUNK), bool), -1)
    diff = b[..., :, None] - b[..., None, :]
    decay = jnp.where(incl, jnp.exp(jnp.where(incl, diff, 0.0)), 0.0)
    kb = k * beta[..., None]
    L = jnp.where(strict, jnp.einsum('bhncd,bhnsd->bhncs', kb, k) * decay, 0.0)
    eye = jnp.eye(CHUNK, dtype=F32)
    tinv = lax.linalg.triangular_solve(eye + L, jnp.broadcast_to(eye, L.shape),
                                       left_side=True, lower=True, unit_diagonal=True)
    u = tinv @ (v * beta[..., None])
    w = tinv @ (kb * jnp.exp(b)[..., None])
    attn = jnp.einsum('bhncd,bhnsd->bhncs', q, k) * decay
    q_dec = q * jnp.exp(b)[..., None]
    k_dec = k * jnp.exp(b_last - b)[..., None]
    chunk_decay = jnp.exp(b_last)[..., 0]

    def step(s, inp):
        u_i, w_i, attn_i, q_i, k_i, d_i = inp
        v_new = u_i - jnp.einsum('bhck,bhkv->bhcv', w_i, s)
        o_i = (jnp.einsum('bhck,bhkv->bhcv', q_i, s)
               + jnp.einsum('bhcs,bhsv->bhcv', attn_i, v_new))
        s = d_i[..., None, None] * s + jnp.einsum('bhck,bhcv->bhkv', k_i, v_new)
        return s, o_i

    xs = tuple(jnp.moveaxis(t, 2, 0) for t in (u, w, attn, q_dec, k_dec, chunk_decay))
    s_final, o = lax.scan(step, s0, xs)
    return _from_chunks(jnp.moveaxis(o, 0, 2)), s_final


def _rev(t, d):
    return t if d == 0 else jnp.flip(t, axis=1)


def bidirectional_scan(scan_fn, lat_args, lat_dir, ctx_args, ctx_dir, s0):
    out_lat, out_ctx = None, None
    for d in range(N_DIR):
        o_c, s_c = scan_fn(*[_rev(t, d) for t in ctx_args + ctx_dir[d]], s0)
        o_x, _ = scan_fn(*[_rev(t, d) for t in lat_args + lat_dir[d]], s_c)
        o_c, o_x = _rev(o_c, d), _rev(o_x, d)
        out_lat = o_x if out_lat is None else out_lat + o_x
        out_ctx = o_c if out_ctx is None else out_ctx + o_c
    return out_lat, out_ctx


def gla_prep(q, k, v, lr, w2, b):
    B, T, _ = q.shape
    q = q.astype(F32).reshape(B, T, GLA_HEADS, GLA_DK) * GLA_DK ** -0.5
    k = k.astype(F32).reshape(B, T, GLA_HEADS, GLA_DK)
    v = v.astype(F32).reshape(B, T, GLA_HEADS, GLA_DV)
    z = jnp.einsum('btdr,drk->dbtk', lr.reshape(B, T, N_DIR, GLA_RANK), w2) + b[:, None, None, :]
    log_a = jax.nn.log_sigmoid(z.astype(F32)) / GLA_TAU
    log_a = log_a.reshape(N_DIR, B, T, GLA_HEADS, GLA_DK)
    return (q, k, v), tuple((log_a[d],) for d in range(N_DIR))


def gdn_prep(qkv, a, bt, conv_w, a_log, dt_bias):
    B, T, _ = qkv.shape
    qkv = jax.nn.silu(dwconv1d_centred(qkv, conv_w)).astype(F32)
    q, k, v = _split_cols(qkv, (GDN_QK, GDN_QK, GDN_V))
    q = l2norm(q.reshape(B, T, GDN_HEADS, GDN_DK)) * GDN_DK ** -0.5
    k = l2norm(k.reshape(B, T, GDN_HEADS, GDN_DK))
    v = v.reshape(B, T, GDN_HEADS, GDN_DV)
    a = a.astype(F32).reshape(B, T, N_DIR, GDN_HEADS)
    g = -jnp.exp(a_log.astype(F32)) * jax.nn.softplus(a + dt_bias.astype(F32))
    beta = jax.nn.sigmoid(bt.astype(F32).reshape(B, T, N_DIR, GDN_HEADS))
    return (q, k, v), tuple((g[:, :, d], beta[:, :, d]) for d in range(N_DIR))


def multiscale_pool(u, pool_w, pool_scale):
    B, T, _ = u.shape
    uf = u.astype(F32)
    cs = jnp.concatenate([jnp.zeros((B, 1, POOL_WIDTH), F32), jnp.cumsum(uf, axis=1)], axis=1)
    t = jnp.arange(T)
    means = []
    for gi, win in enumerate(POOL_WINDOWS):
        lo = jnp.clip(t - win // 2, 0, T)
        hi = jnp.clip(t - win // 2 + win, 0, T)
        csg = cs[..., gi * POOL_GROUP:(gi + 1) * POOL_GROUP]
        win_sum = jnp.take(csg, hi, axis=1) - jnp.take(csg, lo, axis=1)
        means.append(win_sum / (hi - lo).astype(F32)[None, :, None])
    p = jnp.concatenate(means, axis=-1) - uf
    p = jnp.einsum('btgc,gcd->btgd', p.reshape(B, T, POOL_GROUPS, POOL_GROUP), pool_w.astype(F32))
    return p.reshape(B, T, POOL_WIDTH) * pool_scale.astype(F32)


def hybrid_mixer(hx, hc, w_in, gla_w2, gla_b, gla_nw, gdn_cw, gdn_alog, gdn_dtb, gdn_nw,
                 pool_w, pool_scale, w_br_gla, w_br_gdn, w_br_pool, w_out, need_ctx):
    dt = hx.dtype
    B = hx.shape[0]
    px = _split_cols(mm3(hx, w_in), IN_SPLITS)
    pc = _split_cols(mm3(hc, w_in), IN_SPLITS)
    ax, ax_dir = gla_prep(px[0], px[1], px[2], px[4], gla_w2, gla_b)
    ac, ac_dir = gla_prep(pc[0], pc[1], pc[2], pc[4], gla_w2, gla_b)
    s0_gla = jnp.zeros((B, GLA_HEADS, GLA_DK, GLA_DV), F32)
    o_gla_x, o_gla_c = bidirectional_scan(gla_chunked, ax, ax_dir, ac, ac_dir, s0_gla)
    bx, bx_dir = gdn_prep(px[5], px[6], px[7], gdn_cw, gdn_alog, gdn_dtb)
    bc, bc_dir = gdn_prep(pc[5], pc[6], pc[7], gdn_cw, gdn_alog, gdn_dtb)
    s0_gdn = jnp.zeros((B, GDN_HEADS, GDN_DK, GDN_DV), F32)
    o_gdn_x, o_gdn_c = bidirectional_scan(gdn_chunked, bx, bx_dir, bc, bc_dir, s0_gdn)

    def merge(p, o_gla, o_gdn):
        Bq, T, _ = p[0].shape
        z_gla = jax.nn.silu(p[3].astype(F32)).reshape(Bq, T, GLA_HEADS, GLA_DV)
        y_gla = (rmsnorm(o_gla, gla_nw) * z_gla).reshape(Bq, T, GLA_V).astype(dt)
        z_gdn = jax.nn.silu(p[8].astype(F32)).reshape(Bq, T, GDN_HEADS, GDN_DV)
        y_gdn = (rmsnorm(o_gdn, gdn_nw) * z_gdn).reshape(Bq, T, GDN_V).astype(dt)
        y_pool = multiscale_pool(p[9], pool_w, pool_scale).astype(dt)
        gates = jax.nn.sigmoid(p[10].astype(F32)).reshape(Bq, T, N_BRANCH, D_MODEL)
        m = (gates[:, :, 0] * mm3(y_gla, w_br_gla) + gates[:, :, 1] * mm3(y_gdn, w_br_gdn)
             + gates[:, :, 2] * mm3(y_pool, w_br_pool))
        return mm3(m.astype(dt), w_out)

    y_x = merge(px, o_gla_x, o_gdn_x)
    y_c = merge(pc, o_gla_c, o_gdn_c) if need_ctx else None
    return y_x, y_c


def conv_ffn(h, w_up, w_conv, w_down, rows, cols):
    a, v = _split_cols(mm3(h, w_up), (D_FF, D_FF))
    a = dwconv2d_grid(a, w_conv, rows, cols)
    return mm3((jax.nn.silu(a) * v).astype(BF16), w_down)


def kernel(x, c, ctx, c_ctx, ada_w, ada_b, norm1_w, norm2_w, w_in, gla_lr_w2, gla_lr_b,
           gla_norm_w, gdn_conv_w, gdn_a_log, gdn_dt_bias, gdn_norm_w, pool_w, pool_scale,
           w_br_gla, w_br_gdn, w_br_pool, w_out, ffn_up, ffn_conv, ffn_down, final_norm_w):
    B, T, _ = x.shape
    rows = T // GRID_W
    ctx_len = ctx.shape[1]
    h_ctx = ctx
    cond = jnp.concatenate([c, c_ctx[None, :], jnp.zeros((16 - B - 1, D_MODEL), F32)], axis=0)
    cond = jax.nn.silu(cond)
    for l in range(DEPTH):
        last = l == DEPTH - 1
        mod = pmm(cond, ada_w[l].astype(BF16)) + ada_b[l]
        mx = mod[:B].reshape(B, 6, D_MODEL)
        mc = mod[B].reshape(6, D_MODEL)
        hx = modulate(rmsnorm(x, norm1_w[l]), mx[:, None, 0], mx[:, None, 1])
        hc = modulate(rmsnorm(h_ctx, norm1_w[l]), mc[0], mc[1])
        y_x, y_c = hybrid_mixer(hx, hc, w_in[l].astype(BF16), gla_lr_w2[l], gla_lr_b[l],
                                gla_norm_w[l], gdn_conv_w[l], gdn_a_log[l], gdn_dt_bias[l],
                                gdn_norm_w[l], pool_w[l], pool_scale[l],
                                w_br_gla[l].astype(BF16), w_br_gdn[l].astype(BF16),
                                w_br_pool[l].astype(BF16), w_out[l].astype(BF16), not last)
        x = x + mx[:, None, 2] * y_x
        hx2 = modulate(rmsnorm(x, norm2_w[l]), mx[:, None, 3], mx[:, None, 4])
        w_up, w_down = ffn_up[l].astype(BF16), ffn_down[l].astype(BF16)
        x = x + mx[:, None, 5] * conv_ffn(hx2, w_up, ffn_conv[l], w_down, rows, GRID_W)
        if not last:
            h_ctx = h_ctx + mc[2] * y_c
            hc2 = modulate(rmsnorm(h_ctx, norm2_w[l]), mc[3], mc[4])
            h_ctx = h_ctx + mc[5] * conv_ffn(hc2, w_up, ffn_conv[l], w_down, 1, ctx_len)
    return rmsnorm(x, final_norm_w)
```

```python
import jax
import jax.numpy as jnp
from jax import lax
from jax.experimental import pallas as pl
from jax.experimental.pallas import tpu as pltpu

D_MODEL = 2048
DEPTH = 4
GRID_W = 64
CHUNK = 64
N_DIR = 2
N_BRANCH = 3
EPS = 1e-6

GLA_HEADS = 4
GLA_DK = 128
GLA_DV = 256
GLA_RANK = 16
GLA_TAU = 16.0

GDN_HEADS = 8
GDN_DK = 128
GDN_DV = 128

POOL_WINDOWS = (2, 4, 8, 16)
POOL_GROUPS = 4
POOL_GROUP = 256

D_FF = 5632

GLA_QK = GLA_HEADS * GLA_DK
GLA_V = GLA_HEADS * GLA_DV
GDN_QK = GDN_HEADS * GDN_DK
GDN_V = GDN_HEADS * GDN_DV
GDN_QKV = 2 * GDN_QK + GDN_V
POOL_WIDTH = POOL_GROUPS * POOL_GROUP
IN_SPLITS = (GLA_QK, GLA_QK, GLA_V, GLA_V, N_DIR * GLA_RANK,
             GDN_QKV, N_DIR * GDN_HEADS, N_DIR * GDN_HEADS, GDN_V,
             POOL_WIDTH, N_BRANCH * D_MODEL)
F32 = jnp.float32
BF16 = jnp.bfloat16
HIGHEST = lax.Precision.HIGHEST

V7X_VMEM_BYTES = 64 * 1024 * 1024
VMEM_LIMIT_BYTES = V7X_VMEM_BYTES * 3 // 4
MM_TILE_BUDGET_BYTES = V7X_VMEM_BYTES * 5 // 8
LANES = 128

SMALL_WIDTH = LANES
SMALL_COLS = N_DIR * GLA_RANK + 2 * N_DIR * GDN_HEADS


def _split_cols(t, sizes):
    parts, start = [], 0
    for size in sizes:
        parts.append(t[..., start:start + size])
        start += size
    return parts


def _mm_kernel(x_ref, w_ref, o_ref):
    o_ref[...] = jnp.dot(x_ref[...].astype(BF16), w_ref[...],
                         preferred_element_type=F32).astype(o_ref.dtype)


def _mm_tiles(m, k, n, x_bytes, o_bytes):
    best = None
    for tm in (1024, 512, 256, 128, 64, 32, 16, 8):
        if m % tm:
            continue
        for tn in (2048, 1024, 512, 256, 128):
            if n % tn:
                continue
            need = 2 * (tm * k * x_bytes + k * tn * 2 + tm * tn * o_bytes)
            if need > MM_TILE_BUDGET_BYTES:
                continue
            score = tm * tn / (tm + tn)
            if best is None or score > best[0]:
                best = (score, tm, tn)
    if best is None:
        raise ValueError(f"no matmul tiling for {(m, k, n)}")
    return best[1], best[2]


def pmm(x, w, out_dtype=F32):
    m, k = x.shape
    n = w.shape[1]
    tm, tn = _mm_tiles(m, k, n, x.dtype.itemsize, jnp.dtype(out_dtype).itemsize)
    return pl.pallas_call(
        _mm_kernel,
        grid=(m // tm, n // tn),
        in_specs=[pl.BlockSpec((tm, k), lambda i, j: (i, 0)),
                  pl.BlockSpec((k, tn), lambda i, j: (0, j))],
        out_specs=pl.BlockSpec((tm, tn), lambda i, j: (i, j)),
        out_shape=jax.ShapeDtypeStruct((m, n), out_dtype),
        compiler_params=pltpu.CompilerParams(
            dimension_semantics=("parallel", "arbitrary"),
            vmem_limit_bytes=VMEM_LIMIT_BYTES),
        name="mm",
    )(x, w)


def mm3(t, w, out_dtype=F32):
    b, s, k = t.shape
    return pmm(t.reshape(b * s, k), w, out_dtype).reshape(b, s, w.shape[1])


def _bdot(a, b):
    return jnp.dot(a.astype(BF16), b.astype(BF16), preferred_element_type=F32)


def _bdot_nt(a, b):
    return lax.dot_general(a.astype(BF16), b.astype(BF16), (((1,), (1,)), ((), ())),
                           preferred_element_type=F32)


def _bdot_tn(a, b):
    return lax.dot_general(a.astype(BF16), b.astype(BF16), (((0,), (0,)), ((), ())),
                           preferred_element_type=F32)


def _rev_chunk(t, n_ctx, n_all):
    return jnp.where(t < n_ctx, n_ctx - 1 - t, n_all + n_ctx - 1 - t)


def _chunk_masks(d):
    ri = lax.broadcasted_iota(jnp.int32, (CHUNK, CHUNK), 0)
    ci = lax.broadcasted_iota(jnp.int32, (CHUNK, CHUNK), 1)
    lower, upper = ci <= ri, ci >= ri
    if d == 0:
        return lower, ci < ri, lower.astype(F32), upper.astype(F32)
    return upper, ci > ri, upper.astype(F32), lower.astype(F32)


def _gla_kernel(qf, kf, vf, sf, qr, kr, vr, sr, w2_ref, b_ref, of, orr, st_ref):
    @pl.when(pl.program_id(1) == 0)
    def _():
        st_ref[...] = jnp.zeros_like(st_ref)

    for d, (q_ref, k_ref, v_ref, s_ref, o_ref) in enumerate(
            ((qf, kf, vf, sf, of), (qr, kr, vr, sr, orr))):
        incl, _, tri, _ = _chunk_masks(d)
        lr = s_ref[0][:, d * GLA_RANK:(d + 1) * GLA_RANK]
        z = _bdot(lr, w2_ref[d]) + b_ref[d]
        log_a = -(jnp.maximum(-z, 0.0) + jnp.log(1.0 + jnp.exp(-jnp.abs(z)))) / GLA_TAU
        bcum = jnp.dot(tri, log_a, precision=HIGHEST, preferred_element_type=F32)
        last = CHUNK - 1 if d == 0 else 0
        b_last = bcum[last:last + 1, :]
        q_all, k_all, v_all = q_ref[0], k_ref[0], v_ref[0]
        for h in range(GLA_HEADS):
            ks = slice(h * GLA_DK, (h + 1) * GLA_DK)
            vs = slice(h * GLA_DV, (h + 1) * GLA_DV)
            b = bcum[:, ks]
            bl = b_last[:, ks]
            q_dec = q_all[:, ks] * (GLA_DK ** -0.5) * jnp.exp(b)
            k = k_all[:, ks]
            v = v_all[:, vs]
            scores = jnp.where(incl, _bdot_nt(q_dec, k * jnp.exp(-b)), 0.0)
            st = st_ref[d, h]
            o_ref[0, :, vs] = _bdot(scores, v) + _bdot_nt(q_dec, st)
            st_ref[d, h] = jnp.exp(bl) * st + _bdot_tn(v, k * jnp.exp(bl - b))


def gla_scan(q, k, v, small, w2, bias, n_ctx):
    bsz, s, _ = q.shape
    n_all = s // CHUNK

    def fwd(b, t):
        return (b, t, 0)

    def rev(b, t):
        return (b, _rev_chunk(t, n_ctx, n_all), 0)

    def specs(imap):
        return [pl.BlockSpec((1, CHUNK, GLA_QK), imap), pl.BlockSpec((1, CHUNK, GLA_QK), imap),
                pl.BlockSpec((1, CHUNK, GLA_V), imap), pl.BlockSpec((1, CHUNK, SMALL_WIDTH), imap)]

    return pl.pallas_call(
        _gla_kernel,
        grid=(bsz, n_all),
        in_specs=specs(fwd) + specs(rev) + [
            pl.BlockSpec((N_DIR, GLA_RANK, GLA_QK), lambda b, t: (0, 0, 0)),
            pl.BlockSpec((N_DIR, 1, GLA_QK), lambda b, t: (0, 0, 0))],
        out_specs=[pl.BlockSpec((1, CHUNK, GLA_V), fwd), pl.BlockSpec((1, CHUNK, GLA_V), rev)],
        out_shape=[jax.ShapeDtypeStruct((bsz, s, GLA_V), F32)] * 2,
        scratch_shapes=[pltpu.VMEM((N_DIR, GLA_HEADS, GLA_DV, GLA_DK), F32)],
        compiler_params=pltpu.CompilerParams(
            dimension_semantics=("parallel", "arbitrary"),
            vmem_limit_bytes=VMEM_LIMIT_BYTES),
        name="gla_scan",
    )(q, k, v, small, q, k, v, small, w2, bias.reshape(N_DIR, 1, GLA_QK))


def _gdn_kernel(qf, kf, vf, gcf, grf, qr, kr, vr, gcr, grr, of, orr, s_ref):
    @pl.when(pl.program_id(1) == 0)
    def _():
        s_ref[...] = jnp.zeros_like(s_ref)

    ri = lax.broadcasted_iota(jnp.int32, (CHUNK, CHUNK), 0)
    ci = lax.broadcasted_iota(jnp.int32, (CHUNK, CHUNK), 1)
    eye = (ri == ci).astype(F32)
    nh = GDN_HEADS
    chains = []
    for d, (q_ref, k_ref, v_ref, gc_ref, gr_ref, o_ref) in enumerate(
            ((qf, kf, vf, gcf, grf, of), (qr, kr, vr, gcr, grr, orr))):
        incl, strict, tri, tri_t = _chunk_masks(d)
        gcol = gc_ref[0, 0]
        grow = gr_ref[0, 0]
        bcol = jnp.dot(tri, gcol[:, :N_DIR * nh], precision=HIGHEST, preferred_element_type=F32)
        brow = jnp.dot(grow[:N_DIR * nh, :], tri_t, precision=HIGHEST, preferred_element_type=F32)
        last = CHUNK - 1 if d == 0 else 0
        btot = bcol[last:last + 1, :]
        for h in range(nh):
            c = d * nh + h
            ch = dict(d=d, h=h, o_ref=o_ref, incl=incl, strict=strict)
            ch["bc"] = bcol[:, c:c + 1]
            ch["br"] = brow[c:c + 1, :]
            ch["bl"] = btot[:, c:c + 1]
            ch["beta"] = gcol[:, N_DIR * nh + c:N_DIR * nh + c + 1]
            ch["q"], ch["k"], ch["v"] = q_ref[0, h], k_ref[0, h], v_ref[0, h]
            chains.append(ch)

    for ch in chains:
        incl = ch["incl"]
        ch["decay"] = jnp.where(incl, jnp.exp(jnp.where(incl, ch["bc"] - ch["br"], 0.0)), 0.0)
        ch["kb"] = ch["k"] * ch["beta"]
        ch["a2"] = _bdot_nt(jnp.concatenate([ch["kb"], ch["q"]], axis=0), ch["k"])
    for ch in chains:
        a2 = ch.pop("a2")
        n_mat = -jnp.where(ch["strict"], a2[:CHUNK] * ch["decay"], 0.0)
        ch["attn"] = a2[CHUNK:] * ch["decay"]
        ch["p"] = eye + n_mat
        ch["n"] = _bdot(n_mat, n_mat)
    for it in range(5):
        for ch in chains:
            ch["p"] = ch["p"] + _bdot(ch["p"], ch["n"])
            if it < 4:
                ch["n"] = _bdot(ch["n"], ch["n"])
    for ch in chains:
        ch["e_b"] = jnp.exp(ch["bc"])
        uw = _bdot(ch["p"], jnp.concatenate([ch["v"] * ch["beta"], ch["kb"] * ch["e_b"]], axis=1))
        ch["u"], ch["w"] = uw[:, :GDN_DV], uw[:, GDN_DV:]
    for ch in chains:
        ch["s"] = s_ref[ch["d"], ch["h"]]
        ch["wq"] = _bdot(jnp.concatenate([ch["w"], ch["q"] * ch["e_b"]], axis=0), ch["s"])
    for ch in chains:
        wq = ch["wq"]
        v_new = ch["u"] - wq[:CHUNK]
        ch["o_ref"][0, ch["h"]] = wq[CHUNK:] + _bdot(ch["attn"], v_new)
        k_dec = ch["k"] * jnp.exp(ch["bl"] - ch["bc"])
        s_ref[ch["d"], ch["h"]] = jnp.exp(ch["bl"]) * ch["s"] + _bdot_tn(k_dec, v_new)


def gdn_scan(q, k, v, gb_col, gb_row, n_ctx):
    bsz, nh, s, dh = q.shape
    n_all = s // CHUNK

    def specs(rev):
        def row(b, t):
            return _rev_chunk(t, n_ctx, n_all) if rev else t
        qkv = pl.BlockSpec((1, nh, CHUNK, dh), lambda b, t: (b, 0, row(b, t), 0))
        return [qkv, qkv, qkv,
                pl.BlockSpec((1, 1, CHUNK, gb_col.shape[-1]), lambda b, t: (b, row(b, t), 0, 0)),
                pl.BlockSpec((1, 1, gb_row.shape[-2], CHUNK), lambda b, t: (b, row(b, t), 0, 0))]

    out_f = pl.BlockSpec((1, nh, CHUNK, dh), lambda b, t: (b, 0, t, 0))
    out_r = pl.BlockSpec((1, nh, CHUNK, dh), lambda b, t: (b, 0, _rev_chunk(t, n_ctx, n_all), 0))
    return pl.pallas_call(
        _gdn_kernel,
        grid=(bsz, n_all),
        in_specs=specs(False) + specs(True),
        out_specs=[out_f, out_r],
        out_shape=[jax.ShapeDtypeStruct((bsz, nh, s, dh), F32)] * 2,
        scratch_shapes=[pltpu.VMEM((N_DIR, nh, GDN_DK, GDN_DV), F32)],
        compiler_params=pltpu.CompilerParams(
            dimension_semantics=("parallel", "arbitrary"),
            vmem_limit_bytes=VMEM_LIMIT_BYTES),
        name="gdn_scan",
    )(q, k, v, gb_col, gb_row, q, k, v, gb_col, gb_row)


def rmsnorm(t, w):
    tf = t.astype(F32)
    y = tf * lax.rsqrt(jnp.mean(tf * tf, axis=-1, keepdims=True) + EPS)
    return (y * w.astype(F32)).astype(t.dtype)


def l2norm(t):
    return t * lax.rsqrt(jnp.sum(t * t, axis=-1, keepdims=True) + EPS)


def modulate(h, shift, scale):
    return h * (1.0 + scale) + shift


def dwconv1d_centred(t, w):
    k, T = w.shape[0], t.shape[1]
    half = k // 2
    tp = jnp.pad(t, ((0, 0), (half, half), (0, 0)))
    out = tp[:, 0:T] * w[0]
    for j in range(1, k):
        out = out + tp[:, j:j + T] * w[j]
    return out


def dwconv2d_grid(t, w, rows, cols):
    B, T, C = t.shape
    img = t.reshape(B, rows, cols, C)
    y = lax.conv_general_dilated(img, w[:, :, None, :].astype(t.dtype), (1, 1), 'SAME',
                                 dimension_numbers=('NHWC', 'HWIO', 'NHWC'),
                                 feature_group_count=C)
    return y.reshape(B, T, C)


def per_sequence(fn, t, n_ctx_rows):
    if n_ctx_rows == 0:
        return fn(t, False)
    return jnp.concatenate([fn(t[:, :n_ctx_rows], True), fn(t[:, n_ctx_rows:], False)], axis=1)


def multiscale_pool(u, pool_w, pool_scale):
    B, T, _ = u.shape
    uf = u.astype(F32)
    cs = jnp.concatenate([jnp.zeros((B, 1, POOL_WIDTH), F32), jnp.cumsum(uf, axis=1)], axis=1)
    t = jnp.arange(T)
    means = []
    for gi, win in enumerate(POOL_WINDOWS):
        lo = jnp.clip(t - win // 2, 0, T)
        hi = jnp.clip(t - win // 2 + win, 0, T)
        csg = cs[..., gi * POOL_GROUP:(gi + 1) * POOL_GROUP]
        win_sum = jnp.take(csg, hi, axis=1) - jnp.take(csg, lo, axis=1)
        means.append(win_sum / (hi - lo).astype(F32)[None, :, None])
    p = jnp.concatenate(means, axis=-1) - uf
    p = jnp.einsum('btgc,gcd->btgd', p.reshape(B, T, POOL_GROUPS, POOL_GROUP), pool_w.astype(F32))
    return p.reshape(B, T, POOL_WIDTH) * pool_scale.astype(F32)


def permute_w_in(w):
    q, k, v, zg, lr, gqkv, ga, gbt, gz, pool, gates = _split_cols(w, IN_SPLITS)
    main = jnp.concatenate([q, k, v, zg, gqkv, gz, pool, gates], axis=1)
    small = jnp.concatenate([lr, ga, gbt, jnp.zeros((w.shape[0], SMALL_WIDTH - SMALL_COLS), w.dtype)], axis=1)
    return main.astype(BF16), small.astype(BF16)


MAIN_SPLITS = (GLA_QK, GLA_QK, GLA_V, GLA_V, GDN_QKV, GDN_V, POOL_WIDTH, N_BRANCH * D_MODEL)


def hybrid_mixer(h, n_ctx_rows, out_from, w_main, w_small, gla_w2, gla_b, gla_nw, gdn_cw, gdn_alog,
                 gdn_dtb, gdn_nw, pool_w, pool_scale, w_br_gla, w_br_gdn, w_br_pool, w_out):
    B, S, _ = h.shape
    n_ctx = n_ctx_rows // CHUNK
    q_a, k_a, v_a, z_a, qkv_b, z_b, u_pool, gates = _split_cols(mm3(h, w_main), MAIN_SPLITS)
    small = mm3(h, w_small)
    o_f, o_r = gla_scan(q_a, k_a, v_a, small, gla_w2.astype(BF16), gla_b, n_ctx)
    o_gla = (o_f + o_r)[:, out_from:].reshape(B, S - out_from, GLA_HEADS, GLA_DV)
    qkv = per_sequence(lambda t, _: jax.nn.silu(dwconv1d_centred(t, gdn_cw)), qkv_b, n_ctx_rows)
    q, k, v = _split_cols(qkv, (GDN_QK, GDN_QK, GDN_V))
    q = l2norm(q.reshape(B, S, GDN_HEADS, GDN_DK)) * GDN_DK ** -0.5
    k = l2norm(k.reshape(B, S, GDN_HEADS, GDN_DK))
    v = v.reshape(B, S, GDN_HEADS, GDN_DV)
    a = small[..., N_DIR * GLA_RANK:N_DIR * GLA_RANK + N_DIR * GDN_HEADS]
    bt = small[..., N_DIR * GLA_RANK + N_DIR * GDN_HEADS:SMALL_COLS]
    g = (-jnp.exp(gdn_alog.astype(F32)) * jax.nn.softplus(
        a.reshape(B, S, N_DIR, GDN_HEADS) + gdn_dtb.astype(F32))).reshape(B, S, N_DIR * GDN_HEADS)
    gb = jnp.concatenate([g, jax.nn.sigmoid(bt)], axis=-1).reshape(B, S // CHUNK, CHUNK, -1)
    hm = lambda t: t.transpose(0, 2, 1, 3)
    o_f, o_r = gdn_scan(hm(q), hm(k), hm(v), gb, gb.transpose(0, 1, 3, 2), n_ctx)
    o_gdn = hm(o_f + o_r)[:, out_from:]
    dt = h.dtype
    Sq = S - out_from
    z_gla = jax.nn.silu(z_a[:, out_from:]).reshape(B, Sq, GLA_HEADS, GLA_DV)
    y_gla = (rmsnorm(o_gla, gla_nw) * z_gla).reshape(B, Sq, GLA_V)
    z_gdn = jax.nn.silu(z_b[:, out_from:]).reshape(B, Sq, GDN_HEADS, GDN_DV)
    y_gdn = (rmsnorm(o_gdn, gdn_nw) * z_gdn).reshape(B, Sq, GDN_V)
    y_pool = per_sequence(lambda t, _: multiscale_pool(t, pool_w, pool_scale),
                          u_pool[:, out_from:], n_ctx_rows - out_from)
    gt = jax.nn.sigmoid(gates[:, out_from:]).reshape(B, Sq, N_BRANCH, D_MODEL)
    m = (gt[:, :, 0] * mm3(y_gla, w_br_gla) + gt[:, :, 1] * mm3(y_gdn, w_br_gdn)
         + gt[:, :, 2] * mm3(y_pool, w_br_pool))
    return mm3(m.astype(dt), w_out)


def conv_ffn(h, n_ctx_rows, w_up, w_conv, w_down):
    a, v = _split_cols(mm3(h, w_up), (D_FF, D_FF))

    def conv(t, is_ctx):
        rows = 1 if is_ctx else t.shape[1] // GRID_W
        return dwconv2d_grid(t, w_conv, rows, t.shape[1] // rows)

    a = per_sequence(conv, a, n_ctx_rows)
    return mm3((jax.nn.silu(a) * v).astype(BF16), w_down)


def kernel(x, c, ctx, c_ctx, ada_w, ada_b, norm1_w, norm2_w, w_in, gla_lr_w2, gla_lr_b,
           gla_norm_w, gdn_conv_w, gdn_a_log, gdn_dt_bias, gdn_norm_w, pool_w, pool_scale,
           w_br_gla, w_br_gdn, w_br_pool, w_out, ffn_up, ffn_conv, ffn_down, final_norm_w):
    B, T, _ = x.shape
    ctx_len = ctx.shape[1]
    xa = jnp.concatenate([ctx, x], axis=1)
    is_ctx = (jnp.arange(ctx_len + T) < ctx_len)[None, :, None]
    cond = jnp.concatenate([c, c_ctx[None, :], jnp.zeros((16 - B - 1, D_MODEL), F32)], axis=0)
    cond = jax.nn.silu(cond)
    for l in range(DEPTH):
        last = l == DEPTH - 1
        mod = pmm(cond, ada_w[l].astype(BF16)) + ada_b[l]
        mx = mod[:B].reshape(B, 1, 6, D_MODEL)
        mc = mod[B].reshape(1, 1, 6, D_MODEL)

        def mod_rows(i, lo):
            return jnp.where(is_ctx[:, lo:], mc[:, :, i], mx[:, :, i])

        h = modulate(rmsnorm(xa, norm1_w[l]), mod_rows(0, 0), mod_rows(1, 0))
        w_main, w_small = permute_w_in(w_in[l])
        lo = ctx_len if last else 0
        y = hybrid_mixer(h, ctx_len, lo, w_main, w_small, gla_lr_w2[l], gla_lr_b[l], gla_norm_w[l],
                         gdn_conv_w[l], gdn_a_log[l], gdn_dt_bias[l], gdn_norm_w[l],
                         pool_w[l], pool_scale[l], w_br_gla[l].astype(BF16),
                         w_br_gdn[l].astype(BF16), w_br_pool[l].astype(BF16), w_out[l].astype(BF16))
        xa = xa[:, lo:] + mod_rows(2, lo) * y
        h2 = modulate(rmsnorm(xa, norm2_w[l]), mod_rows(3, lo), mod_rows(4, lo))
        xa = xa + mod_rows(5, lo) * conv_ffn(h2, ctx_len - lo, ffn_up[l].astype(BF16), ffn_conv[l],
                                             ffn_down[l].astype(BF16))
    return rmsnorm(xa, final_norm_w)
```

```python
import functools

import jax
import jax.numpy as jnp
from jax import lax
from jax.experimental import pallas as pl
from jax.experimental.pallas import tpu as pltpu

D_MODEL = 2048
DEPTH = 4
GRID_W = 64
CHUNK = 64
N_DIR = 2
N_BRANCH = 3
EPS = 1e-6

GLA_HEADS = 4
GLA_DK = 128
GLA_DV = 256
GLA_RANK = 16
GLA_TAU = 16.0

GDN_HEADS = 8
GDN_DK = 128
GDN_DV = 128

POOL_WINDOWS = (2, 4, 8, 16)
POOL_GROUPS = 4
POOL_GROUP = 256

D_FF = 5632

GLA_QK = GLA_HEADS * GLA_DK
GLA_V = GLA_HEADS * GLA_DV
GDN_QK = GDN_HEADS * GDN_DK
GDN_V = GDN_HEADS * GDN_DV
GDN_QKV = 2 * GDN_QK + GDN_V
POOL_WIDTH = POOL_GROUPS * POOL_GROUP
IN_SPLITS = (GLA_QK, GLA_QK, GLA_V, GLA_V, N_DIR * GLA_RANK,
             GDN_QKV, N_DIR * GDN_HEADS, N_DIR * GDN_HEADS, GDN_V,
             POOL_WIDTH, N_BRANCH * D_MODEL)
F32 = jnp.float32
BF16 = jnp.bfloat16
HIGHEST = lax.Precision.HIGHEST

V7X_VMEM_BYTES = 64 * 1024 * 1024
VMEM_LIMIT_BYTES = V7X_VMEM_BYTES * 3 // 4
MM_TILE_BUDGET_BYTES = V7X_VMEM_BYTES * 5 // 8
LANES = 128
BF16_SUBLANES = 16

SMALL_WIDTH = LANES
SMALL_COLS = N_DIR * GLA_RANK + 2 * N_DIR * GDN_HEADS
A_COL = N_DIR * GLA_RANK
BT_COL = A_COL + N_DIR * GDN_HEADS
MAIN_SPLITS = (GLA_QK, GLA_QK, GLA_V, GLA_V, GDN_QKV, GDN_V, POOL_WIDTH, N_BRANCH * D_MODEL)
MAIN_WIDTH = sum(MAIN_SPLITS)
GDN_QKV_COL = 2 * GLA_QK + 2 * GLA_V

ROW_TILE = 256


def _split_cols(t, sizes):
    parts, start = [], 0
    for size in sizes:
        parts.append(t[..., start:start + size])
        start += size
    return parts


def _params(*sem):
    return pltpu.CompilerParams(dimension_semantics=sem, vmem_limit_bytes=VMEM_LIMIT_BYTES)


def _sigmoid(x):
    return 1.0 / (1.0 + jnp.exp(-x))


def _softplus(x):
    return jnp.maximum(x, 0.0) + jnp.log(1.0 + jnp.exp(-jnp.abs(x)))


def _mm_kernel(x_ref, w_ref, o_ref):
    o_ref[...] = jnp.dot(x_ref[...].astype(BF16), w_ref[...],
                         preferred_element_type=F32).astype(o_ref.dtype)


def _mm_tiles(m, k, n, x_bytes, o_bytes):
    best = None
    for tm in (1024, 512, 256, 128, 64, 32, 16, 8):
        if m % tm:
            continue
        for tn in (2048, 1024, 512, 256, 128):
            if n % tn:
                continue
            need = 2 * (tm * k * x_bytes + k * tn * 2 + tm * tn * o_bytes)
            if need > MM_TILE_BUDGET_BYTES:
                continue
            score = tm * tn / (tm + tn)
            if best is None or score > best[0]:
                best = (score, tm, tn)
    if best is None:
        raise ValueError(f"no matmul tiling for {(m, k, n)}")
    return best[1], best[2]


def pmm(x, w, out_dtype=F32):
    m, k = x.shape
    n = w.shape[1]
    tm, tn = _mm_tiles(m, k, n, x.dtype.itemsize, jnp.dtype(out_dtype).itemsize)
    return pl.pallas_call(
        _mm_kernel,
        grid=(m // tm, n // tn),
        in_specs=[pl.BlockSpec((tm, k), lambda i, j: (i, 0)),
                  pl.BlockSpec((k, tn), lambda i, j: (0, j))],
        out_specs=pl.BlockSpec((tm, tn), lambda i, j: (i, j)),
        out_shape=jax.ShapeDtypeStruct((m, n), out_dtype),
        compiler_params=_params("parallel", "arbitrary"),
        name="mm",
    )(x, w)


class Rows:
    def __init__(self, batch, has_ctx, lat_tiles):
        self.batch, self.has_ctx, self.lat_tiles = batch, has_ctx, lat_tiles
        self.per_batch = lat_tiles + (1 if has_ctx else 0)
        self.n_tiles = batch * self.per_batch

    def block_of(self, i, other):
        if other.has_ctx == self.has_ctx:
            return i
        assert other.has_ctx and not self.has_ctx
        return (i // self.per_batch) * other.per_batch + 1 + i % self.per_batch

    def mod_row(self, i):
        b = i // self.per_batch
        if not self.has_ctx:
            return b
        return jnp.where(i % self.per_batch == 0, self.batch, b)

    def flags(self, i):
        r = i % self.per_batch
        if not self.has_ctx:
            return False, r == 0, r == self.per_batch - 1
        is_ctx = r == 0
        return is_ctx, is_ctx | (r == 1), is_ctx | (r == self.per_batch - 1)


def _norm_kernel(*refs, has_resid, gate_idx, has_mod, shift_idx, scale_idx, write_x):
    refs = list(refs)
    x_ref = refs.pop(0)
    x = x_ref[...]
    if has_resid:
        y_ref, gmod_ref = refs.pop(0), refs.pop(0)
        x = x + gmod_ref[0, gate_idx:gate_idx + 1, :] * y_ref[...].astype(F32)
    nw_ref = refs.pop(0)
    mod_ref = refs.pop(0) if has_mod else None
    if write_x:
        refs.pop(0)[...] = x
    h_ref = refs.pop(0)
    y = x * lax.rsqrt(jnp.mean(x * x, axis=-1, keepdims=True) + EPS) * nw_ref[...]
    if has_mod:
        y = y * (1.0 + mod_ref[0, scale_idx:scale_idx + 1, :]) + mod_ref[0, shift_idx:shift_idx + 1, :]
    h_ref[...] = y.astype(h_ref.dtype)


def resid_norm(x, x_rows, rows, norm_w, *, resid=None, gate_mod=None, gate_idx=0,
               mod=None, shift_idx=0, scale_idx=1, out_dtype=BF16):
    d = x.shape[-1]
    row = lambda i: (i, 0)
    modspec = pl.BlockSpec((1, 6, d), lambda i: (rows.mod_row(i), 0, 0))
    args = [x]
    in_specs = [pl.BlockSpec((ROW_TILE, d), lambda i: (rows.block_of(i, x_rows), 0))]
    if resid is not None:
        args += [resid, gate_mod]
        in_specs += [pl.BlockSpec((ROW_TILE, d), row), modspec]
    args.append(norm_w.reshape(1, d))
    in_specs.append(pl.BlockSpec((1, d), lambda i: (0, 0)))
    if mod is not None:
        args.append(mod)
        in_specs.append(modspec)
    m = rows.n_tiles * ROW_TILE
    out_shape, out_specs = [], []
    if resid is not None:
        out_shape.append(jax.ShapeDtypeStruct((m, d), F32))
        out_specs.append(pl.BlockSpec((ROW_TILE, d), row))
    out_shape.append(jax.ShapeDtypeStruct((m, d), out_dtype))
    out_specs.append(pl.BlockSpec((ROW_TILE, d), row))
    outs = pl.pallas_call(
        functools.partial(_norm_kernel, has_resid=resid is not None, gate_idx=gate_idx,
                          has_mod=mod is not None, shift_idx=shift_idx, scale_idx=scale_idx,
                          write_x=resid is not None),
        grid=(rows.n_tiles,), in_specs=in_specs, out_specs=out_specs, out_shape=out_shape,
        compiler_params=_params("parallel"), name="resid_norm",
    )(*args)
    return (outs[0], outs[1]) if resid is not None else (None, outs[0])


def _gdn_prep_kernel(x_ref, up_ref, dn_ref, w_ref, o_ref, ext_ref, *, rows):
    halo = BF16_SUBLANES
    _, first, last = rows.flags(pl.program_id(0))
    ext_ref[0:halo] = jnp.where(first, 0.0, up_ref[...].astype(F32))
    ext_ref[halo:halo + ROW_TILE] = x_ref[...].astype(F32)
    ext_ref[halo + ROW_TILE:2 * halo + ROW_TILE] = jnp.where(last, 0.0, dn_ref[...].astype(F32))
    y = (ext_ref[pl.ds(halo - 1, ROW_TILE), :] * w_ref[0:1, :]
         + ext_ref[pl.ds(halo, ROW_TILE), :] * w_ref[1:2, :]
         + ext_ref[pl.ds(halo + 1, ROW_TILE), :] * w_ref[2:3, :])
    y = y * _sigmoid(y)
    for hd in range(2 * GDN_HEADS):
        sl = slice(hd * GDN_DK, (hd + 1) * GDN_DK)
        seg = y[:, sl]
        inv = lax.rsqrt(jnp.sum(seg * seg, axis=-1, keepdims=True) + EPS)
        if hd < GDN_HEADS:
            inv = inv * GDN_DK ** -0.5
        o_ref[:, sl] = (seg * inv).astype(o_ref.dtype)
    o_ref[:, 2 * GDN_QK:] = y[:, 2 * GDN_QK:].astype(o_ref.dtype)


def gdn_prep(p, rows, conv_w):
    m = p.shape[0]
    cb = GDN_QKV_COL // GDN_QKV
    hb = ROW_TILE // BF16_SUBLANES
    n_hblocks = m // BF16_SUBLANES
    return pl.pallas_call(
        functools.partial(_gdn_prep_kernel, rows=rows),
        grid=(rows.n_tiles,),
        in_specs=[pl.BlockSpec((ROW_TILE, GDN_QKV), lambda i: (i, cb)),
                  pl.BlockSpec((BF16_SUBLANES, GDN_QKV), lambda i: (jnp.maximum(i * hb - 1, 0), cb)),
                  pl.BlockSpec((BF16_SUBLANES, GDN_QKV),
                               lambda i: (jnp.minimum((i + 1) * hb, n_hblocks - 1), cb)),
                  pl.BlockSpec((3, GDN_QKV), lambda i: (0, 0))],
        out_specs=pl.BlockSpec((ROW_TILE, GDN_QKV), lambda i: (i, 0)),
        out_shape=jax.ShapeDtypeStruct((m, GDN_QKV), BF16),
        scratch_shapes=[pltpu.VMEM((ROW_TILE + 2 * BF16_SUBLANES, GDN_QKV), F32)],
        compiler_params=_params("parallel"), name="gdn_prep",
    )(p, p, p, conv_w)


def _bdot(a, b):
    return jnp.dot(a.astype(BF16), b.astype(BF16), preferred_element_type=F32)


def _bdot_nt(a, b):
    return lax.dot_general(a.astype(BF16), b.astype(BF16), (((1,), (1,)), ((), ())),
                           preferred_element_type=F32)


def _bdot_tn(a, b):
    return lax.dot_general(a.astype(BF16), b.astype(BF16), (((0,), (0,)), ((), ())),
                           preferred_element_type=F32)


def _rev_chunk(t, n_ctx, n_all):
    return jnp.where(t < n_ctx, n_ctx - 1 - t, n_all + n_ctx - 1 - t)


def _chunk_masks(d):
    ri = lax.broadcasted_iota(jnp.int32, (CHUNK, CHUNK), 0)
    ci = lax.broadcasted_iota(jnp.int32, (CHUNK, CHUNK), 1)
    lower, upper = ci <= ri, ci >= ri
    if d == 0:
        return lower, ci < ri, lower.astype(F32), upper.astype(F32)
    return upper, ci > ri, upper.astype(F32), lower.astype(F32)


def _gla_kernel(qf, kf, vf, sf, qr, kr, vr, sr, w2_ref, b_ref, of, orr, st_ref):
    @pl.when(pl.program_id(1) == 0)
    def _():
        st_ref[...] = jnp.zeros_like(st_ref)

    for d, (q_ref, k_ref, v_ref, s_ref, o_ref) in enumerate(
            ((qf, kf, vf, sf, of), (qr, kr, vr, sr, orr))):
        incl, _, tri, _ = _chunk_masks(d)
        lr = s_ref[0][:, d * GLA_RANK:(d + 1) * GLA_RANK]
        z = _bdot(lr, w2_ref[d]) + b_ref[d]
        log_a = -_softplus(-z) / GLA_TAU
        bcum = jnp.dot(tri, log_a, precision=HIGHEST, preferred_element_type=F32)
        last = CHUNK - 1 if d == 0 else 0
        b_last = bcum[last:last + 1, :]
        q_all, k_all, v_all = q_ref[0].astype(F32), k_ref[0].astype(F32), v_ref[0]
        for h in range(GLA_HEADS):
            ks = slice(h * GLA_DK, (h + 1) * GLA_DK)
            vs = slice(h * GLA_DV, (h + 1) * GLA_DV)
            b = bcum[:, ks]
            bl = b_last[:, ks]
            q_dec = q_all[:, ks] * (GLA_DK ** -0.5) * jnp.exp(b)
            k = k_all[:, ks]
            v = v_all[:, vs]
            scores = jnp.where(incl, _bdot_nt(q_dec, k * jnp.exp(-b)), 0.0)
            st = st_ref[d, h]
            o_ref[0, :, vs] = _bdot(scores, v) + _bdot_nt(q_dec, st)
            st_ref[d, h] = jnp.exp(bl) * st + _bdot_tn(v, k * jnp.exp(bl - b))


def gla_scan(p, small, w2, bias, n_ctx):
    bsz, s, _ = p.shape
    n_all = s // CHUNK

    def specs(rev):
        def row(t):
            return _rev_chunk(t, n_ctx, n_all) if rev else t
        return [pl.BlockSpec((1, CHUNK, GLA_QK), lambda b, t: (b, row(t), 0)),
                pl.BlockSpec((1, CHUNK, GLA_QK), lambda b, t: (b, row(t), 1)),
                pl.BlockSpec((1, CHUNK, GLA_V), lambda b, t: (b, row(t), 2 * GLA_QK // GLA_V)),
                pl.BlockSpec((1, CHUNK, SMALL_WIDTH), lambda b, t: (b, row(t), 0))]

    out_f = pl.BlockSpec((1, CHUNK, GLA_V), lambda b, t: (b, t, 0))
    out_r = pl.BlockSpec((1, CHUNK, GLA_V), lambda b, t: (b, _rev_chunk(t, n_ctx, n_all), 0))
    return pl.pallas_call(
        _gla_kernel,
        grid=(bsz, n_all),
        in_specs=specs(False) + specs(True) + [
            pl.BlockSpec((N_DIR, GLA_RANK, GLA_QK), lambda b, t: (0, 0, 0)),
            pl.BlockSpec((N_DIR, 1, GLA_QK), lambda b, t: (0, 0, 0))],
        out_specs=[out_f, out_r],
        out_shape=[jax.ShapeDtypeStruct((bsz, s, GLA_V), F32)] * 2,
        scratch_shapes=[pltpu.VMEM((N_DIR, GLA_HEADS, GLA_DV, GLA_DK), F32)],
        compiler_params=_params("parallel", "arbitrary"),
        name="gla_scan",
    )(p, p, p, small, p, p, p, small, w2, bias.reshape(N_DIR, 1, GLA_QK))


def _gdn_kernel(qf, kf, vf, sf, qr, kr, vr, sr, alog_ref, dtb_ref, of, orr, s_ref):
    @pl.when(pl.program_id(1) == 0)
    def _():
        s_ref[...] = jnp.zeros_like(s_ref)

    ri = lax.broadcasted_iota(jnp.int32, (CHUNK, CHUNK), 0)
    ci = lax.broadcasted_iota(jnp.int32, (CHUNK, CHUNK), 1)
    eye = (ri == ci).astype(F32)
    same16 = (ri >> 4) == (ci >> 4)
    same32 = (ri >> 5) == (ci >> 5)
    nh = GDN_HEADS
    chains = []
    for d, (q_ref, k_ref, v_ref, sm_ref, o_ref) in enumerate(
            ((qf, kf, vf, sf, of), (qr, kr, vr, sr, orr))):
        incl, strict, tri, tri_t = _chunk_masks(d)
        sm = sm_ref[0]
        g_all = -jnp.exp(alog_ref[...]) * _softplus(sm + dtb_ref[...])
        beta_all = _sigmoid(sm)
        gcol = g_all[:, A_COL:A_COL + N_DIR * nh]
        grow = g_all.T[A_COL:A_COL + N_DIR * nh, :]
        bcol = jnp.dot(tri, gcol, precision=HIGHEST, preferred_element_type=F32)
        brow = jnp.dot(grow, tri_t, precision=HIGHEST, preferred_element_type=F32)
        last = CHUNK - 1 if d == 0 else 0
        btot = bcol[last:last + 1, :]
        q_all, k_all, v_all = q_ref[0].astype(F32), k_ref[0].astype(F32), v_ref[0].astype(F32)
        for h in range(nh):
            c = d * nh + h
            hs = slice(h * GDN_DK, (h + 1) * GDN_DK)
            ch = dict(d=d, h=h, hs=hs, o_ref=o_ref, incl=incl, strict=strict)
            ch["bc"] = bcol[:, c:c + 1]
            ch["br"] = brow[c:c + 1, :]
            ch["bl"] = btot[:, c:c + 1]
            ch["beta"] = beta_all[:, BT_COL + c:BT_COL + c + 1]
            ch["q"], ch["k"], ch["v"] = q_all[:, hs], k_all[:, hs], v_all[:, hs]
            chains.append(ch)

    for ch in chains:
        incl = ch["incl"]
        ch["decay"] = jnp.where(incl, jnp.exp(jnp.where(incl, ch["bc"] - ch["br"], 0.0)), 0.0)
        ch["kb"] = ch["k"] * ch["beta"]
        ch["a2"] = _bdot_nt(jnp.concatenate([ch["kb"], ch["q"]], axis=0), ch["k"])
    for ch in chains:
        a2 = ch.pop("a2")
        n_mat = -jnp.where(ch["strict"], a2[:CHUNK] * ch["decay"], 0.0)
        ch["attn"] = a2[CHUNK:] * ch["decay"]
        n_diag = jnp.where(same16, n_mat, 0.0)
        ch["n32"] = jnp.where(same32, n_mat, 0.0) - n_diag
        ch["n64"] = jnp.where(same32, 0.0, n_mat)
        ch["p"] = eye + n_diag
        ch["n"] = _bdot(n_diag, n_diag)
    for it in range(3):
        for ch in chains:
            ch["p"] = ch["p"] + _bdot(ch["p"], ch["n"])
            if it < 2:
                ch["n"] = _bdot(ch["n"], ch["n"])
    for off in ("n32", "n64"):
        for ch in chains:
            ch["x"] = _bdot(ch["p"], ch[off])
        for ch in chains:
            ch["p"] = ch["p"] + _bdot(ch["x"], ch["p"])
    for ch in chains:
        ch["e_b"] = jnp.exp(ch["bc"])
        uw = _bdot(ch["p"], jnp.concatenate([ch["v"] * ch["beta"], ch["kb"] * ch["e_b"]], axis=1))
        ch["u"], ch["w"] = uw[:, :GDN_DV], uw[:, GDN_DV:]
    for ch in chains:
        ch["s"] = s_ref[ch["d"], ch["h"]]
        ch["wq"] = _bdot(jnp.concatenate([ch["w"], ch["q"] * ch["e_b"]], axis=0), ch["s"])
    for ch in chains:
        wq = ch["wq"]
        v_new = ch["u"] - wq[:CHUNK]
        ch["o_ref"][0, :, ch["hs"]] = wq[CHUNK:] + _bdot(ch["attn"], v_new)
        k_dec = ch["k"] * jnp.exp(ch["bl"] - ch["bc"])
        s_ref[ch["d"], ch["h"]] = jnp.exp(ch["bl"]) * ch["s"] + _bdot_tn(k_dec, v_new)


def gdn_scan(qkv, small, a_log, dt_bias, n_ctx):
    bsz, s, _ = qkv.shape
    n_all = s // CHUNK
    lane_row = lambda t: jnp.zeros((1, SMALL_WIDTH), F32).at[0, A_COL:A_COL + N_DIR * GDN_HEADS].set(
        t.astype(F32).reshape(-1))

    def specs(rev):
        def row(t):
            return _rev_chunk(t, n_ctx, n_all) if rev else t
        return [pl.BlockSpec((1, CHUNK, GDN_QK), lambda b, t: (b, row(t), 0)),
                pl.BlockSpec((1, CHUNK, GDN_QK), lambda b, t: (b, row(t), 1)),
                pl.BlockSpec((1, CHUNK, GDN_V), lambda b, t: (b, row(t), 2)),
                pl.BlockSpec((1, CHUNK, SMALL_WIDTH), lambda b, t: (b, row(t), 0))]

    const = pl.BlockSpec((1, SMALL_WIDTH), lambda b, t: (0, 0))
    out_f = pl.BlockSpec((1, CHUNK, GDN_V), lambda b, t: (b, t, 0))
    out_r = pl.BlockSpec((1, CHUNK, GDN_V), lambda b, t: (b, _rev_chunk(t, n_ctx, n_all), 0))
    return pl.pallas_call(
        _gdn_kernel,
        grid=(bsz, n_all),
        in_specs=specs(False) + specs(True) + [const, const],
        out_specs=[out_f, out_r],
        out_shape=[jax.ShapeDtypeStruct((bsz, s, GDN_V), F32)] * 2,
        scratch_shapes=[pltpu.VMEM((N_DIR, GDN_HEADS, GDN_DK, GDN_DV), F32)],
        compiler_params=_params("parallel", "arbitrary"),
        name="gdn_scan",
    )(qkv, qkv, qkv, small, qkv, qkv, qkv, small, lane_row(a_log), lane_row(dt_bias))


FFN_COL_TILE = 512


def _ffn_gate_kernel(a_ref, up_ref, dn_ref, v_ref, w_ref, o_ref, ext_ref, *, rows):
    pad, g = 8, GRID_W
    base = pad + g
    is_ctx, first, last = rows.flags(pl.program_id(0))
    zeros = jnp.zeros((pad, FFN_COL_TILE), F32)
    ext_ref[0:pad] = zeros
    ext_ref[base + ROW_TILE + g:base + ROW_TILE + g + pad] = zeros
    ext_ref[pad:base] = jnp.where(first, 0.0, up_ref[...].astype(F32))
    ext_ref[base:base + ROW_TILE] = a_ref[...].astype(F32)
    ext_ref[base + ROW_TILE:base + ROW_TILE + g] = jnp.where(last, 0.0, dn_ref[...].astype(F32))
    t = lax.broadcasted_iota(jnp.int32, (ROW_TILE, 1), 0)
    col = jnp.where(is_ctx, t, t & (g - 1))
    left_ok = col != 0
    right_ok = col != jnp.where(is_ctx, ROW_TILE - 1, g - 1)
    vert = jnp.where(is_ctx, 0.0, 1.0)
    acc = None
    for dr in (-1, 0, 1):
        for dc in (-1, 0, 1):
            val = ext_ref[pl.ds(base + dr * g + dc, ROW_TILE), :]
            if dc == -1:
                val = jnp.where(left_ok, val, 0.0)
            elif dc == 1:
                val = jnp.where(right_ok, val, 0.0)
            wt = w_ref[3 * (dr + 1) + dc + 1:3 * (dr + 1) + dc + 2, :]
            if dr != 0:
                wt = wt * vert
            acc = val * wt if acc is None else acc + val * wt
    o_ref[...] = (acc * _sigmoid(acc) * v_ref[...].astype(F32)).astype(o_ref.dtype)


def ffn_gate(up, rows, w_conv):
    m = up.shape[0]
    nc = D_FF // FFN_COL_TILE
    gb = ROW_TILE // GRID_W
    n_gblocks = m // GRID_W
    return pl.pallas_call(
        functools.partial(_ffn_gate_kernel, rows=rows),
        grid=(rows.n_tiles, nc),
        in_specs=[pl.BlockSpec((ROW_TILE, FFN_COL_TILE), lambda i, j: (i, j)),
                  pl.BlockSpec((GRID_W, FFN_COL_TILE), lambda i, j: (jnp.maximum(i * gb - 1, 0), j)),
                  pl.BlockSpec((GRID_W, FFN_COL_TILE),
                               lambda i, j: (jnp.minimum((i + 1) * gb, n_gblocks - 1), j)),
                  pl.BlockSpec((ROW_TILE, FFN_COL_TILE), lambda i, j: (i, nc + j)),
                  pl.BlockSpec((9, FFN_COL_TILE), lambda i, j: (0, j))],
        out_specs=pl.BlockSpec((ROW_TILE, FFN_COL_TILE), lambda i, j: (i, j)),
        out_shape=jax.ShapeDtypeStruct((m, D_FF), BF16),
        scratch_shapes=[pltpu.VMEM((ROW_TILE + 2 * GRID_W + 16, FFN_COL_TILE), F32)],
        compiler_params=_params("parallel", "arbitrary"), name="ffn_gate",
    )(up, up, up, up, w_conv.reshape(9, D_FF))


def rmsnorm(t, w):
    tf = t.astype(F32)
    y = tf * lax.rsqrt(jnp.mean(tf * tf, axis=-1, keepdims=True) + EPS)
    return (y * w.astype(F32)).astype(t.dtype)


def per_sequence(fn, t, n_ctx_rows):
    if n_ctx_rows == 0:
        return fn(t, False)
    return jnp.concatenate([fn(t[:, :n_ctx_rows], True), fn(t[:, n_ctx_rows:], False)], axis=1)


def multiscale_pool(u, pool_w, pool_scale):
    B, T, _ = u.shape
    uf = u.astype(F32)
    cs = jnp.concatenate([jnp.zeros((B, 1, POOL_WIDTH), F32), jnp.cumsum(uf, axis=1)], axis=1)
    t = jnp.arange(T)
    means = []
    for gi, win in enumerate(POOL_WINDOWS):
        lo = jnp.clip(t - win // 2, 0, T)
        hi = jnp.clip(t - win // 2 + win, 0, T)
        csg = cs[..., gi * POOL_GROUP:(gi + 1) * POOL_GROUP]
        win_sum = jnp.take(csg, hi, axis=1) - jnp.take(csg, lo, axis=1)
        means.append(win_sum / (hi - lo).astype(F32)[None, :, None])
    p = jnp.concatenate(means, axis=-1) - uf
    p = jnp.einsum('btgc,gcd->btgd', p.reshape(B, T, POOL_GROUPS, POOL_GROUP), pool_w.astype(F32))
    return p.reshape(B, T, POOL_WIDTH) * pool_scale.astype(F32)


def permute_w_in(w):
    q, k, v, zg, lr, gqkv, ga, gbt, gz, pool, gates = _split_cols(w, IN_SPLITS)
    main = jnp.concatenate([q, k, v, zg, gqkv, gz, pool, gates], axis=1)
    small = jnp.concatenate([lr, ga, gbt, jnp.zeros((w.shape[0], SMALL_WIDTH - SMALL_COLS), w.dtype)], axis=1)
    return main.astype(BF16), small.astype(BF16)


def hybrid_mixer(h, rows, out_rows, bsz, w_main, w_small, gla_w2, gla_b, gla_nw, gdn_cw, gdn_alog,
                 gdn_dtb, gdn_nw, pool_w, pool_scale, w_br_gla, w_br_gdn, w_br_pool, w_out):
    m = h.shape[0]
    s = m // bsz
    n_ctx_rows = ROW_TILE if rows.has_ctx else 0
    out_from = n_ctx_rows if not out_rows.has_ctx else 0
    p = pmm(h, w_main, BF16)
    small = pmm(h, w_small)
    p3, small3 = p.reshape(bsz, s, MAIN_WIDTH), small.reshape(bsz, s, SMALL_WIDTH)
    n_ctx = n_ctx_rows // CHUNK
    o_f, o_r = gla_scan(p3, small3, gla_w2.astype(BF16), gla_b, n_ctx)
    qkv = gdn_prep(p, rows, gdn_cw).reshape(bsz, s, GDN_QKV)
    g_f, g_r = gdn_scan(qkv, small3, gdn_alog, gdn_dtb, n_ctx)
    sq = s - out_from
    _, _, _, z_a, _, z_b, u_pool, gates = _split_cols(p3[:, out_from:].astype(F32), MAIN_SPLITS)
    o_gla = (o_f + o_r)[:, out_from:].reshape(bsz, sq, GLA_HEADS, GLA_DV)
    o_gdn = (g_f + g_r)[:, out_from:].reshape(bsz, sq, GDN_HEADS, GDN_DV)
    z_gla = jax.nn.silu(z_a).reshape(bsz, sq, GLA_HEADS, GLA_DV)
    y_gla = (rmsnorm(o_gla, gla_nw) * z_gla).reshape(bsz * sq, GLA_V)
    z_gdn = jax.nn.silu(z_b).reshape(bsz, sq, GDN_HEADS, GDN_DV)
    y_gdn = (rmsnorm(o_gdn, gdn_nw) * z_gdn).reshape(bsz * sq, GDN_V)
    y_pool = per_sequence(lambda t, _: multiscale_pool(t, pool_w, pool_scale),
                          u_pool, n_ctx_rows - out_from).reshape(bsz * sq, POOL_WIDTH)
    gt = jax.nn.sigmoid(gates).reshape(bsz * sq, N_BRANCH, D_MODEL)
    mrg = (gt[:, 0] * pmm(y_gla, w_br_gla) + gt[:, 1] * pmm(y_gdn, w_br_gdn)
           + gt[:, 2] * pmm(y_pool, w_br_pool))
    return pmm(mrg.astype(BF16), w_out)


def kernel(x, c, ctx, c_ctx, ada_w, ada_b, norm1_w, norm2_w, w_in, gla_lr_w2, gla_lr_b,
           gla_norm_w, gdn_conv_w, gdn_a_log, gdn_dt_bias, gdn_norm_w, pool_w, pool_scale,
           w_br_gla, w_br_gdn, w_br_pool, w_out, ffn_up, ffn_conv, ffn_down, final_norm_w):
    B, T, _ = x.shape
    assert ctx.shape[1] == ROW_TILE and T % ROW_TILE == 0 and (T // GRID_W) % (ROW_TILE // GRID_W) == 0
    s_all = ROW_TILE + T
    all_rows = Rows(B, True, T // ROW_TILE)
    lat_rows = Rows(B, False, T // ROW_TILE)
    xa = jnp.concatenate([ctx, x], axis=1).reshape(B * s_all, D_MODEL)
    cond = jnp.concatenate([c, c_ctx[None, :], jnp.zeros((16 - B - 1, D_MODEL), F32)], axis=0)
    cond = jax.nn.silu(cond)
    mods = [(pmm(cond, ada_w[l].astype(BF16)) + ada_b[l]).reshape(16, 6, D_MODEL) for l in range(DEPTH)]
    _, h = resid_norm(xa, all_rows, all_rows, norm1_w[0], mod=mods[0], shift_idx=0, scale_idx=1)
    x_rows = all_rows
    for l in range(DEPTH):
        last = l == DEPTH - 1
        rows = lat_rows if last else all_rows
        w_main, w_small = permute_w_in(w_in[l])
        y = hybrid_mixer(h, x_rows, rows, B, w_main, w_small, gla_lr_w2[l], gla_lr_b[l], gla_norm_w[l],
                         gdn_conv_w[l], gdn_a_log[l], gdn_dt_bias[l], gdn_norm_w[l],
                         pool_w[l], pool_scale[l], w_br_gla[l].astype(BF16),
                         w_br_gdn[l].astype(BF16), w_br_pool[l].astype(BF16), w_out[l].astype(BF16))
        xa, h2 = resid_norm(xa, x_rows, rows, norm2_w[l], resid=y, gate_mod=mods[l], gate_idx=2,
                            mod=mods[l], shift_idx=3, scale_idx=4)
        x_rows = rows
        up = pmm(h2, ffn_up[l].astype(BF16), BF16)
        gated = ffn_gate(up, rows, ffn_conv[l])
        dn = pmm(gated, ffn_down[l].astype(BF16))
        if last:
            _, out = resid_norm(xa, x_rows, rows, final_norm_w, resid=dn, gate_mod=mods[l], gate_idx=5,
                                out_dtype=F32)
            return out.reshape(B, T, D_MODEL)
        xa, h = resid_norm(xa, x_rows, rows, norm1_w[l + 1], resid=dn, gate_mod=mods[l], gate_idx=5,
                           mod=mods[l + 1], shift_idx=0, scale_idx=1)
```

```python
import functools

import jax
import jax.numpy as jnp
from jax import lax
from jax.experimental import pallas as pl
from jax.experimental.pallas import tpu as pltpu

D_MODEL = 2048
DEPTH = 4
GRID_W = 64
CHUNK = 64
N_DIR = 2
N_BRANCH = 3
EPS = 1e-6

GLA_HEADS = 4
GLA_DK = 128
GLA_DV = 256
GLA_RANK = 16
GLA_TAU = 16.0

GDN_HEADS = 8
GDN_DK = 128
GDN_DV = 128

POOL_WINDOWS = (2, 4, 8, 16)
POOL_GROUPS = 4
POOL_GROUP = 256

D_FF = 5632

GLA_QK = GLA_HEADS * GLA_DK
GLA_V = GLA_HEADS * GLA_DV
GDN_QK = GDN_HEADS * GDN_DK
GDN_V = GDN_HEADS * GDN_DV
GDN_QKV = 2 * GDN_QK + GDN_V
POOL_WIDTH = POOL_GROUPS * POOL_GROUP
IN_SPLITS = (GLA_QK, GLA_QK, GLA_V, GLA_V, N_DIR * GLA_RANK,
             GDN_QKV, N_DIR * GDN_HEADS, N_DIR * GDN_HEADS, GDN_V,
             POOL_WIDTH, N_BRANCH * D_MODEL)
F32 = jnp.float32
BF16 = jnp.bfloat16
HIGHEST = lax.Precision.HIGHEST

V7X_VMEM_BYTES = 64 * 1024 * 1024
VMEM_LIMIT_BYTES = V7X_VMEM_BYTES * 3 // 4
MM_TILE_BUDGET_BYTES = V7X_VMEM_BYTES * 5 // 8
LANES = 128
BF16_SUBLANES = 16

SMALL_WIDTH = LANES
SMALL_COLS = N_DIR * GLA_RANK + 2 * N_DIR * GDN_HEADS
A_COL = N_DIR * GLA_RANK
BT_COL = A_COL + N_DIR * GDN_HEADS
MAIN_SPLITS = (GLA_QK, GLA_QK, GLA_V, GLA_V, GDN_QKV, GDN_V, POOL_WIDTH, N_BRANCH * D_MODEL)
MAIN_WIDTH = sum(MAIN_SPLITS)
GDN_QKV_COL = 2 * GLA_QK + 2 * GLA_V

ROW_TILE = 256


def _split_cols(t, sizes):
    parts, start = [], 0
    for size in sizes:
        parts.append(t[..., start:start + size])
        start += size
    return parts


def _params(*sem):
    return pltpu.CompilerParams(dimension_semantics=sem, vmem_limit_bytes=VMEM_LIMIT_BYTES)


def _sigmoid(x):
    return 1.0 / (1.0 + jnp.exp(-x))


def _softplus(x):
    return jnp.maximum(x, 0.0) + jnp.log(1.0 + jnp.exp(-jnp.abs(x)))


def _mm_kernel(x_ref, w_ref, o_ref):
    o_ref[...] = jnp.dot(x_ref[...].astype(BF16), w_ref[...],
                         preferred_element_type=F32).astype(o_ref.dtype)


def _mm_tiles(m, k, n, x_bytes, o_bytes):
    best = None
    for tm in (1024, 512, 256, 128, 64, 32, 16, 8):
        if m % tm:
            continue
        for tn in (2048, 1024, 512, 256, 128):
            if n % tn:
                continue
            need = 2 * (tm * k * x_bytes + k * tn * 2 + tm * tn * o_bytes)
            if need > MM_TILE_BUDGET_BYTES:
                continue
            score = tm * tn / (tm + tn)
            if best is None or score > best[0]:
                best = (score, tm, tn)
    if best is None:
        raise ValueError(f"no matmul tiling for {(m, k, n)}")
    return best[1], best[2]


def pmm(x, w, out_dtype=F32):
    m, k = x.shape
    n = w.shape[1]
    tm, tn = _mm_tiles(m, k, n, x.dtype.itemsize, jnp.dtype(out_dtype).itemsize)
    return pl.pallas_call(
        _mm_kernel,
        grid=(m // tm, n // tn),
        in_specs=[pl.BlockSpec((tm, k), lambda i, j: (i, 0)),
                  pl.BlockSpec((k, tn), lambda i, j: (0, j))],
        out_specs=pl.BlockSpec((tm, tn), lambda i, j: (i, j)),
        out_shape=jax.ShapeDtypeStruct((m, n), out_dtype),
        compiler_params=_params("parallel", "arbitrary"),
        name="mm",
    )(x, w)


class Rows:
    def __init__(self, batch, has_ctx, lat_tiles):
        self.batch, self.has_ctx, self.lat_tiles = batch, has_ctx, lat_tiles
        self.per_batch = lat_tiles + (1 if has_ctx else 0)
        self.n_tiles = batch * self.per_batch

    def block_of(self, i, other):
        if other.has_ctx == self.has_ctx:
            return i
        assert other.has_ctx and not self.has_ctx
        return (i // self.per_batch) * other.per_batch + 1 + i % self.per_batch

    def mod_row(self, i):
        b = i // self.per_batch
        if not self.has_ctx:
            return b
        return jnp.where(i % self.per_batch == 0, self.batch, b)

    def flags(self, i):
        r = i % self.per_batch
        if not self.has_ctx:
            return False, r == 0, r == self.per_batch - 1
        is_ctx = r == 0
        return is_ctx, is_ctx | (r == 1), is_ctx | (r == self.per_batch - 1)


def _norm_kernel(*refs, has_resid, gate_idx, has_mod, shift_idx, scale_idx, write_x):
    refs = list(refs)
    x_ref = refs.pop(0)
    x = x_ref[...]
    if has_resid:
        y_ref, gmod_ref = refs.pop(0), refs.pop(0)
        x = x + gmod_ref[0, gate_idx:gate_idx + 1, :] * y_ref[...].astype(F32)
    nw_ref = refs.pop(0)
    mod_ref = refs.pop(0) if has_mod else None
    if write_x:
        refs.pop(0)[...] = x
    h_ref = refs.pop(0)
    y = x * lax.rsqrt(jnp.mean(x * x, axis=-1, keepdims=True) + EPS) * nw_ref[...]
    if has_mod:
        y = y * (1.0 + mod_ref[0, scale_idx:scale_idx + 1, :]) + mod_ref[0, shift_idx:shift_idx + 1, :]
    h_ref[...] = y.astype(h_ref.dtype)


def resid_norm(x, x_rows, rows, norm_w, *, resid=None, gate_mod=None, gate_idx=0,
               mod=None, shift_idx=0, scale_idx=1, out_dtype=BF16):
    d = x.shape[-1]
    row = lambda i: (i, 0)
    modspec = pl.BlockSpec((1, 6, d), lambda i: (rows.mod_row(i), 0, 0))
    args = [x]
    in_specs = [pl.BlockSpec((ROW_TILE, d), lambda i: (rows.block_of(i, x_rows), 0))]
    if resid is not None:
        args += [resid, gate_mod]
        in_specs += [pl.BlockSpec((ROW_TILE, d), row), modspec]
    args.append(norm_w.reshape(1, d))
    in_specs.append(pl.BlockSpec((1, d), lambda i: (0, 0)))
    if mod is not None:
        args.append(mod)
        in_specs.append(modspec)
    m = rows.n_tiles * ROW_TILE
    out_shape, out_specs = [], []
    if resid is not None:
        out_shape.append(jax.ShapeDtypeStruct((m, d), F32))
        out_specs.append(pl.BlockSpec((ROW_TILE, d), row))
    out_shape.append(jax.ShapeDtypeStruct((m, d), out_dtype))
    out_specs.append(pl.BlockSpec((ROW_TILE, d), row))
    outs = pl.pallas_call(
        functools.partial(_norm_kernel, has_resid=resid is not None, gate_idx=gate_idx,
                          has_mod=mod is not None, shift_idx=shift_idx, scale_idx=scale_idx,
                          write_x=resid is not None),
        grid=(rows.n_tiles,), in_specs=in_specs, out_specs=out_specs, out_shape=out_shape,
        compiler_params=_params("parallel"), name="resid_norm",
    )(*args)
    return (outs[0], outs[1]) if resid is not None else (None, outs[0])


def _gdn_prep_kernel(x_ref, up_ref, dn_ref, w_ref, o_ref, ext_ref, *, rows):
    halo = BF16_SUBLANES
    _, first, last = rows.flags(pl.program_id(0))
    ext_ref[0:halo] = jnp.where(first, 0.0, up_ref[...].astype(F32))
    ext_ref[halo:halo + ROW_TILE] = x_ref[...].astype(F32)
    ext_ref[halo + ROW_TILE:2 * halo + ROW_TILE] = jnp.where(last, 0.0, dn_ref[...].astype(F32))
    y = (ext_ref[pl.ds(halo - 1, ROW_TILE), :] * w_ref[0:1, :]
         + ext_ref[pl.ds(halo, ROW_TILE), :] * w_ref[1:2, :]
         + ext_ref[pl.ds(halo + 1, ROW_TILE), :] * w_ref[2:3, :])
    y = y * _sigmoid(y)
    for hd in range(2 * GDN_HEADS):
        sl = slice(hd * GDN_DK, (hd + 1) * GDN_DK)
        seg = y[:, sl]
        inv = lax.rsqrt(jnp.sum(seg * seg, axis=-1, keepdims=True) + EPS)
        if hd < GDN_HEADS:
            inv = inv * GDN_DK ** -0.5
        o_ref[:, sl] = (seg * inv).astype(o_ref.dtype)
    o_ref[:, 2 * GDN_QK:] = y[:, 2 * GDN_QK:].astype(o_ref.dtype)


def gdn_prep(p, rows, conv_w):
    m = p.shape[0]
    cb = GDN_QKV_COL // GDN_QKV
    hb = ROW_TILE // BF16_SUBLANES
    n_hblocks = m // BF16_SUBLANES
    return pl.pallas_call(
        functools.partial(_gdn_prep_kernel, rows=rows),
        grid=(rows.n_tiles,),
        in_specs=[pl.BlockSpec((ROW_TILE, GDN_QKV), lambda i: (i, cb)),
                  pl.BlockSpec((BF16_SUBLANES, GDN_QKV), lambda i: (jnp.maximum(i * hb - 1, 0), cb)),
                  pl.BlockSpec((BF16_SUBLANES, GDN_QKV),
                               lambda i: (jnp.minimum((i + 1) * hb, n_hblocks - 1), cb)),
                  pl.BlockSpec((3, GDN_QKV), lambda i: (0, 0))],
        out_specs=pl.BlockSpec((ROW_TILE, GDN_QKV), lambda i: (i, 0)),
        out_shape=jax.ShapeDtypeStruct((m, GDN_QKV), BF16),
        scratch_shapes=[pltpu.VMEM((ROW_TILE + 2 * BF16_SUBLANES, GDN_QKV), F32)],
        compiler_params=_params("parallel"), name="gdn_prep",
    )(p, p, p, conv_w)


def _bdot(a, b):
    return jnp.dot(a.astype(BF16), b.astype(BF16), preferred_element_type=F32)


def _bdot_nt(a, b):
    return lax.dot_general(a.astype(BF16), b.astype(BF16), (((1,), (1,)), ((), ())),
                           preferred_element_type=F32)


def _bdot_tn(a, b):
    return lax.dot_general(a.astype(BF16), b.astype(BF16), (((0,), (0,)), ((), ())),
                           preferred_element_type=F32)


def _rev_chunk(t, n_ctx, n_all):
    return jnp.where(t < n_ctx, n_ctx - 1 - t, n_all + n_ctx - 1 - t)


def _chunk_masks(d):
    ri = lax.broadcasted_iota(jnp.int32, (CHUNK, CHUNK), 0)
    ci = lax.broadcasted_iota(jnp.int32, (CHUNK, CHUNK), 1)
    lower, upper = ci <= ri, ci >= ri
    if d == 0:
        return lower, ci < ri, lower.astype(F32), upper.astype(F32)
    return upper, ci > ri, upper.astype(F32), lower.astype(F32)


def _gla_kernel(qf, kf, vf, sf, qr, kr, vr, sr, w2_ref, b_ref, of, orr, st_ref):
    @pl.when(pl.program_id(1) == 0)
    def _():
        st_ref[...] = jnp.zeros_like(st_ref)

    dirs = ((qf, kf, vf, sf, of), (qr, kr, vr, sr, orr))
    zs = [_bdot(s_ref[0][:, d * GLA_RANK:(d + 1) * GLA_RANK], w2_ref[d]) + b_ref[d]
          for d, (_, _, _, s_ref, _) in enumerate(dirs)]
    bcums = [jnp.dot(_chunk_masks(d)[2], -_softplus(-z) / GLA_TAU, precision=HIGHEST,
                     preferred_element_type=F32) for d, z in enumerate(zs)]
    chains = []
    for d, (q_ref, k_ref, v_ref, _, o_ref) in enumerate(dirs):
        last = CHUNK - 1 if d == 0 else 0
        q_all, k_all, v_all = q_ref[0].astype(F32), k_ref[0].astype(F32), v_ref[0]
        for h in range(GLA_HEADS):
            ks = slice(h * GLA_DK, (h + 1) * GLA_DK)
            vs = slice(h * GLA_DV, (h + 1) * GLA_DV)
            b = bcums[d][:, ks]
            bl = bcums[d][last:last + 1, ks]
            k = k_all[:, ks]
            ch = dict(d=d, h=h, vs=vs, o_ref=o_ref, bl=bl, v=v_all[:, vs], incl=_chunk_masks(d)[0])
            ch["q_dec"] = q_all[:, ks] * (GLA_DK ** -0.5) * jnp.exp(b)
            ch["k_neg"] = k * jnp.exp(-b)
            ch["k_dec"] = k * jnp.exp(bl - b)
            chains.append(ch)
    for ch in chains:
        ch["scores"] = jnp.where(ch["incl"], _bdot_nt(ch["q_dec"], ch["k_neg"]), 0.0)
        ch["st"] = st_ref[ch["d"], ch["h"]]
        ch["inter"] = _bdot_nt(ch["q_dec"], ch["st"])
    for ch in chains:
        ch["o_ref"][0, :, ch["vs"]] = _bdot(ch["scores"], ch["v"]) + ch["inter"]
        st_ref[ch["d"], ch["h"]] = jnp.exp(ch["bl"]) * ch["st"] + _bdot_tn(ch["v"], ch["k_dec"])


def gla_scan(p, small, w2, bias, n_ctx):
    bsz, s, _ = p.shape
    n_all = s // CHUNK

    def specs(rev):
        def row(t):
            return _rev_chunk(t, n_ctx, n_all) if rev else t
        return [pl.BlockSpec((1, CHUNK, GLA_QK), lambda b, t: (b, row(t), 0)),
                pl.BlockSpec((1, CHUNK, GLA_QK), lambda b, t: (b, row(t), 1)),
                pl.BlockSpec((1, CHUNK, GLA_V), lambda b, t: (b, row(t), 2 * GLA_QK // GLA_V)),
                pl.BlockSpec((1, CHUNK, SMALL_WIDTH), lambda b, t: (b, row(t), 0))]

    out_f = pl.BlockSpec((1, CHUNK, GLA_V), lambda b, t: (b, t, 0))
    out_r = pl.BlockSpec((1, CHUNK, GLA_V), lambda b, t: (b, _rev_chunk(t, n_ctx, n_all), 0))
    return pl.pallas_call(
        _gla_kernel,
        grid=(bsz, n_all),
        in_specs=specs(False) + specs(True) + [
            pl.BlockSpec((N_DIR, GLA_RANK, GLA_QK), lambda b, t: (0, 0, 0)),
            pl.BlockSpec((N_DIR, 1, GLA_QK), lambda b, t: (0, 0, 0))],
        out_specs=[out_f, out_r],
        out_shape=[jax.ShapeDtypeStruct((bsz, s, GLA_V), F32)] * 2,
        scratch_shapes=[pltpu.VMEM((N_DIR, GLA_HEADS, GLA_DV, GLA_DK), F32)],
        compiler_params=_params("parallel", "arbitrary"),
        name="gla_scan",
    )(p, p, p, small, p, p, p, small, w2, bias.reshape(N_DIR, 1, GLA_QK))


def _gdn_kernel(qf, kf, vf, sf, qr, kr, vr, sr, alog_ref, dtb_ref, of, orr, s_ref):
    @pl.when(pl.program_id(1) == 0)
    def _():
        s_ref[...] = jnp.zeros_like(s_ref)

    ri = lax.broadcasted_iota(jnp.int32, (CHUNK, CHUNK), 0)
    ci = lax.broadcasted_iota(jnp.int32, (CHUNK, CHUNK), 1)
    eye = (ri == ci).astype(F32)
    same16 = (ri >> 4) == (ci >> 4)
    same32 = (ri >> 5) == (ci >> 5)
    nh = GDN_HEADS
    chains = []
    for d, (q_ref, k_ref, v_ref, sm_ref, o_ref) in enumerate(
            ((qf, kf, vf, sf, of), (qr, kr, vr, sr, orr))):
        incl, strict, tri, tri_t = _chunk_masks(d)
        sm = sm_ref[0]
        g_all = -jnp.exp(alog_ref[...]) * _softplus(sm + dtb_ref[...])
        beta_all = _sigmoid(sm)
        gcol = g_all[:, A_COL:A_COL + N_DIR * nh]
        grow = g_all.T[A_COL:A_COL + N_DIR * nh, :]
        bcol = jnp.dot(tri, gcol, precision=HIGHEST, preferred_element_type=F32)
        brow = jnp.dot(grow, tri_t, precision=HIGHEST, preferred_element_type=F32)
        last = CHUNK - 1 if d == 0 else 0
        btot = bcol[last:last + 1, :]
        q_all, k_all, v_all = q_ref[0].astype(F32), k_ref[0].astype(F32), v_ref[0].astype(F32)
        for h in range(nh):
            c = d * nh + h
            hs = slice(h * GDN_DK, (h + 1) * GDN_DK)
            ch = dict(d=d, h=h, hs=hs, o_ref=o_ref, incl=incl, strict=strict)
            ch["bc"] = bcol[:, c:c + 1]
            ch["br"] = brow[c:c + 1, :]
            ch["bl"] = btot[:, c:c + 1]
            ch["beta"] = beta_all[:, BT_COL + c:BT_COL + c + 1]
            ch["q"], ch["k"], ch["v"] = q_all[:, hs], k_all[:, hs], v_all[:, hs]
            chains.append(ch)

    for ch in chains:
        incl = ch["incl"]
        ch["decay"] = jnp.where(incl, jnp.exp(jnp.where(incl, ch["bc"] - ch["br"], 0.0)), 0.0)
        ch["kb"] = ch["k"] * ch["beta"]
        ch["a2"] = _bdot_nt(jnp.concatenate([ch["kb"], ch["q"]], axis=0), ch["k"])
    for ch in chains:
        a2 = ch.pop("a2")
        n_mat = -jnp.where(ch["strict"], a2[:CHUNK] * ch["decay"], 0.0)
        ch["attn"] = a2[CHUNK:] * ch["decay"]
        n_diag = jnp.where(same16, n_mat, 0.0)
        ch["n32"] = jnp.where(same32, n_mat, 0.0) - n_diag
        ch["n64"] = jnp.where(same32, 0.0, n_mat)
        ch["p"] = eye + n_diag
        ch["n"] = _bdot(n_diag, n_diag)
    for it in range(3):
        for ch in chains:
            ch["p"] = ch["p"] + _bdot(ch["p"], ch["n"])
            if it < 2:
                ch["n"] = _bdot(ch["n"], ch["n"])
    for off in ("n32", "n64"):
        for ch in chains:
            ch["x"] = _bdot(ch["p"], ch[off])
        for ch in chains:
            ch["p"] = ch["p"] + _bdot(ch["x"], ch["p"])
    for ch in chains:
        ch["e_b"] = jnp.exp(ch["bc"])
        uw = _bdot(ch["p"], jnp.concatenate([ch["v"] * ch["beta"], ch["kb"] * ch["e_b"]], axis=1))
        ch["u"], ch["w"] = uw[:, :GDN_DV], uw[:, GDN_DV:]
    for ch in chains:
        ch["s"] = s_ref[ch["d"], ch["h"]]
        ch["wq"] = _bdot(jnp.concatenate([ch["w"], ch["q"] * ch["e_b"]], axis=0), ch["s"])
    for ch in chains:
        wq = ch["wq"]
        v_new = ch["u"] - wq[:CHUNK]
        ch["o_ref"][0, :, ch["hs"]] = wq[CHUNK:] + _bdot(ch["attn"], v_new)
        k_dec = ch["k"] * jnp.exp(ch["bl"] - ch["bc"])
        s_ref[ch["d"], ch["h"]] = jnp.exp(ch["bl"]) * ch["s"] + _bdot_tn(k_dec, v_new)


def gdn_scan(qkv, small, a_log, dt_bias, n_ctx):
    bsz, s, _ = qkv.shape
    n_all = s // CHUNK
    lane_row = lambda t: jnp.zeros((1, SMALL_WIDTH), F32).at[0, A_COL:A_COL + N_DIR * GDN_HEADS].set(
        t.astype(F32).reshape(-1))

    def specs(rev):
        def row(t):
            return _rev_chunk(t, n_ctx, n_all) if rev else t
        return [pl.BlockSpec((1, CHUNK, GDN_QK), lambda b, t: (b, row(t), 0)),
                pl.BlockSpec((1, CHUNK, GDN_QK), lambda b, t: (b, row(t), 1)),
                pl.BlockSpec((1, CHUNK, GDN_V), lambda b, t: (b, row(t), 2)),
                pl.BlockSpec((1, CHUNK, SMALL_WIDTH), lambda b, t: (b, row(t), 0))]

    const = pl.BlockSpec((1, SMALL_WIDTH), lambda b, t: (0, 0))
    out_f = pl.BlockSpec((1, CHUNK, GDN_V), lambda b, t: (b, t, 0))
    out_r = pl.BlockSpec((1, CHUNK, GDN_V), lambda b, t: (b, _rev_chunk(t, n_ctx, n_all), 0))
    return pl.pallas_call(
        _gdn_kernel,
        grid=(bsz, n_all),
        in_specs=specs(False) + specs(True) + [const, const],
        out_specs=[out_f, out_r],
        out_shape=[jax.ShapeDtypeStruct((bsz, s, GDN_V), F32)] * 2,
        scratch_shapes=[pltpu.VMEM((N_DIR, GDN_HEADS, GDN_DK, GDN_DV), F32)],
        compiler_params=_params("parallel", "arbitrary"),
        name="gdn_scan",
    )(qkv, qkv, qkv, small, qkv, qkv, qkv, small, lane_row(a_log), lane_row(dt_bias))


FFN_COL_TILE = 512


FFN_EXT_ROWS = ROW_TILE + 2 * GRID_W
FFN_SUB_COLS = 256


def _ffn_gate_kernel(a_ref, up_ref, dn_ref, v_ref, w_ref, o_ref, shift_ref, *, rows):
    g, n = GRID_W, FFN_EXT_ROWS

    @pl.when((pl.program_id(0) == 0) & (pl.program_id(1) == 0))
    def _():
        ri = lax.broadcasted_iota(jnp.int32, (n, n), 0)
        ci = lax.broadcasted_iota(jnp.int32, (n, n), 1)
        for kind in range(2):
            col = (ri & (g - 1)) if kind == 0 else ri - g
            last_col = g - 1 if kind == 0 else ROW_TILE - 1
            shift_ref[kind, 0] = jnp.where((ci == ri - 1) & (col != 0), 1.0, 0.0).astype(BF16)
            shift_ref[kind, 1] = jnp.where((ci == ri + 1) & (col != last_col), 1.0, 0.0).astype(BF16)

    is_ctx, first, last = rows.flags(pl.program_id(0))
    kind = jnp.where(is_ctx, 1, 0)
    halo_zero = jnp.zeros(up_ref.shape, up_ref.dtype)
    up = jnp.where(first, halo_zero, up_ref[...])
    dn = jnp.where(last, halo_zero, dn_ref[...])
    ext_all = jnp.concatenate([up, a_ref[...], dn], axis=0)
    vert = jnp.where(is_ctx, 0.0, 1.0)
    for c0 in range(0, FFN_COL_TILE, FFN_SUB_COLS):
        cs = slice(c0, c0 + FFN_SUB_COLS)
        ext = ext_all[:, cs]
        shifted = (jnp.dot(shift_ref[kind, 0], ext, preferred_element_type=F32),
                   ext.astype(F32),
                   jnp.dot(shift_ref[kind, 1], ext, preferred_element_type=F32))
        acc = None
        for dr in (-1, 0, 1):
            for dc in (-1, 0, 1):
                val = shifted[dc + 1][g + dr * g:g + dr * g + ROW_TILE]
                wt = w_ref[3 * (dr + 1) + dc + 1:3 * (dr + 1) + dc + 2, cs]
                if dr != 0:
                    wt = wt * vert
                acc = val * wt if acc is None else acc + val * wt
        o_ref[:, cs] = (acc * _sigmoid(acc) * v_ref[:, cs].astype(F32)).astype(o_ref.dtype)


def ffn_gate(up, rows, w_conv):
    m = up.shape[0]
    nc = D_FF // FFN_COL_TILE
    gb = ROW_TILE // GRID_W
    n_gblocks = m // GRID_W
    return pl.pallas_call(
        functools.partial(_ffn_gate_kernel, rows=rows),
        grid=(rows.n_tiles, nc),
        in_specs=[pl.BlockSpec((ROW_TILE, FFN_COL_TILE), lambda i, j: (i, j)),
                  pl.BlockSpec((GRID_W, FFN_COL_TILE), lambda i, j: (jnp.maximum(i * gb - 1, 0), j)),
                  pl.BlockSpec((GRID_W, FFN_COL_TILE),
                               lambda i, j: (jnp.minimum((i + 1) * gb, n_gblocks - 1), j)),
                  pl.BlockSpec((ROW_TILE, FFN_COL_TILE), lambda i, j: (i, nc + j)),
                  pl.BlockSpec((9, FFN_COL_TILE), lambda i, j: (0, j))],
        out_specs=pl.BlockSpec((ROW_TILE, FFN_COL_TILE), lambda i, j: (i, j)),
        out_shape=jax.ShapeDtypeStruct((m, D_FF), BF16),
        scratch_shapes=[pltpu.VMEM((2, 2, FFN_EXT_ROWS, FFN_EXT_ROWS), BF16)],
        compiler_params=_params("arbitrary", "arbitrary"), name="ffn_gate",
    )(up, up, up, up, w_conv.reshape(9, D_FF))


BRANCH_WIDTH = GLA_V + GDN_V + POOL_WIDTH
Z_GLA_COL = 2 * GLA_QK + GLA_V
Z_GDN_COL = GDN_QKV_COL + GDN_QKV
POOL_COL = Z_GDN_COL + GDN_V
GATES_COL = POOL_COL + POOL_WIDTH
POOL_HALO = BF16_SUBLANES
assert POOL_HALO >= max(POOL_WINDOWS) // 2


def _headnorm_gate(o, z, nw, n_heads, width, o_ref, col0):
    for h in range(n_heads):
        sl = slice(h * width, (h + 1) * width)
        oh, zh = o[:, sl], z[:, sl]
        inv = lax.rsqrt(jnp.mean(oh * oh, axis=-1, keepdims=True) + EPS)
        o_ref[:, col0 + h * width:col0 + (h + 1) * width] = (
            oh * inv * nw * (zh * _sigmoid(zh))).astype(o_ref.dtype)


def _merge_prep_kernel(gf_ref, gr_ref, df_ref, dr_ref, za_ref, zb_ref, u_ref, uup_ref, udn_ref,
                       gnw_ref, dnw_ref, pw_ref, ps_ref, o_ref, ext_ref, *, rows):
    _, first, last = rows.flags(pl.program_id(0))
    _headnorm_gate(gf_ref[...] + gr_ref[...], za_ref[...].astype(F32), gnw_ref[...],
                   GLA_HEADS, GLA_DV, o_ref, 0)
    _headnorm_gate(df_ref[...] + dr_ref[...], zb_ref[...].astype(F32), dnw_ref[...],
                   GDN_HEADS, GDN_DV, o_ref, GLA_V)
    halo = POOL_HALO
    ext_ref[0:halo] = jnp.where(first, 0.0, uup_ref[...].astype(F32))
    ext_ref[halo:halo + ROW_TILE] = u_ref[...].astype(F32)
    ext_ref[halo + ROW_TILE:2 * halo + ROW_TILE] = jnp.where(last, 0.0, udn_ref[...].astype(F32))
    t = lax.broadcasted_iota(jnp.int32, (ROW_TILE, 1), 0)
    for gi, win in enumerate(POOL_WINDOWS):
        sl = slice(gi * POOL_GROUP, (gi + 1) * POOL_GROUP)
        half = win // 2
        acc = None
        for off in range(-half, win - half):
            val = ext_ref[pl.ds(halo + off, ROW_TILE), sl]
            acc = val if acc is None else acc + val
        lo_clip = jnp.where(first, jnp.maximum(half - t, 0), 0)
        hi_clip = jnp.where(last, jnp.maximum(t - half + win - ROW_TILE, 0), 0)
        cnt = (win - lo_clip - hi_clip).astype(F32)
        pg = acc / cnt - ext_ref[pl.ds(halo, ROW_TILE), sl]
        yp = _bdot(pg, pw_ref[gi]) * ps_ref[:, sl]
        o_ref[:, GLA_V + GDN_V + gi * POOL_GROUP:GLA_V + GDN_V + (gi + 1) * POOL_GROUP] = yp.astype(o_ref.dtype)


def merge_prep(p, o_gla, o_gdn, p_rows, rows, gla_nw, gdn_nw, pool_w, pool_scale):
    hb = ROW_TILE // BF16_SUBLANES
    n_hblocks = p.shape[0] // BF16_SUBLANES
    blk = lambda i: rows.block_of(i, p_rows)
    wide = lambda col: pl.BlockSpec((ROW_TILE, 1024), lambda i: (blk(i), col // 1024))
    const = lambda shape: pl.BlockSpec(shape, lambda i: (0,) * len(shape))
    pc = POOL_COL // POOL_WIDTH
    return pl.pallas_call(
        functools.partial(_merge_prep_kernel, rows=rows),
        grid=(rows.n_tiles,),
        in_specs=[wide(0), wide(0), wide(0), wide(0), wide(Z_GLA_COL), wide(Z_GDN_COL), wide(POOL_COL),
                  pl.BlockSpec((POOL_HALO, POOL_WIDTH), lambda i: (jnp.maximum(blk(i) * hb - 1, 0), pc)),
                  pl.BlockSpec((POOL_HALO, POOL_WIDTH),
                               lambda i: (jnp.minimum((blk(i) + 1) * hb, n_hblocks - 1), pc)),
                  const((1, GLA_DV)), const((1, GDN_DV)),
                  const((POOL_GROUPS, POOL_GROUP, POOL_GROUP)), const((1, POOL_WIDTH))],
        out_specs=pl.BlockSpec((ROW_TILE, BRANCH_WIDTH), lambda i: (i, 0)),
        out_shape=jax.ShapeDtypeStruct((rows.n_tiles * ROW_TILE, BRANCH_WIDTH), BF16),
        scratch_shapes=[pltpu.VMEM((ROW_TILE + 2 * POOL_HALO, POOL_WIDTH), F32)],
        compiler_params=_params("parallel"), name="merge_prep",
    )(o_gla[0], o_gla[1], o_gdn[0], o_gdn[1], p, p, p, p, p,
      gla_nw.reshape(1, GLA_DV), gdn_nw.reshape(1, GDN_DV), pool_w.astype(BF16),
      pool_scale.reshape(1, POOL_WIDTH))


def _branch_kernel(ya, yb, yc, ga, gb, gc, wa, wb, wc, o_ref):
    acc = None
    for y_ref, g_ref, w_ref in ((ya, ga, wa), (yb, gb, wb), (yc, gc, wc)):
        term = _sigmoid(g_ref[...].astype(F32)) * jnp.dot(y_ref[...], w_ref[...], preferred_element_type=F32)
        acc = term if acc is None else acc + term
    o_ref[...] = acc.astype(o_ref.dtype)


def branch_merge(ycat, p, p_rows, rows, w_gla, w_gdn, w_pool):
    blk = lambda i: rows.block_of(i, p_rows)
    ysp = lambda k: pl.BlockSpec((ROW_TILE, 1024), lambda i: (i, k))
    gsp = lambda k: pl.BlockSpec((ROW_TILE, D_MODEL), lambda i: (blk(i), GATES_COL // D_MODEL + k))
    wsp = pl.BlockSpec((1024, D_MODEL), lambda i: (0, 0))
    return pl.pallas_call(
        _branch_kernel,
        grid=(rows.n_tiles,),
        in_specs=[ysp(0), ysp(1), ysp(2), gsp(0), gsp(1), gsp(2), wsp, wsp, wsp],
        out_specs=pl.BlockSpec((ROW_TILE, D_MODEL), lambda i: (i, 0)),
        out_shape=jax.ShapeDtypeStruct((rows.n_tiles * ROW_TILE, D_MODEL), BF16),
        compiler_params=_params("parallel"), name="branch_merge",
    )(ycat, ycat, ycat, p, p, p, w_gla, w_gdn, w_pool)


def permute_w_in(w):
    q, k, v, zg, lr, gqkv, ga, gbt, gz, pool, gates = _split_cols(w, IN_SPLITS)
    main = jnp.concatenate([q, k, v, zg, gqkv, gz, pool, gates], axis=1)
    small = jnp.concatenate([lr, ga, gbt, jnp.zeros((w.shape[0], SMALL_WIDTH - SMALL_COLS), w.dtype)], axis=1)
    return main.astype(BF16), small.astype(BF16)


def hybrid_mixer(h, rows, out_rows, bsz, w_main, w_small, gla_w2, gla_b, gla_nw, gdn_cw, gdn_alog,
                 gdn_dtb, gdn_nw, pool_w, pool_scale, w_br_gla, w_br_gdn, w_br_pool, w_out):
    m = h.shape[0]
    s = m // bsz
    p = pmm(h, w_main, BF16)
    small = pmm(h, w_small)
    p3, small3 = p.reshape(bsz, s, MAIN_WIDTH), small.reshape(bsz, s, SMALL_WIDTH)
    n_ctx = (ROW_TILE if rows.has_ctx else 0) // CHUNK
    o_gla = [t.reshape(m, GLA_V) for t in gla_scan(p3, small3, gla_w2.astype(BF16), gla_b, n_ctx)]
    qkv = gdn_prep(p, rows, gdn_cw).reshape(bsz, s, GDN_QKV)
    o_gdn = [t.reshape(m, GDN_V) for t in gdn_scan(qkv, small3, gdn_alog, gdn_dtb, n_ctx)]
    ycat = merge_prep(p, o_gla, o_gdn, rows, out_rows, gla_nw, gdn_nw, pool_w, pool_scale)
    mrg = branch_merge(ycat, p, rows, out_rows, w_br_gla, w_br_gdn, w_br_pool)
    return pmm(mrg, w_out)


def kernel(x, c, ctx, c_ctx, ada_w, ada_b, norm1_w, norm2_w, w_in, gla_lr_w2, gla_lr_b,
           gla_norm_w, gdn_conv_w, gdn_a_log, gdn_dt_bias, gdn_norm_w, pool_w, pool_scale,
           w_br_gla, w_br_gdn, w_br_pool, w_out, ffn_up, ffn_conv, ffn_down, final_norm_w):
    B, T, _ = x.shape
    assert ctx.shape[1] == ROW_TILE and T % ROW_TILE == 0 and (T // GRID_W) % (ROW_TILE // GRID_W) == 0
    s_all = ROW_TILE + T
    all_rows = Rows(B, True, T // ROW_TILE)
    lat_rows = Rows(B, False, T // ROW_TILE)
    xa = jnp.concatenate([ctx, x], axis=1).reshape(B * s_all, D_MODEL)
    cond = jnp.concatenate([c, c_ctx[None, :], jnp.zeros((16 - B - 1, D_MODEL), F32)], axis=0)
    cond = jax.nn.silu(cond)
    mods = [(pmm(cond, ada_w[l].astype(BF16)) + ada_b[l]).reshape(16, 6, D_MODEL) for l in range(DEPTH)]
    _, h = resid_norm(xa, all_rows, all_rows, norm1_w[0], mod=mods[0], shift_idx=0, scale_idx=1)
    x_rows = all_rows
    for l in range(DEPTH):
        last = l == DEPTH - 1
        rows = lat_rows if last else all_rows
        w_main, w_small = permute_w_in(w_in[l])
        y = hybrid_mixer(h, x_rows, rows, B, w_main, w_small, gla_lr_w2[l], gla_lr_b[l], gla_norm_w[l],
                         gdn_conv_w[l], gdn_a_log[l], gdn_dt_bias[l], gdn_norm_w[l],
                         pool_w[l], pool_scale[l], w_br_gla[l].astype(BF16),
                         w_br_gdn[l].astype(BF16), w_br_pool[l].astype(BF16), w_out[l].astype(BF16))
        xa, h2 = resid_norm(xa, x_rows, rows, norm2_w[l], resid=y, gate_mod=mods[l], gate_idx=2,
                            mod=mods[l], shift_idx=3, scale_idx=4)
        x_rows = rows
        up = pmm(h2, ffn_up[l].astype(BF16), BF16)
        gated = ffn_gate(up, rows, ffn_conv[l])
        dn = pmm(gated, ffn_down[l].astype(BF16))
        if last:
            _, out = resid_norm(xa, x_rows, rows, final_norm_w, resid=dn, gate_mod=mods[l], gate_idx=5,
                                out_dtype=F32)
            return out.reshape(B, T, D_MODEL)
        xa, h = resid_norm(xa, x_rows, rows, norm1_w[l + 1], resid=dn, gate_mod=mods[l], gate_idx=5,
                           mod=mods[l + 1], shift_idx=0, scale_idx=1)
```

```python
import functools

import jax
import jax.numpy as jnp
from jax import lax
from jax.experimental import pallas as pl
from jax.experimental.pallas import tpu as pltpu

D_MODEL = 2048
DEPTH = 4
GRID_W = 64
CHUNK = 64
N_DIR = 2
N_BRANCH = 3
EPS = 1e-6

GLA_HEADS = 4
GLA_DK = 128
GLA_DV = 256
GLA_RANK = 16
GLA_TAU = 16.0

GDN_HEADS = 8
GDN_DK = 128
GDN_DV = 128

POOL_WINDOWS = (2, 4, 8, 16)
POOL_GROUPS = 4
POOL_GROUP = 256

D_FF = 5632

GLA_QK = GLA_HEADS * GLA_DK
GLA_V = GLA_HEADS * GLA_DV
GDN_QK = GDN_HEADS * GDN_DK
GDN_V = GDN_HEADS * GDN_DV
GDN_QKV = 2 * GDN_QK + GDN_V
POOL_WIDTH = POOL_GROUPS * POOL_GROUP
IN_SPLITS = (GLA_QK, GLA_QK, GLA_V, GLA_V, N_DIR * GLA_RANK,
             GDN_QKV, N_DIR * GDN_HEADS, N_DIR * GDN_HEADS, GDN_V,
             POOL_WIDTH, N_BRANCH * D_MODEL)
F32 = jnp.float32
BF16 = jnp.bfloat16
HIGHEST = lax.Precision.HIGHEST

V7X_VMEM_BYTES = 64 * 1024 * 1024
VMEM_LIMIT_BYTES = V7X_VMEM_BYTES * 3 // 4
MM_TILE_BUDGET_BYTES = V7X_VMEM_BYTES * 5 // 8
LANES = 128
BF16_SUBLANES = 16

SMALL_WIDTH = LANES
SMALL_COLS = N_DIR * GLA_RANK + 2 * N_DIR * GDN_HEADS
A_COL = N_DIR * GLA_RANK
BT_COL = A_COL + N_DIR * GDN_HEADS
MAIN_SPLITS = (GLA_QK, GLA_QK, GLA_V, GLA_V, GDN_QKV, GDN_V, POOL_WIDTH, N_BRANCH * D_MODEL)
MAIN_WIDTH = sum(MAIN_SPLITS)
GDN_QKV_COL = 2 * GLA_QK + 2 * GLA_V

ROW_TILE = 256


def _split_cols(t, sizes):
    parts, start = [], 0
    for size in sizes:
        parts.append(t[..., start:start + size])
        start += size
    return parts


def _params(*sem):
    return pltpu.CompilerParams(dimension_semantics=sem, vmem_limit_bytes=VMEM_LIMIT_BYTES)


def _sigmoid(x):
    return 1.0 / (1.0 + jnp.exp(-x))


def _softplus(x):
    return jnp.maximum(x, 0.0) + jnp.log(1.0 + jnp.exp(-jnp.abs(x)))


def _mm_kernel(x_ref, w_ref, o_ref):
    o_ref[...] = jnp.dot(x_ref[...].astype(BF16), w_ref[...],
                         preferred_element_type=F32).astype(o_ref.dtype)


def _mm_tiles(m, k, n, x_bytes, o_bytes):
    best = None
    for tm in (1024, 512, 256, 128, 64, 32, 16, 8):
        if m % tm:
            continue
        for tn in (2048, 1024, 512, 256, 128):
            if n % tn:
                continue
            need = 2 * (tm * k * x_bytes + k * tn * 2 + tm * tn * o_bytes)
            if need > MM_TILE_BUDGET_BYTES:
                continue
            score = tm * tn / (tm + tn)
            if best is None or score > best[0]:
                best = (score, tm, tn)
    if best is None:
        raise ValueError(f"no matmul tiling for {(m, k, n)}")
    return best[1], best[2]


def pmm(x, w, out_dtype=F32):
    m, k = x.shape
    n = w.shape[1]
    tm, tn = _mm_tiles(m, k, n, x.dtype.itemsize, jnp.dtype(out_dtype).itemsize)
    return pl.pallas_call(
        _mm_kernel,
        grid=(m // tm, n // tn),
        in_specs=[pl.BlockSpec((tm, k), lambda i, j: (i, 0)),
                  pl.BlockSpec((k, tn), lambda i, j: (0, j))],
        out_specs=pl.BlockSpec((tm, tn), lambda i, j: (i, j)),
        out_shape=jax.ShapeDtypeStruct((m, n), out_dtype),
        compiler_params=_params("parallel", "arbitrary"),
        name="mm",
    )(x, w)


class Rows:
    def __init__(self, batch, has_ctx, lat_tiles):
        self.batch, self.has_ctx, self.lat_tiles = batch, has_ctx, lat_tiles
        self.per_batch = lat_tiles + (1 if has_ctx else 0)
        self.n_tiles = batch * self.per_batch

    def block_of(self, i, other):
        if other.has_ctx == self.has_ctx:
            return i
        assert other.has_ctx and not self.has_ctx
        return (i // self.per_batch) * other.per_batch + 1 + i % self.per_batch

    def mod_row(self, i):
        b = i // self.per_batch
        if not self.has_ctx:
            return b
        return jnp.where(i % self.per_batch == 0, self.batch, b)

    def flags(self, i):
        r = i % self.per_batch
        if not self.has_ctx:
            return False, r == 0, r == self.per_batch - 1
        is_ctx = r == 0
        return is_ctx, is_ctx | (r == 1), is_ctx | (r == self.per_batch - 1)


def _norm_kernel(*refs, has_resid, gate_idx, has_mod, shift_idx, scale_idx, write_x):
    refs = list(refs)
    x_ref = refs.pop(0)
    x = x_ref[...]
    if has_resid:
        y_ref, gmod_ref = refs.pop(0), refs.pop(0)
        x = x + gmod_ref[0, gate_idx:gate_idx + 1, :] * y_ref[...].astype(F32)
    nw_ref = refs.pop(0)
    mod_ref = refs.pop(0) if has_mod else None
    if write_x:
        refs.pop(0)[...] = x
    h_ref = refs.pop(0)
    y = x * lax.rsqrt(jnp.mean(x * x, axis=-1, keepdims=True) + EPS) * nw_ref[...]
    if has_mod:
        y = y * (1.0 + mod_ref[0, scale_idx:scale_idx + 1, :]) + mod_ref[0, shift_idx:shift_idx + 1, :]
    h_ref[...] = y.astype(h_ref.dtype)


def resid_norm(x, x_rows, rows, norm_w, *, resid=None, gate_mod=None, gate_idx=0,
               mod=None, shift_idx=0, scale_idx=1, out_dtype=BF16):
    d = x.shape[-1]
    row = lambda i: (i, 0)
    modspec = pl.BlockSpec((1, 6, d), lambda i: (rows.mod_row(i), 0, 0))
    args = [x]
    in_specs = [pl.BlockSpec((ROW_TILE, d), lambda i: (rows.block_of(i, x_rows), 0))]
    if resid is not None:
        args += [resid, gate_mod]
        in_specs += [pl.BlockSpec((ROW_TILE, d), row), modspec]
    args.append(norm_w.reshape(1, d))
    in_specs.append(pl.BlockSpec((1, d), lambda i: (0, 0)))
    if mod is not None:
        args.append(mod)
        in_specs.append(modspec)
    m = rows.n_tiles * ROW_TILE
    out_shape, out_specs = [], []
    if resid is not None:
        out_shape.append(jax.ShapeDtypeStruct((m, d), F32))
        out_specs.append(pl.BlockSpec((ROW_TILE, d), row))
    out_shape.append(jax.ShapeDtypeStruct((m, d), out_dtype))
    out_specs.append(pl.BlockSpec((ROW_TILE, d), row))
    outs = pl.pallas_call(
        functools.partial(_norm_kernel, has_resid=resid is not None, gate_idx=gate_idx,
                          has_mod=mod is not None, shift_idx=shift_idx, scale_idx=scale_idx,
                          write_x=resid is not None),
        grid=(rows.n_tiles,), in_specs=in_specs, out_specs=out_specs, out_shape=out_shape,
        compiler_params=_params("parallel"), name="resid_norm",
    )(*args)
    return (outs[0], outs[1]) if resid is not None else (None, outs[0])


def _gdn_prep_kernel(x_ref, up_ref, dn_ref, w_ref, o_ref, ext_ref, *, rows):
    halo = BF16_SUBLANES
    _, first, last = rows.flags(pl.program_id(0))
    ext_ref[0:halo] = jnp.where(first, 0.0, up_ref[...].astype(F32))
    ext_ref[halo:halo + ROW_TILE] = x_ref[...].astype(F32)
    ext_ref[halo + ROW_TILE:2 * halo + ROW_TILE] = jnp.where(last, 0.0, dn_ref[...].astype(F32))
    y = (ext_ref[pl.ds(halo - 1, ROW_TILE), :] * w_ref[0:1, :]
         + ext_ref[pl.ds(halo, ROW_TILE), :] * w_ref[1:2, :]
         + ext_ref[pl.ds(halo + 1, ROW_TILE), :] * w_ref[2:3, :])
    y = y * _sigmoid(y)
    for hd in range(2 * GDN_HEADS):
        sl = slice(hd * GDN_DK, (hd + 1) * GDN_DK)
        seg = y[:, sl]
        inv = lax.rsqrt(jnp.sum(seg * seg, axis=-1, keepdims=True) + EPS)
        if hd < GDN_HEADS:
            inv = inv * GDN_DK ** -0.5
        o_ref[:, sl] = (seg * inv).astype(o_ref.dtype)
    o_ref[:, 2 * GDN_QK:] = y[:, 2 * GDN_QK:].astype(o_ref.dtype)


def gdn_prep(p, rows, conv_w):
    m = p.shape[0]
    cb = GDN_QKV_COL // GDN_QKV
    hb = ROW_TILE // BF16_SUBLANES
    n_hblocks = m // BF16_SUBLANES
    return pl.pallas_call(
        functools.partial(_gdn_prep_kernel, rows=rows),
        grid=(rows.n_tiles,),
        in_specs=[pl.BlockSpec((ROW_TILE, GDN_QKV), lambda i: (i, cb)),
                  pl.BlockSpec((BF16_SUBLANES, GDN_QKV), lambda i: (jnp.maximum(i * hb - 1, 0), cb)),
                  pl.BlockSpec((BF16_SUBLANES, GDN_QKV),
                               lambda i: (jnp.minimum((i + 1) * hb, n_hblocks - 1), cb)),
                  pl.BlockSpec((3, GDN_QKV), lambda i: (0, 0))],
        out_specs=pl.BlockSpec((ROW_TILE, GDN_QKV), lambda i: (i, 0)),
        out_shape=jax.ShapeDtypeStruct((m, GDN_QKV), BF16),
        scratch_shapes=[pltpu.VMEM((ROW_TILE + 2 * BF16_SUBLANES, GDN_QKV), F32)],
        compiler_params=_params("parallel"), name="gdn_prep",
    )(p, p, p, conv_w)


def _bdot(a, b):
    return jnp.dot(a.astype(BF16), b.astype(BF16), preferred_element_type=F32)


def _bdot_nt(a, b):
    return lax.dot_general(a.astype(BF16), b.astype(BF16), (((1,), (1,)), ((), ())),
                           preferred_element_type=F32)


def _bdot_tn(a, b):
    return lax.dot_general(a.astype(BF16), b.astype(BF16), (((0,), (0,)), ((), ())),
                           preferred_element_type=F32)


def _rev_chunk(t, n_ctx, n_all):
    return jnp.where(t < n_ctx, n_ctx - 1 - t, n_all + n_ctx - 1 - t)


def _chunk_masks(d):
    ri = lax.broadcasted_iota(jnp.int32, (CHUNK, CHUNK), 0)
    ci = lax.broadcasted_iota(jnp.int32, (CHUNK, CHUNK), 1)
    lower, upper = ci <= ri, ci >= ri
    if d == 0:
        return lower, ci < ri, lower.astype(F32), upper.astype(F32)
    return upper, ci > ri, upper.astype(F32), lower.astype(F32)


def _gla_kernel(qf, kf, vf, sf, qr, kr, vr, sr, w2_ref, b_ref, of, orr, st_ref):
    @pl.when(pl.program_id(1) == 0)
    def _():
        st_ref[...] = jnp.zeros_like(st_ref)

    dirs = ((qf, kf, vf, sf, of), (qr, kr, vr, sr, orr))
    zs = [_bdot(s_ref[0][:, d * GLA_RANK:(d + 1) * GLA_RANK], w2_ref[d]) + b_ref[d]
          for d, (_, _, _, s_ref, _) in enumerate(dirs)]
    bcums = [jnp.dot(_chunk_masks(d)[2], -_softplus(-z) / GLA_TAU, precision=HIGHEST,
                     preferred_element_type=F32) for d, z in enumerate(zs)]
    chains = []
    for d, (q_ref, k_ref, v_ref, _, o_ref) in enumerate(dirs):
        last = CHUNK - 1 if d == 0 else 0
        q_all, k_all, v_all = q_ref[0].astype(F32), k_ref[0].astype(F32), v_ref[0]
        for h in range(GLA_HEADS):
            ks = slice(h * GLA_DK, (h + 1) * GLA_DK)
            vs = slice(h * GLA_DV, (h + 1) * GLA_DV)
            b = bcums[d][:, ks]
            bl = bcums[d][last:last + 1, ks]
            k = k_all[:, ks]
            ch = dict(d=d, h=h, vs=vs, o_ref=o_ref, bl=bl, v=v_all[:, vs], incl=_chunk_masks(d)[0])
            ch["q_dec"] = q_all[:, ks] * (GLA_DK ** -0.5) * jnp.exp(b)
            ch["k_neg"] = k * jnp.exp(-b)
            ch["k_dec"] = k * jnp.exp(bl - b)
            chains.append(ch)
    for ch in chains:
        ch["scores"] = jnp.where(ch["incl"], _bdot_nt(ch["q_dec"], ch["k_neg"]), 0.0)
        ch["st"] = st_ref[ch["d"], ch["h"]]
        ch["inter"] = _bdot_nt(ch["q_dec"], ch["st"])
    for ch in chains:
        ch["o_ref"][0, :, ch["vs"]] = (_bdot(ch["scores"], ch["v"]) + ch["inter"]).astype(ch["o_ref"].dtype)
        st_ref[ch["d"], ch["h"]] = jnp.exp(ch["bl"]) * ch["st"] + _bdot_tn(ch["v"], ch["k_dec"])


def gla_scan(p, small, w2, bias, n_ctx):
    bsz, s, _ = p.shape
    n_all = s // CHUNK

    def specs(rev):
        def row(t):
            return _rev_chunk(t, n_ctx, n_all) if rev else t
        return [pl.BlockSpec((1, CHUNK, GLA_QK), lambda b, t: (b, row(t), 0)),
                pl.BlockSpec((1, CHUNK, GLA_QK), lambda b, t: (b, row(t), 1)),
                pl.BlockSpec((1, CHUNK, GLA_V), lambda b, t: (b, row(t), 2 * GLA_QK // GLA_V)),
                pl.BlockSpec((1, CHUNK, SMALL_WIDTH), lambda b, t: (b, row(t), 0))]

    out_f = pl.BlockSpec((1, CHUNK, GLA_V), lambda b, t: (b, t, 0))
    out_r = pl.BlockSpec((1, CHUNK, GLA_V), lambda b, t: (b, _rev_chunk(t, n_ctx, n_all), 0))
    return pl.pallas_call(
        _gla_kernel,
        grid=(bsz, n_all),
        in_specs=specs(False) + specs(True) + [
            pl.BlockSpec((N_DIR, GLA_RANK, GLA_QK), lambda b, t: (0, 0, 0)),
            pl.BlockSpec((N_DIR, 1, GLA_QK), lambda b, t: (0, 0, 0))],
        out_specs=[out_f, out_r],
        out_shape=[jax.ShapeDtypeStruct((bsz, s, GLA_V), BF16)] * 2,
        scratch_shapes=[pltpu.VMEM((N_DIR, GLA_HEADS, GLA_DV, GLA_DK), F32)],
        compiler_params=_params("parallel", "arbitrary"),
        name="gla_scan",
    )(p, p, p, small, p, p, p, small, w2, bias.reshape(N_DIR, 1, GLA_QK))


def _gdn_kernel(qf, kf, vf, sf, qr, kr, vr, sr, alog_ref, dtb_ref, of, orr, s_ref):
    @pl.when(pl.program_id(1) == 0)
    def _():
        s_ref[...] = jnp.zeros_like(s_ref)

    ri = lax.broadcasted_iota(jnp.int32, (CHUNK, CHUNK), 0)
    ci = lax.broadcasted_iota(jnp.int32, (CHUNK, CHUNK), 1)
    eye = (ri == ci).astype(F32)
    same16 = (ri >> 4) == (ci >> 4)
    same32 = (ri >> 5) == (ci >> 5)
    nh = GDN_HEADS
    chains = []
    for d, (q_ref, k_ref, v_ref, sm_ref, o_ref) in enumerate(
            ((qf, kf, vf, sf, of), (qr, kr, vr, sr, orr))):
        incl, strict, tri, tri_t = _chunk_masks(d)
        sm = sm_ref[0]
        g_all = -jnp.exp(alog_ref[...]) * _softplus(sm + dtb_ref[...])
        beta_all = _sigmoid(sm)
        gcol = g_all[:, A_COL:A_COL + N_DIR * nh]
        grow = g_all.T[A_COL:A_COL + N_DIR * nh, :]
        bcol = jnp.dot(tri, gcol, precision=HIGHEST, preferred_element_type=F32)
        brow = jnp.dot(grow, tri_t, precision=HIGHEST, preferred_element_type=F32)
        last = CHUNK - 1 if d == 0 else 0
        btot = bcol[last:last + 1, :]
        q_all, k_all, v_all = q_ref[0].astype(F32), k_ref[0].astype(F32), v_ref[0].astype(F32)
        for h in range(nh):
            c = d * nh + h
            hs = slice(h * GDN_DK, (h + 1) * GDN_DK)
            ch = dict(d=d, h=h, hs=hs, o_ref=o_ref, incl=incl, strict=strict)
            ch["bc"] = bcol[:, c:c + 1]
            ch["br"] = brow[c:c + 1, :]
            ch["bl"] = btot[:, c:c + 1]
            ch["beta"] = beta_all[:, BT_COL + c:BT_COL + c + 1]
            ch["q"], ch["k"], ch["v"] = q_all[:, hs], k_all[:, hs], v_all[:, hs]
            chains.append(ch)

    for ch in chains:
        incl = ch["incl"]
        ch["decay"] = jnp.where(incl, jnp.exp(jnp.where(incl, ch["bc"] - ch["br"], 0.0)), 0.0)
        ch["kb"] = ch["k"] * ch["beta"]
        ch["a2"] = _bdot_nt(jnp.concatenate([ch["kb"], ch["q"]], axis=0), ch["k"])
    for ch in chains:
        a2 = ch.pop("a2")
        n_mat = -jnp.where(ch["strict"], a2[:CHUNK] * ch["decay"], 0.0)
        ch["attn"] = a2[CHUNK:] * ch["decay"]
        n_diag = jnp.where(same16, n_mat, 0.0)
        ch["n32"] = jnp.where(same32, n_mat, 0.0) - n_diag
        ch["n64"] = jnp.where(same32, 0.0, n_mat)
        ch["p"] = eye + n_diag
        ch["n"] = _bdot(n_diag, n_diag)
    for it in range(3):
        for ch in chains:
            ch["p"] = ch["p"] + _bdot(ch["p"], ch["n"])
            if it < 2:
                ch["n"] = _bdot(ch["n"], ch["n"])
    for off in ("n32", "n64"):
        for ch in chains:
            ch["x"] = _bdot(ch["p"], ch[off])
        for ch in chains:
            ch["p"] = ch["p"] + _bdot(ch["x"], ch["p"])
    for ch in chains:
        ch["e_b"] = jnp.exp(ch["bc"])
        uw = _bdot(ch["p"], jnp.concatenate([ch["v"] * ch["beta"], ch["kb"] * ch["e_b"]], axis=1))
        ch["u"], ch["w"] = uw[:, :GDN_DV], uw[:, GDN_DV:]
    for ch in chains:
        ch["s"] = s_ref[ch["d"], ch["h"]]
        ch["wq"] = _bdot(jnp.concatenate([ch["w"], ch["q"] * ch["e_b"]], axis=0), ch["s"])
    for ch in chains:
        wq = ch["wq"]
        v_new = ch["u"] - wq[:CHUNK]
        ch["o_ref"][0, :, ch["hs"]] = (wq[CHUNK:] + _bdot(ch["attn"], v_new)).astype(ch["o_ref"].dtype)
        k_dec = ch["k"] * jnp.exp(ch["bl"] - ch["bc"])
        s_ref[ch["d"], ch["h"]] = jnp.exp(ch["bl"]) * ch["s"] + _bdot_tn(k_dec, v_new)


def gdn_scan(qkv, small, a_log, dt_bias, n_ctx):
    bsz, s, _ = qkv.shape
    n_all = s // CHUNK
    lane_row = lambda t: jnp.zeros((1, SMALL_WIDTH), F32).at[0, A_COL:A_COL + N_DIR * GDN_HEADS].set(
        t.astype(F32).reshape(-1))

    def specs(rev):
        def row(t):
            return _rev_chunk(t, n_ctx, n_all) if rev else t
        return [pl.BlockSpec((1, CHUNK, GDN_QK), lambda b, t: (b, row(t), 0)),
                pl.BlockSpec((1, CHUNK, GDN_QK), lambda b, t: (b, row(t), 1)),
                pl.BlockSpec((1, CHUNK, GDN_V), lambda b, t: (b, row(t), 2)),
                pl.BlockSpec((1, CHUNK, SMALL_WIDTH), lambda b, t: (b, row(t), 0))]

    const = pl.BlockSpec((1, SMALL_WIDTH), lambda b, t: (0, 0))
    out_f = pl.BlockSpec((1, CHUNK, GDN_V), lambda b, t: (b, t, 0))
    out_r = pl.BlockSpec((1, CHUNK, GDN_V), lambda b, t: (b, _rev_chunk(t, n_ctx, n_all), 0))
    return pl.pallas_call(
        _gdn_kernel,
        grid=(bsz, n_all),
        in_specs=specs(False) + specs(True) + [const, const],
        out_specs=[out_f, out_r],
        out_shape=[jax.ShapeDtypeStruct((bsz, s, GDN_V), BF16)] * 2,
        scratch_shapes=[pltpu.VMEM((N_DIR, GDN_HEADS, GDN_DK, GDN_DV), F32)],
        compiler_params=_params("parallel", "arbitrary"),
        name="gdn_scan",
    )(qkv, qkv, qkv, small, qkv, qkv, qkv, small, lane_row(a_log), lane_row(dt_bias))


FFN_COL_TILE = D_FF // 2


FFN_EXT_ROWS = ROW_TILE + 2 * GRID_W
FFN_SUB_COLS = 256


def _ffn_gate_kernel(a_ref, up_ref, dn_ref, v_ref, w_ref, o_ref, shift_ref, *, rows):
    g, n = GRID_W, FFN_EXT_ROWS

    @pl.when((pl.program_id(0) == 0) & (pl.program_id(1) == 0))
    def _():
        ri = lax.broadcasted_iota(jnp.int32, (n, n), 0)
        ci = lax.broadcasted_iota(jnp.int32, (n, n), 1)
        for kind in range(2):
            col = (ri & (g - 1)) if kind == 0 else ri - g
            last_col = g - 1 if kind == 0 else ROW_TILE - 1
            shift_ref[kind, 0] = jnp.where((ci == ri - 1) & (col != 0), 1.0, 0.0).astype(BF16)
            shift_ref[kind, 1] = jnp.where((ci == ri + 1) & (col != last_col), 1.0, 0.0).astype(BF16)

    is_ctx, first, last = rows.flags(pl.program_id(0))
    kind = jnp.where(is_ctx, 1, 0)
    halo_zero = jnp.zeros(up_ref.shape, up_ref.dtype)
    up = jnp.where(first, halo_zero, up_ref[...])
    dn = jnp.where(last, halo_zero, dn_ref[...])
    ext_all = jnp.concatenate([up, a_ref[...], dn], axis=0)
    vert = jnp.where(is_ctx, 0.0, 1.0)
    for c0 in range(0, FFN_COL_TILE, FFN_SUB_COLS):
        cs = slice(c0, c0 + FFN_SUB_COLS)
        ext = ext_all[:, cs]
        shifted = (jnp.dot(shift_ref[kind, 0], ext, preferred_element_type=F32),
                   ext.astype(F32),
                   jnp.dot(shift_ref[kind, 1], ext, preferred_element_type=F32))
        acc = None
        for dr in (-1, 0, 1):
            for dc in (-1, 0, 1):
                val = shifted[dc + 1][g + dr * g:g + dr * g + ROW_TILE]
                wt = w_ref[3 * (dr + 1) + dc + 1:3 * (dr + 1) + dc + 2, cs]
                if dr != 0:
                    wt = wt * vert
                acc = val * wt if acc is None else acc + val * wt
        o_ref[:, cs] = (acc * _sigmoid(acc) * v_ref[:, cs].astype(F32)).astype(o_ref.dtype)


def ffn_gate(up, rows, w_conv):
    m = up.shape[0]
    nc = D_FF // FFN_COL_TILE
    gb = ROW_TILE // GRID_W
    n_gblocks = m // GRID_W
    return pl.pallas_call(
        functools.partial(_ffn_gate_kernel, rows=rows),
        grid=(rows.n_tiles, nc),
        in_specs=[pl.BlockSpec((ROW_TILE, FFN_COL_TILE), lambda i, j: (i, j)),
                  pl.BlockSpec((GRID_W, FFN_COL_TILE), lambda i, j: (jnp.maximum(i * gb - 1, 0), j)),
                  pl.BlockSpec((GRID_W, FFN_COL_TILE),
                               lambda i, j: (jnp.minimum((i + 1) * gb, n_gblocks - 1), j)),
                  pl.BlockSpec((ROW_TILE, FFN_COL_TILE), lambda i, j: (i, nc + j)),
                  pl.BlockSpec((9, FFN_COL_TILE), lambda i, j: (0, j))],
        out_specs=pl.BlockSpec((ROW_TILE, FFN_COL_TILE), lambda i, j: (i, j)),
        out_shape=jax.ShapeDtypeStruct((m, D_FF), BF16),
        scratch_shapes=[pltpu.VMEM((2, 2, FFN_EXT_ROWS, FFN_EXT_ROWS), BF16)],
        compiler_params=_params("arbitrary", "arbitrary"), name="ffn_gate",
    )(up, up, up, up, w_conv.reshape(9, D_FF))


BRANCH_WIDTH = GLA_V + GDN_V + POOL_WIDTH
Z_GLA_COL = 2 * GLA_QK + GLA_V
Z_GDN_COL = GDN_QKV_COL + GDN_QKV
POOL_COL = Z_GDN_COL + GDN_V
GATES_COL = POOL_COL + POOL_WIDTH
POOL_HALO = BF16_SUBLANES
assert POOL_HALO >= max(POOL_WINDOWS) // 2


def _headnorm_gate(o, z, nw, n_heads, width, o_ref, col0):
    for h in range(n_heads):
        sl = slice(h * width, (h + 1) * width)
        oh, zh = o[:, sl], z[:, sl]
        inv = lax.rsqrt(jnp.mean(oh * oh, axis=-1, keepdims=True) + EPS)
        o_ref[:, col0 + h * width:col0 + (h + 1) * width] = (
            oh * inv * nw * (zh * _sigmoid(zh))).astype(o_ref.dtype)


def _merge_prep_kernel(gf_ref, gr_ref, df_ref, dr_ref, za_ref, zb_ref, u_ref, uup_ref, udn_ref,
                       gnw_ref, dnw_ref, pw_ref, ps_ref, o_ref, win_ref, *, rows):
    halo, n = POOL_HALO, ROW_TILE + 2 * POOL_HALO

    @pl.when(pl.program_id(0) == 0)
    def _():
        ri = lax.broadcasted_iota(jnp.int32, (ROW_TILE, n), 0)
        ci = lax.broadcasted_iota(jnp.int32, (ROW_TILE, n), 1) - halo
        for gi, win in enumerate(POOL_WINDOWS):
            lo = ri - win // 2
            win_ref[gi] = jnp.where((ci >= lo) & (ci < lo + win), 1.0, 0.0).astype(BF16)

    _, first, last = rows.flags(pl.program_id(0))
    _headnorm_gate(gf_ref[...].astype(F32) + gr_ref[...].astype(F32), za_ref[...].astype(F32),
                   gnw_ref[...], GLA_HEADS, GLA_DV, o_ref, 0)
    _headnorm_gate(df_ref[...].astype(F32) + dr_ref[...].astype(F32), zb_ref[...].astype(F32),
                   dnw_ref[...], GDN_HEADS, GDN_DV, o_ref, GLA_V)
    halo_zero = jnp.zeros(uup_ref.shape, uup_ref.dtype)
    u = u_ref[...]
    ext = jnp.concatenate([jnp.where(first, halo_zero, uup_ref[...]), u,
                           jnp.where(last, halo_zero, udn_ref[...])], axis=0)
    t = lax.broadcasted_iota(jnp.int32, (ROW_TILE, 1), 0)
    for gi, win in enumerate(POOL_WINDOWS):
        sl = slice(gi * POOL_GROUP, (gi + 1) * POOL_GROUP)
        half = win // 2
        win_sum = jnp.dot(win_ref[gi], ext[:, sl], preferred_element_type=F32)
        lo_clip = jnp.where(first, jnp.maximum(half - t, 0), 0)
        hi_clip = jnp.where(last, jnp.maximum(t - half + win - ROW_TILE, 0), 0)
        cnt = (win - lo_clip - hi_clip).astype(F32)
        pg = win_sum / cnt - u[:, sl].astype(F32)
        yp = _bdot(pg, pw_ref[gi]) * ps_ref[:, sl]
        o_ref[:, GLA_V + GDN_V + gi * POOL_GROUP:GLA_V + GDN_V + (gi + 1) * POOL_GROUP] = yp.astype(o_ref.dtype)


def merge_prep(p, o_gla, o_gdn, p_rows, rows, gla_nw, gdn_nw, pool_w, pool_scale):
    hb = ROW_TILE // BF16_SUBLANES
    n_hblocks = p.shape[0] // BF16_SUBLANES
    blk = lambda i: rows.block_of(i, p_rows)
    wide = lambda col: pl.BlockSpec((ROW_TILE, 1024), lambda i: (blk(i), col // 1024))
    const = lambda shape: pl.BlockSpec(shape, lambda i: (0,) * len(shape))
    pc = POOL_COL // POOL_WIDTH
    return pl.pallas_call(
        functools.partial(_merge_prep_kernel, rows=rows),
        grid=(rows.n_tiles,),
        in_specs=[wide(0), wide(0), wide(0), wide(0), wide(Z_GLA_COL), wide(Z_GDN_COL), wide(POOL_COL),
                  pl.BlockSpec((POOL_HALO, POOL_WIDTH), lambda i: (jnp.maximum(blk(i) * hb - 1, 0), pc)),
                  pl.BlockSpec((POOL_HALO, POOL_WIDTH),
                               lambda i: (jnp.minimum((blk(i) + 1) * hb, n_hblocks - 1), pc)),
                  const((1, GLA_DV)), const((1, GDN_DV)),
                  const((POOL_GROUPS, POOL_GROUP, POOL_GROUP)), const((1, POOL_WIDTH))],
        out_specs=pl.BlockSpec((ROW_TILE, BRANCH_WIDTH), lambda i: (i, 0)),
        out_shape=jax.ShapeDtypeStruct((rows.n_tiles * ROW_TILE, BRANCH_WIDTH), BF16),
        scratch_shapes=[pltpu.VMEM((POOL_GROUPS, ROW_TILE, ROW_TILE + 2 * POOL_HALO), BF16)],
        compiler_params=_params("arbitrary"), name="merge_prep",
    )(o_gla[0], o_gla[1], o_gdn[0], o_gdn[1], p, p, p, p, p,
      gla_nw.reshape(1, GLA_DV), gdn_nw.reshape(1, GDN_DV), pool_w.astype(BF16),
      pool_scale.reshape(1, POOL_WIDTH))


def _branch_kernel(ya, yb, yc, ga, gb, gc, wa, wb, wc, o_ref):
    acc = None
    for y_ref, g_ref, w_ref in ((ya, ga, wa), (yb, gb, wb), (yc, gc, wc)):
        term = _sigmoid(g_ref[...].astype(F32)) * jnp.dot(y_ref[...], w_ref[...], preferred_element_type=F32)
        acc = term if acc is None else acc + term
    o_ref[...] = acc.astype(o_ref.dtype)


def branch_merge(ycat, p, p_rows, rows, w_gla, w_gdn, w_pool):
    blk = lambda i: rows.block_of(i, p_rows)
    ysp = lambda k: pl.BlockSpec((ROW_TILE, 1024), lambda i: (i, k))
    gsp = lambda k: pl.BlockSpec((ROW_TILE, D_MODEL), lambda i: (blk(i), GATES_COL // D_MODEL + k))
    wsp = pl.BlockSpec((1024, D_MODEL), lambda i: (0, 0))
    return pl.pallas_call(
        _branch_kernel,
        grid=(rows.n_tiles,),
        in_specs=[ysp(0), ysp(1), ysp(2), gsp(0), gsp(1), gsp(2), wsp, wsp, wsp],
        out_specs=pl.BlockSpec((ROW_TILE, D_MODEL), lambda i: (i, 0)),
        out_shape=jax.ShapeDtypeStruct((rows.n_tiles * ROW_TILE, D_MODEL), BF16),
        compiler_params=_params("parallel"), name="branch_merge",
    )(ycat, ycat, ycat, p, p, p, w_gla, w_gdn, w_pool)


def permute_w_in(w):
    q, k, v, zg, lr, gqkv, ga, gbt, gz, pool, gates = _split_cols(w, IN_SPLITS)
    main = jnp.concatenate([q, k, v, zg, gqkv, gz, pool, gates], axis=1)
    small = jnp.concatenate([lr, ga, gbt, jnp.zeros((w.shape[0], SMALL_WIDTH - SMALL_COLS), w.dtype)], axis=1)
    return main.astype(BF16), small.astype(BF16)


def hybrid_mixer(h, rows, out_rows, bsz, w_main, w_small, gla_w2, gla_b, gla_nw, gdn_cw, gdn_alog,
                 gdn_dtb, gdn_nw, pool_w, pool_scale, w_br_gla, w_br_gdn, w_br_pool, w_out):
    m = h.shape[0]
    s = m // bsz
    p = pmm(h, w_main, BF16)
    small = pmm(h, w_small)
    p3, small3 = p.reshape(bsz, s, MAIN_WIDTH), small.reshape(bsz, s, SMALL_WIDTH)
    n_ctx = (ROW_TILE if rows.has_ctx else 0) // CHUNK
    o_gla = [t.reshape(m, GLA_V) for t in gla_scan(p3, small3, gla_w2.astype(BF16), gla_b, n_ctx)]
    qkv = gdn_prep(p, rows, gdn_cw).reshape(bsz, s, GDN_QKV)
    o_gdn = [t.reshape(m, GDN_V) for t in gdn_scan(qkv, small3, gdn_alog, gdn_dtb, n_ctx)]
    ycat = merge_prep(p, o_gla, o_gdn, rows, out_rows, gla_nw, gdn_nw, pool_w, pool_scale)
    mrg = branch_merge(ycat, p, rows, out_rows, w_br_gla, w_br_gdn, w_br_pool)
    return pmm(mrg, w_out, BF16)


def kernel(x, c, ctx, c_ctx, ada_w, ada_b, norm1_w, norm2_w, w_in, gla_lr_w2, gla_lr_b,
           gla_norm_w, gdn_conv_w, gdn_a_log, gdn_dt_bias, gdn_norm_w, pool_w, pool_scale,
           w_br_gla, w_br_gdn, w_br_pool, w_out, ffn_up, ffn_conv, ffn_down, final_norm_w):
    B, T, _ = x.shape
    assert ctx.shape[1] == ROW_TILE and T % ROW_TILE == 0 and (T // GRID_W) % (ROW_TILE // GRID_W) == 0
    s_all = ROW_TILE + T
    all_rows = Rows(B, True, T // ROW_TILE)
    lat_rows = Rows(B, False, T // ROW_TILE)
    xa = jnp.concatenate([ctx, x], axis=1).reshape(B * s_all, D_MODEL)
    cond = jnp.concatenate([c, c_ctx[None, :], jnp.zeros((16 - B - 1, D_MODEL), F32)], axis=0)
    cond = jax.nn.silu(cond)
    mods = [(pmm(cond, ada_w[l].astype(BF16)) + ada_b[l]).reshape(16, 6, D_MODEL) for l in range(DEPTH)]
    _, h = resid_norm(xa, all_rows, all_rows, norm1_w[0], mod=mods[0], shift_idx=0, scale_idx=1)
    x_rows = all_rows
    for l in range(DEPTH):
        last = l == DEPTH - 1
        rows = lat_rows if last else all_rows
        w_main, w_small = permute_w_in(w_in[l])
        y = hybrid_mixer(h, x_rows, rows, B, w_main, w_small, gla_lr_w2[l], gla_lr_b[l], gla_norm_w[l],
                         gdn_conv_w[l], gdn_a_log[l], gdn_dt_bias[l], gdn_norm_w[l],
                         pool_w[l], pool_scale[l], w_br_gla[l].astype(BF16),
                         w_br_gdn[l].astype(BF16), w_br_pool[l].astype(BF16), w_out[l].astype(BF16))
        xa, h2 = resid_norm(xa, x_rows, rows, norm2_w[l], resid=y, gate_mod=mods[l], gate_idx=2,
                            mod=mods[l], shift_idx=3, scale_idx=4)
        x_rows = rows
        up = pmm(h2, ffn_up[l].astype(BF16), BF16)
        gated = ffn_gate(up, rows, ffn_conv[l])
        dn = pmm(gated, ffn_down[l].astype(BF16), BF16)
        if last:
            _, out = resid_norm(xa, x_rows, rows, final_norm_w, resid=dn, gate_mod=mods[l], gate_idx=5,
                                out_dtype=F32)
            return out.reshape(B, T, D_MODEL)
        xa, h = resid_norm(xa, x_rows, rows, norm1_w[l + 1], resid=dn, gate_mod=mods[l], gate_idx=5,
                           mod=mods[l + 1], shift_idx=0, scale_idx=1)
```

```python
import functools

import jax
import jax.numpy as jnp
from jax import lax
from jax.experimental import pallas as pl
from jax.experimental.pallas import tpu as pltpu

D_MODEL = 2048
DEPTH = 4
GRID_W = 64
CHUNK = 64
N_DIR = 2
N_BRANCH = 3
EPS = 1e-6

GLA_HEADS = 4
GLA_DK = 128
GLA_DV = 256
GLA_RANK = 16
GLA_TAU = 16.0

GDN_HEADS = 8
GDN_DK = 128
GDN_DV = 128

POOL_WINDOWS = (2, 4, 8, 16)
POOL_GROUPS = 4
POOL_GROUP = 256

D_FF = 5632

GLA_QK = GLA_HEADS * GLA_DK
GLA_V = GLA_HEADS * GLA_DV
GDN_QK = GDN_HEADS * GDN_DK
GDN_V = GDN_HEADS * GDN_DV
GDN_QKV = 2 * GDN_QK + GDN_V
POOL_WIDTH = POOL_GROUPS * POOL_GROUP
IN_SPLITS = (GLA_QK, GLA_QK, GLA_V, GLA_V, N_DIR * GLA_RANK,
             GDN_QKV, N_DIR * GDN_HEADS, N_DIR * GDN_HEADS, GDN_V,
             POOL_WIDTH, N_BRANCH * D_MODEL)
F32 = jnp.float32
BF16 = jnp.bfloat16
HIGHEST = lax.Precision.HIGHEST

V7X_VMEM_BYTES = 64 * 1024 * 1024
VMEM_LIMIT_BYTES = V7X_VMEM_BYTES * 3 // 4
MM_TILE_BUDGET_BYTES = V7X_VMEM_BYTES * 5 // 8
LANES = 128
BF16_SUBLANES = 16

SMALL_WIDTH = LANES
SMALL_COLS = N_DIR * GLA_RANK + 2 * N_DIR * GDN_HEADS
A_COL = N_DIR * GLA_RANK
BT_COL = A_COL + N_DIR * GDN_HEADS
MAIN_SPLITS = (GLA_QK, GLA_QK, GLA_V, GLA_V, GDN_QKV, GDN_V, POOL_WIDTH, N_BRANCH * D_MODEL)
MAIN_WIDTH = sum(MAIN_SPLITS)
GDN_QKV_COL = 2 * GLA_QK + 2 * GLA_V

ROW_TILE = 256


def _split_cols(t, sizes):
    parts, start = [], 0
    for size in sizes:
        parts.append(t[..., start:start + size])
        start += size
    return parts


def _params(*sem):
    return pltpu.CompilerParams(dimension_semantics=sem, vmem_limit_bytes=VMEM_LIMIT_BYTES)


def _sigmoid(x):
    return 1.0 / (1.0 + jnp.exp(-x))


def _softplus(x):
    return jnp.maximum(x, 0.0) + jnp.log(1.0 + jnp.exp(-jnp.abs(x)))


def _mm_kernel(x_ref, w_ref, o_ref):
    o_ref[...] = jnp.dot(x_ref[...].astype(BF16), w_ref[...].astype(BF16),
                         preferred_element_type=F32).astype(o_ref.dtype)


def _mm_tiles(m, k, n, x_bytes, w_bytes, o_bytes):
    best = None
    for tm in (1024, 512, 256, 128, 64, 32, 16, 8):
        if m % tm:
            continue
        for tn in (2048, 1024, 512, 256, 128):
            if n % tn:
                continue
            need = 2 * (tm * k * x_bytes + k * tn * w_bytes + tm * tn * o_bytes)
            if need > MM_TILE_BUDGET_BYTES:
                continue
            score = tm * tn / (tm + tn)
            if best is None or score > best[0]:
                best = (score, tm, tn)
    if best is None:
        raise ValueError(f"no matmul tiling for {(m, k, n)}")
    return best[1], best[2]


def pmm(x, w, out_dtype=F32):
    m, k = x.shape
    n = w.shape[1]
    tm, tn = _mm_tiles(m, k, n, x.dtype.itemsize, w.dtype.itemsize, jnp.dtype(out_dtype).itemsize)
    return pl.pallas_call(
        _mm_kernel,
        grid=(m // tm, n // tn),
        in_specs=[pl.BlockSpec((tm, k), lambda i, j: (i, 0)),
                  pl.BlockSpec((k, tn), lambda i, j: (0, j))],
        out_specs=pl.BlockSpec((tm, tn), lambda i, j: (i, j)),
        out_shape=jax.ShapeDtypeStruct((m, n), out_dtype),
        compiler_params=_params("parallel", "arbitrary"),
        name="mm",
    )(x, w)


class Rows:
    def __init__(self, batch, has_ctx, lat_tiles):
        self.batch, self.has_ctx, self.lat_tiles = batch, has_ctx, lat_tiles
        self.per_batch = lat_tiles + (1 if has_ctx else 0)
        self.n_tiles = batch * self.per_batch

    def block_of(self, i, other):
        if other.has_ctx == self.has_ctx:
            return i
        assert other.has_ctx and not self.has_ctx
        return (i // self.per_batch) * other.per_batch + 1 + i % self.per_batch

    def mod_row(self, i):
        b = i // self.per_batch
        if not self.has_ctx:
            return b
        return jnp.where(i % self.per_batch == 0, self.batch, b)

    def flags(self, i):
        r = i % self.per_batch
        if not self.has_ctx:
            return False, r == 0, r == self.per_batch - 1
        is_ctx = r == 0
        return is_ctx, is_ctx | (r == 1), is_ctx | (r == self.per_batch - 1)


def _norm_kernel(*refs, has_resid, gate_idx, has_mod, shift_idx, scale_idx, write_x):
    refs = list(refs)
    x_ref = refs.pop(0)
    x = x_ref[...]
    if has_resid:
        y_ref, gmod_ref = refs.pop(0), refs.pop(0)
        x = x + gmod_ref[0, gate_idx:gate_idx + 1, :] * y_ref[...].astype(F32)
    nw_ref = refs.pop(0)
    mod_ref = refs.pop(0) if has_mod else None
    if write_x:
        refs.pop(0)[...] = x
    h_ref = refs.pop(0)
    y = x * lax.rsqrt(jnp.mean(x * x, axis=-1, keepdims=True) + EPS) * nw_ref[...]
    if has_mod:
        y = y * (1.0 + mod_ref[0, scale_idx:scale_idx + 1, :]) + mod_ref[0, shift_idx:shift_idx + 1, :]
    h_ref[...] = y.astype(h_ref.dtype)


def resid_norm(x, x_rows, rows, norm_w, *, resid=None, gate_mod=None, gate_idx=0,
               mod=None, shift_idx=0, scale_idx=1, out_dtype=BF16):
    d = x.shape[-1]
    row = lambda i: (i, 0)
    modspec = pl.BlockSpec((1, 6, d), lambda i: (rows.mod_row(i), 0, 0))
    args = [x]
    in_specs = [pl.BlockSpec((ROW_TILE, d), lambda i: (rows.block_of(i, x_rows), 0))]
    if resid is not None:
        args += [resid, gate_mod]
        in_specs += [pl.BlockSpec((ROW_TILE, d), row), modspec]
    args.append(norm_w.reshape(1, d))
    in_specs.append(pl.BlockSpec((1, d), lambda i: (0, 0)))
    if mod is not None:
        args.append(mod)
        in_specs.append(modspec)
    m = rows.n_tiles * ROW_TILE
    out_shape, out_specs = [], []
    if resid is not None:
        out_shape.append(jax.ShapeDtypeStruct((m, d), F32))
        out_specs.append(pl.BlockSpec((ROW_TILE, d), row))
    out_shape.append(jax.ShapeDtypeStruct((m, d), out_dtype))
    out_specs.append(pl.BlockSpec((ROW_TILE, d), row))
    outs = pl.pallas_call(
        functools.partial(_norm_kernel, has_resid=resid is not None, gate_idx=gate_idx,
                          has_mod=mod is not None, shift_idx=shift_idx, scale_idx=scale_idx,
                          write_x=resid is not None),
        grid=(rows.n_tiles,), in_specs=in_specs, out_specs=out_specs, out_shape=out_shape,
        compiler_params=_params("parallel"), name="resid_norm",
    )(*args)
    return (outs[0], outs[1]) if resid is not None else (None, outs[0])


def _gdn_prep_kernel(x_ref, up_ref, dn_ref, w_ref, o_ref, ext_ref, *, rows):
    halo = BF16_SUBLANES
    _, first, last = rows.flags(pl.program_id(0))
    ext_ref[0:halo] = jnp.where(first, 0.0, up_ref[...].astype(F32))
    ext_ref[halo:halo + ROW_TILE] = x_ref[...].astype(F32)
    ext_ref[halo + ROW_TILE:2 * halo + ROW_TILE] = jnp.where(last, 0.0, dn_ref[...].astype(F32))
    y = (ext_ref[pl.ds(halo - 1, ROW_TILE), :] * w_ref[0:1, :]
         + ext_ref[pl.ds(halo, ROW_TILE), :] * w_ref[1:2, :]
         + ext_ref[pl.ds(halo + 1, ROW_TILE), :] * w_ref[2:3, :])
    y = y * _sigmoid(y)
    for hd in range(2 * GDN_HEADS):
        sl = slice(hd * GDN_DK, (hd + 1) * GDN_DK)
        seg = y[:, sl]
        inv = lax.rsqrt(jnp.sum(seg * seg, axis=-1, keepdims=True) + EPS)
        if hd < GDN_HEADS:
            inv = inv * GDN_DK ** -0.5
        o_ref[:, sl] = (seg * inv).astype(o_ref.dtype)
    o_ref[:, 2 * GDN_QK:] = y[:, 2 * GDN_QK:].astype(o_ref.dtype)


def gdn_prep(p, rows, conv_w):
    m = p.shape[0]
    cb = GDN_QKV_COL // GDN_QKV
    hb = ROW_TILE // BF16_SUBLANES
    n_hblocks = m // BF16_SUBLANES
    return pl.pallas_call(
        functools.partial(_gdn_prep_kernel, rows=rows),
        grid=(rows.n_tiles,),
        in_specs=[pl.BlockSpec((ROW_TILE, GDN_QKV), lambda i: (i, cb)),
                  pl.BlockSpec((BF16_SUBLANES, GDN_QKV), lambda i: (jnp.maximum(i * hb - 1, 0), cb)),
                  pl.BlockSpec((BF16_SUBLANES, GDN_QKV),
                               lambda i: (jnp.minimum((i + 1) * hb, n_hblocks - 1), cb)),
                  pl.BlockSpec((3, GDN_QKV), lambda i: (0, 0))],
        out_specs=pl.BlockSpec((ROW_TILE, GDN_QKV), lambda i: (i, 0)),
        out_shape=jax.ShapeDtypeStruct((m, GDN_QKV), BF16),
        scratch_shapes=[pltpu.VMEM((ROW_TILE + 2 * BF16_SUBLANES, GDN_QKV), F32)],
        compiler_params=_params("parallel"), name="gdn_prep",
    )(p, p, p, conv_w)


def _bdot(a, b):
    return jnp.dot(a.astype(BF16), b.astype(BF16), preferred_element_type=F32)


def _bdot_nt(a, b):
    return lax.dot_general(a.astype(BF16), b.astype(BF16), (((1,), (1,)), ((), ())),
                           preferred_element_type=F32)


def _bdot_tn(a, b):
    return lax.dot_general(a.astype(BF16), b.astype(BF16), (((0,), (0,)), ((), ())),
                           preferred_element_type=F32)


def _rev_chunk(t, n_ctx, n_all):
    return jnp.where(t < n_ctx, n_ctx - 1 - t, n_all + n_ctx - 1 - t)


def _chunk_masks(d):
    ri = lax.broadcasted_iota(jnp.int32, (CHUNK, CHUNK), 0)
    ci = lax.broadcasted_iota(jnp.int32, (CHUNK, CHUNK), 1)
    lower, upper = ci <= ri, ci >= ri
    if d == 0:
        return lower, ci < ri, lower.astype(F32), upper.astype(F32)
    return upper, ci > ri, upper.astype(F32), lower.astype(F32)


SCAN_CHUNKS = 2


def _cumsum_dot(tri, x):
    tb = tri.astype(BF16)
    hi = x.astype(BF16)
    rest = x - hi.astype(F32)
    mid = rest.astype(BF16)
    lo = (rest - mid.astype(F32)).astype(BF16)
    return (jnp.dot(tb, hi, preferred_element_type=F32) + jnp.dot(tb, mid, preferred_element_type=F32)
            + jnp.dot(tb, lo, preferred_element_type=F32))


def _gla_kernel(qf, kf, vf, sf, qr, kr, vr, sr, w2_ref, b_ref, of, orr, st_ref):
    @pl.when(pl.program_id(1) == 0)
    def _():
        st_ref[...] = jnp.zeros_like(st_ref)

    dirs = ((qf, kf, vf, sf, of), (qr, kr, vr, sr, orr))
    slots = [[] for _ in range(SCAN_CHUNKS)]
    row_slice = lambda d, slot: slice((slot if d == 0 else SCAN_CHUNKS - 1 - slot) * CHUNK,
                                      (slot if d == 0 else SCAN_CHUNKS - 1 - slot) * CHUNK + CHUNK)
    blocks = [(d, slot) for d in range(N_DIR) for slot in range(SCAN_CHUNKS)]
    zs = {(d, slot): _bdot(dirs[d][3][0, row_slice(d, slot), d * GLA_RANK:(d + 1) * GLA_RANK], w2_ref[d])
          + b_ref[d] for d, slot in blocks}
    bcums = {(d, slot): _cumsum_dot(_chunk_masks(d)[2], -_softplus(-zs[d, slot]) / GLA_TAU)
             for d, slot in blocks}
    for d, (q_ref, k_ref, v_ref, s_ref, o_ref) in enumerate(dirs):
        incl = _chunk_masks(d)[0]
        last = CHUNK - 1 if d == 0 else 0
        for slot in range(SCAN_CHUNKS):
            rs = row_slice(d, slot)
            bcum = bcums[d, slot]
            q_all, k_all, v_all = q_ref[0, rs, :].astype(F32), k_ref[0, rs, :].astype(F32), v_ref[0, rs, :]
            for h in range(GLA_HEADS):
                ks = slice(h * GLA_DK, (h + 1) * GLA_DK)
                vs = slice(h * GLA_DV, (h + 1) * GLA_DV)
                b = bcum[:, ks]
                bl = bcum[last:last + 1, ks]
                k = k_all[:, ks]
                ch = dict(d=d, h=h, vs=vs, rs=rs, o_ref=o_ref, bl=bl, v=v_all[:, vs], incl=incl)
                ch["q_dec"] = q_all[:, ks] * (GLA_DK ** -0.5) * jnp.exp(b)
                ch["k_neg"] = k * jnp.exp(-b)
                ch["k_dec"] = k * jnp.exp(bl - b)
                slots[slot].append(ch)
    chains = [ch for slot in slots for ch in slot]
    for ch in chains:
        ch["scores"] = jnp.where(ch["incl"], _bdot_nt(ch["q_dec"], ch["k_neg"]), 0.0)
    for ch in chains:
        ch["intra"] = _bdot(ch["scores"], ch["v"])
        ch["kv"] = _bdot_tn(ch["v"], ch["k_dec"])
    state = {(d, h): st_ref[d, h] for d in range(N_DIR) for h in range(GLA_HEADS)}
    for slot in slots:
        for ch in slot:
            key = ch["d"], ch["h"]
            out = ch["intra"] + _bdot_nt(ch["q_dec"], state[key])
            ch["o_ref"][0, ch["rs"], ch["vs"]] = out.astype(ch["o_ref"].dtype)
            state[key] = jnp.exp(ch["bl"]) * state[key] + ch["kv"]
    for (d, h), st in state.items():
        st_ref[d, h] = st


def gla_scan(p, small, w2, bias, n_ctx):
    bsz, s, _ = p.shape
    rows_per_step = SCAN_CHUNKS * CHUNK
    n_ctx_steps, n_steps = n_ctx // SCAN_CHUNKS, s // rows_per_step
    assert n_ctx % SCAN_CHUNKS == 0 and s % rows_per_step == 0

    def specs(rev):
        def row(t):
            return _rev_chunk(t, n_ctx_steps, n_steps) if rev else t
        return [pl.BlockSpec((1, rows_per_step, GLA_QK), lambda b, t: (b, row(t), 0)),
                pl.BlockSpec((1, rows_per_step, GLA_QK), lambda b, t: (b, row(t), 1)),
                pl.BlockSpec((1, rows_per_step, GLA_V), lambda b, t: (b, row(t), 2 * GLA_QK // GLA_V)),
                pl.BlockSpec((1, rows_per_step, SMALL_WIDTH), lambda b, t: (b, row(t), 0))]

    out_f = pl.BlockSpec((1, rows_per_step, GLA_V), lambda b, t: (b, t, 0))
    out_r = pl.BlockSpec((1, rows_per_step, GLA_V),
                         lambda b, t: (b, _rev_chunk(t, n_ctx_steps, n_steps), 0))
    return pl.pallas_call(
        _gla_kernel,
        grid=(bsz, n_steps),
        in_specs=specs(False) + specs(True) + [
            pl.BlockSpec((N_DIR, GLA_RANK, GLA_QK), lambda b, t: (0, 0, 0)),
            pl.BlockSpec((N_DIR, 1, GLA_QK), lambda b, t: (0, 0, 0))],
        out_specs=[out_f, out_r],
        out_shape=[jax.ShapeDtypeStruct((bsz, s, GLA_V), BF16)] * 2,
        scratch_shapes=[pltpu.VMEM((N_DIR, GLA_HEADS, GLA_DV, GLA_DK), F32)],
        compiler_params=_params("parallel", "arbitrary"),
        name="gla_scan",
    )(p, p, p, small, p, p, p, small, w2, bias.reshape(N_DIR, 1, GLA_QK))


def _gdn_kernel(qf, kf, vf, sf, qr, kr, vr, sr, alog_ref, dtb_ref, of, orr, s_ref):
    @pl.when(pl.program_id(1) == 0)
    def _():
        s_ref[...] = jnp.zeros_like(s_ref)

    ri = lax.broadcasted_iota(jnp.int32, (CHUNK, CHUNK), 0)
    ci = lax.broadcasted_iota(jnp.int32, (CHUNK, CHUNK), 1)
    eye = (ri == ci).astype(F32)
    same16 = (ri >> 4) == (ci >> 4)
    same32 = (ri >> 5) == (ci >> 5)
    nh = GDN_HEADS
    slots = [[] for _ in range(SCAN_CHUNKS)]
    for d, (q_ref, k_ref, v_ref, sm_ref, o_ref) in enumerate(
            ((qf, kf, vf, sf, of), (qr, kr, vr, sr, orr))):
        incl, strict, tri, tri_t = _chunk_masks(d)
        for slot in range(SCAN_CHUNKS):
            sub = slot if d == 0 else SCAN_CHUNKS - 1 - slot
            rs = slice(sub * CHUNK, (sub + 1) * CHUNK)
            sm = sm_ref[0, rs, :]
            g_all = -jnp.exp(alog_ref[...]) * _softplus(sm + dtb_ref[...])
            beta_all = _sigmoid(sm)
            gcol = g_all[:, A_COL:A_COL + N_DIR * nh]
            grow = g_all.T[A_COL:A_COL + N_DIR * nh, :]
            bcol = jnp.dot(tri, gcol, precision=HIGHEST, preferred_element_type=F32)
            brow = jnp.dot(grow, tri_t, precision=HIGHEST, preferred_element_type=F32)
            last = CHUNK - 1 if d == 0 else 0
            btot = bcol[last:last + 1, :]
            q_all = q_ref[0, rs, :].astype(F32)
            k_all = k_ref[0, rs, :].astype(F32)
            v_all = v_ref[0, rs, :].astype(F32)
            for h in range(nh):
                c = d * nh + h
                hs = slice(h * GDN_DK, (h + 1) * GDN_DK)
                ch = dict(d=d, h=h, hs=hs, rs=rs, o_ref=o_ref, incl=incl, strict=strict)
                ch["bc"] = bcol[:, c:c + 1]
                ch["br"] = brow[c:c + 1, :]
                ch["bl"] = btot[:, c:c + 1]
                ch["beta"] = beta_all[:, BT_COL + c:BT_COL + c + 1]
                ch["q"], ch["k"], ch["v"] = q_all[:, hs], k_all[:, hs], v_all[:, hs]
                slots[slot].append(ch)

    def stage_scores(chains):
        for ch in chains:
            incl = ch["incl"]
            ch["decay"] = jnp.where(incl, jnp.exp(jnp.where(incl, ch["bc"] - ch["br"], 0.0)), 0.0)
            ch["kb"] = ch["k"] * ch["beta"]
            ch["a2"] = _bdot_nt(jnp.concatenate([ch["kb"], ch["q"]], axis=0), ch["k"])

    def stage_split(chains):
        for ch in chains:
            a2 = ch.pop("a2")
            n_mat = -jnp.where(ch["strict"], a2[:CHUNK] * ch["decay"], 0.0)
            ch["attn"] = a2[CHUNK:] * ch["decay"]
            n_diag = jnp.where(same16, n_mat, 0.0)
            ch["n32"] = jnp.where(same32, n_mat, 0.0) - n_diag
            ch["n64"] = jnp.where(same32, 0.0, n_mat)
            ch["p"] = eye + n_diag
            ch["n"] = _bdot(n_diag, n_diag)

    def stage_diag(square):
        def run(chains):
            for ch in chains:
                ch["p"] = ch["p"] + _bdot(ch["p"], ch["n"])
                if square:
                    ch["n"] = _bdot(ch["n"], ch["n"])
        return run

    def stage_merge_a(off):
        def run(chains):
            for ch in chains:
                ch["x"] = _bdot(ch["p"], ch[off])
        return run

    def stage_merge_b(chains):
        for ch in chains:
            ch["p"] = ch["p"] + _bdot(ch.pop("x"), ch["p"])

    def stage_uw(chains):
        for ch in chains:
            ch["e_b"] = jnp.exp(ch["bc"])
            uw = _bdot(ch["p"], jnp.concatenate([ch["v"] * ch["beta"], ch["kb"] * ch["e_b"]], axis=1))
            ch["u"], ch["w"] = uw[:, :GDN_DV], uw[:, GDN_DV:]

    state = {(d, h): s_ref[d, h] for d in range(N_DIR) for h in range(nh)}

    def tail_read(chains):
        for ch in chains:
            s = state[ch["d"], ch["h"]]
            ch["wq"] = _bdot(jnp.concatenate([ch["w"], ch["q"] * ch["e_b"]], axis=0), s)

    def tail_update(chains):
        for ch in chains:
            wq = ch.pop("wq")
            v_new = ch["u"] - wq[:CHUNK]
            ch["o_ref"][0, ch["rs"], ch["hs"]] = (
                wq[CHUNK:] + _bdot(ch["attn"], v_new)).astype(ch["o_ref"].dtype)
            k_dec = ch["k"] * jnp.exp(ch["bl"] - ch["bc"])
            key = ch["d"], ch["h"]
            state[key] = jnp.exp(ch["bl"]) * state[key] + _bdot_tn(k_dec, v_new)

    stages = [stage_scores, stage_split, stage_diag(True), stage_diag(True), stage_diag(False),
              stage_merge_a("n32"), stage_merge_b, stage_merge_a("n64"), stage_merge_b, stage_uw]
    for stage in stages:
        stage(slots[0])
    for slot in range(1, SCAN_CHUNKS):
        for i, stage in enumerate(stages):
            stage(slots[slot])
            if i == 1:
                tail_read(slots[slot - 1])
            elif i == 4:
                tail_update(slots[slot - 1])
    tail_read(slots[-1])
    tail_update(slots[-1])
    for (d, h), s in state.items():
        s_ref[d, h] = s


def gdn_scan(qkv, small, a_log, dt_bias, n_ctx):
    bsz, s, _ = qkv.shape
    rows_per_step = SCAN_CHUNKS * CHUNK
    n_ctx_steps, n_steps = n_ctx // SCAN_CHUNKS, s // rows_per_step
    assert n_ctx % SCAN_CHUNKS == 0 and s % rows_per_step == 0
    lane_row = lambda t: jnp.zeros((1, SMALL_WIDTH), F32).at[0, A_COL:A_COL + N_DIR * GDN_HEADS].set(
        t.astype(F32).reshape(-1))

    def specs(rev):
        def row(t):
            return _rev_chunk(t, n_ctx_steps, n_steps) if rev else t
        return [pl.BlockSpec((1, rows_per_step, GDN_QK), lambda b, t: (b, row(t), 0)),
                pl.BlockSpec((1, rows_per_step, GDN_QK), lambda b, t: (b, row(t), 1)),
                pl.BlockSpec((1, rows_per_step, GDN_V), lambda b, t: (b, row(t), 2)),
                pl.BlockSpec((1, rows_per_step, SMALL_WIDTH), lambda b, t: (b, row(t), 0))]

    const = pl.BlockSpec((1, SMALL_WIDTH), lambda b, t: (0, 0))
    out_f = pl.BlockSpec((1, rows_per_step, GDN_V), lambda b, t: (b, t, 0))
    out_r = pl.BlockSpec((1, rows_per_step, GDN_V),
                         lambda b, t: (b, _rev_chunk(t, n_ctx_steps, n_steps), 0))
    return pl.pallas_call(
        _gdn_kernel,
        grid=(bsz, n_steps),
        in_specs=specs(False) + specs(True) + [const, const],
        out_specs=[out_f, out_r],
        out_shape=[jax.ShapeDtypeStruct((bsz, s, GDN_V), BF16)] * 2,
        scratch_shapes=[pltpu.VMEM((N_DIR, GDN_HEADS, GDN_DK, GDN_DV), F32)],
        compiler_params=_params("parallel", "arbitrary"),
        name="gdn_scan",
    )(qkv, qkv, qkv, small, qkv, qkv, qkv, small, lane_row(a_log), lane_row(dt_bias))


FFN_COL_TILE = D_FF // 2


FFN_EXT_ROWS = ROW_TILE + 2 * GRID_W
FFN_SUB_COLS = 256


def _ffn_gate_kernel(a_ref, up_ref, dn_ref, v_ref, w_ref, o_ref, shift_ref, *, rows):
    g, n = GRID_W, FFN_EXT_ROWS

    @pl.when((pl.program_id(0) == 0) & (pl.program_id(1) == 0))
    def _():
        ri = lax.broadcasted_iota(jnp.int32, (n, n), 0)
        ci = lax.broadcasted_iota(jnp.int32, (n, n), 1)
        for kind in range(2):
            col = (ri & (g - 1)) if kind == 0 else ri - g
            last_col = g - 1 if kind == 0 else ROW_TILE - 1
            shift_ref[kind, 0] = jnp.where((ci == ri - 1) & (col != 0), 1.0, 0.0).astype(BF16)
            shift_ref[kind, 1] = jnp.where((ci == ri + 1) & (col != last_col), 1.0, 0.0).astype(BF16)

    is_ctx, first, last = rows.flags(pl.program_id(0))
    kind = jnp.where(is_ctx, 1, 0)
    halo_zero = jnp.zeros(up_ref.shape, up_ref.dtype)
    up = jnp.where(first, halo_zero, up_ref[...])
    dn = jnp.where(last, halo_zero, dn_ref[...])
    ext_all = jnp.concatenate([up, a_ref[...], dn], axis=0)
    vert = jnp.where(is_ctx, 0.0, 1.0)
    for c0 in range(0, FFN_COL_TILE, FFN_SUB_COLS):
        cs = slice(c0, c0 + FFN_SUB_COLS)
        ext = ext_all[:, cs]
        shifted = (jnp.dot(shift_ref[kind, 0], ext, preferred_element_type=F32),
                   ext.astype(F32),
                   jnp.dot(shift_ref[kind, 1], ext, preferred_element_type=F32))
        acc = None
        for dr in (-1, 0, 1):
            for dc in (-1, 0, 1):
                val = shifted[dc + 1][g + dr * g:g + dr * g + ROW_TILE]
                wt = w_ref[3 * (dr + 1) + dc + 1:3 * (dr + 1) + dc + 2, cs]
                if dr != 0:
                    wt = wt * vert
                acc = val * wt if acc is None else acc + val * wt
        o_ref[:, cs] = (acc * _sigmoid(acc) * v_ref[:, cs].astype(F32)).astype(o_ref.dtype)


def ffn_gate(up, rows, w_conv):
    m = up.shape[0]
    nc = D_FF // FFN_COL_TILE
    gb = ROW_TILE // GRID_W
    n_gblocks = m // GRID_W
    return pl.pallas_call(
        functools.partial(_ffn_gate_kernel, rows=rows),
        grid=(rows.n_tiles, nc),
        in_specs=[pl.BlockSpec((ROW_TILE, FFN_COL_TILE), lambda i, j: (i, j)),
                  pl.BlockSpec((GRID_W, FFN_COL_TILE), lambda i, j: (jnp.maximum(i * gb - 1, 0), j)),
                  pl.BlockSpec((GRID_W, FFN_COL_TILE),
                               lambda i, j: (jnp.minimum((i + 1) * gb, n_gblocks - 1), j)),
                  pl.BlockSpec((ROW_TILE, FFN_COL_TILE), lambda i, j: (i, nc + j)),
                  pl.BlockSpec((9, FFN_COL_TILE), lambda i, j: (0, j))],
        out_specs=pl.BlockSpec((ROW_TILE, FFN_COL_TILE), lambda i, j: (i, j)),
        out_shape=jax.ShapeDtypeStruct((m, D_FF), BF16),
        scratch_shapes=[pltpu.VMEM((2, 2, FFN_EXT_ROWS, FFN_EXT_ROWS), BF16)],
        compiler_params=_params("arbitrary", "arbitrary"), name="ffn_gate",
    )(up, up, up, up, w_conv.reshape(9, D_FF))


BRANCH_WIDTH = GLA_V + GDN_V + POOL_WIDTH
Z_GLA_COL = 2 * GLA_QK + GLA_V
Z_GDN_COL = GDN_QKV_COL + GDN_QKV
POOL_COL = Z_GDN_COL + GDN_V
GATES_COL = POOL_COL + POOL_WIDTH
POOL_HALO = BF16_SUBLANES
assert POOL_HALO >= max(POOL_WINDOWS) // 2


def _headnorm_gate(o, z, nw, n_heads, width, o_ref, col0):
    for h in range(n_heads):
        sl = slice(h * width, (h + 1) * width)
        oh, zh = o[:, sl], z[:, sl]
        inv = lax.rsqrt(jnp.mean(oh * oh, axis=-1, keepdims=True) + EPS)
        o_ref[:, col0 + h * width:col0 + (h + 1) * width] = (
            oh * inv * nw * (zh * _sigmoid(zh))).astype(o_ref.dtype)


def _merge_prep_kernel(gf_ref, gr_ref, df_ref, dr_ref, za_ref, zb_ref, u_ref, uup_ref, udn_ref,
                       gnw_ref, dnw_ref, pw_ref, ps_ref, o_ref, win_ref, *, rows):
    halo, n = POOL_HALO, ROW_TILE + 2 * POOL_HALO

    @pl.when(pl.program_id(0) == 0)
    def _():
        ri = lax.broadcasted_iota(jnp.int32, (ROW_TILE, n), 0)
        ci = lax.broadcasted_iota(jnp.int32, (ROW_TILE, n), 1) - halo
        for gi, win in enumerate(POOL_WINDOWS):
            lo = ri - win // 2
            win_ref[gi] = jnp.where((ci >= lo) & (ci < lo + win), 1.0, 0.0).astype(BF16)

    _, first, last = rows.flags(pl.program_id(0))
    _headnorm_gate(gf_ref[...].astype(F32) + gr_ref[...].astype(F32), za_ref[...].astype(F32),
                   gnw_ref[...], GLA_HEADS, GLA_DV, o_ref, 0)
    _headnorm_gate(df_ref[...].astype(F32) + dr_ref[...].astype(F32), zb_ref[...].astype(F32),
                   dnw_ref[...], GDN_HEADS, GDN_DV, o_ref, GLA_V)
    halo_zero = jnp.zeros(uup_ref.shape, uup_ref.dtype)
    u = u_ref[...]
    ext = jnp.concatenate([jnp.where(first, halo_zero, uup_ref[...]), u,
                           jnp.where(last, halo_zero, udn_ref[...])], axis=0)
    t = lax.broadcasted_iota(jnp.int32, (ROW_TILE, 1), 0)
    for gi, win in enumerate(POOL_WINDOWS):
        sl = slice(gi * POOL_GROUP, (gi + 1) * POOL_GROUP)
        half = win // 2
        win_sum = jnp.dot(win_ref[gi], ext[:, sl], preferred_element_type=F32)
        lo_clip = jnp.where(first, jnp.maximum(half - t, 0), 0)
        hi_clip = jnp.where(last, jnp.maximum(t - half + win - ROW_TILE, 0), 0)
        cnt = (win - lo_clip - hi_clip).astype(F32)
        pg = win_sum / cnt - u[:, sl].astype(F32)
        yp = _bdot(pg, pw_ref[gi]) * ps_ref[:, sl]
        o_ref[:, GLA_V + GDN_V + gi * POOL_GROUP:GLA_V + GDN_V + (gi + 1) * POOL_GROUP] = yp.astype(o_ref.dtype)


def merge_prep(p, o_gla, o_gdn, p_rows, rows, gla_nw, gdn_nw, pool_w, pool_scale):
    hb = ROW_TILE // BF16_SUBLANES
    n_hblocks = p.shape[0] // BF16_SUBLANES
    blk = lambda i: rows.block_of(i, p_rows)
    wide = lambda col: pl.BlockSpec((ROW_TILE, 1024), lambda i: (blk(i), col // 1024))
    const = lambda shape: pl.BlockSpec(shape, lambda i: (0,) * len(shape))
    pc = POOL_COL // POOL_WIDTH
    return pl.pallas_call(
        functools.partial(_merge_prep_kernel, rows=rows),
        grid=(rows.n_tiles,),
        in_specs=[wide(0), wide(0), wide(0), wide(0), wide(Z_GLA_COL), wide(Z_GDN_COL), wide(POOL_COL),
                  pl.BlockSpec((POOL_HALO, POOL_WIDTH), lambda i: (jnp.maximum(blk(i) * hb - 1, 0), pc)),
                  pl.BlockSpec((POOL_HALO, POOL_WIDTH),
                               lambda i: (jnp.minimum((blk(i) + 1) * hb, n_hblocks - 1), pc)),
                  const((1, GLA_DV)), const((1, GDN_DV)),
                  const((POOL_GROUPS, POOL_GROUP, POOL_GROUP)), const((1, POOL_WIDTH))],
        out_specs=pl.BlockSpec((ROW_TILE, BRANCH_WIDTH), lambda i: (i, 0)),
        out_shape=jax.ShapeDtypeStruct((rows.n_tiles * ROW_TILE, BRANCH_WIDTH), BF16),
        scratch_shapes=[pltpu.VMEM((POOL_GROUPS, ROW_TILE, ROW_TILE + 2 * POOL_HALO), BF16)],
        compiler_params=_params("arbitrary"), name="merge_prep",
    )(o_gla[0], o_gla[1], o_gdn[0], o_gdn[1], p, p, p, p, p,
      gla_nw.reshape(1, GLA_DV), gdn_nw.reshape(1, GDN_DV), pool_w.astype(BF16),
      pool_scale.reshape(1, POOL_WIDTH))


def _branch_kernel(ya, yb, yc, ga, gb, gc, wa, wb, wc, o_ref):
    acc = None
    for y_ref, g_ref, w_ref in ((ya, ga, wa), (yb, gb, wb), (yc, gc, wc)):
        term = _sigmoid(g_ref[...].astype(F32)) * jnp.dot(y_ref[...], w_ref[...], preferred_element_type=F32)
        acc = term if acc is None else acc + term
    o_ref[...] = acc.astype(o_ref.dtype)


def branch_merge(ycat, p, p_rows, rows, w_gla, w_gdn, w_pool):
    blk = lambda i: rows.block_of(i, p_rows)
    ysp = lambda k: pl.BlockSpec((ROW_TILE, 1024), lambda i: (i, k))
    gsp = lambda k: pl.BlockSpec((ROW_TILE, D_MODEL), lambda i: (blk(i), GATES_COL // D_MODEL + k))
    wsp = pl.BlockSpec((1024, D_MODEL), lambda i: (0, 0))
    return pl.pallas_call(
        _branch_kernel,
        grid=(rows.n_tiles,),
        in_specs=[ysp(0), ysp(1), ysp(2), gsp(0), gsp(1), gsp(2), wsp, wsp, wsp],
        out_specs=pl.BlockSpec((ROW_TILE, D_MODEL), lambda i: (i, 0)),
        out_shape=jax.ShapeDtypeStruct((rows.n_tiles * ROW_TILE, D_MODEL), BF16),
        compiler_params=_params("parallel"), name="branch_merge",
    )(ycat, ycat, ycat, p, p, p, w_gla, w_gdn, w_pool)


def permute_w_in(w):
    q, k, v, zg, lr, gqkv, ga, gbt, gz, pool, gates = _split_cols(w, IN_SPLITS)
    main = jnp.concatenate([q, k, v, zg, gqkv, gz, pool, gates], axis=1)
    small = jnp.concatenate([lr, ga, gbt, jnp.zeros((w.shape[0], SMALL_WIDTH - SMALL_COLS), w.dtype)], axis=1)
    return main.astype(BF16), small.astype(BF16)


def hybrid_mixer(h, rows, out_rows, bsz, w_main, w_small, gla_w2, gla_b, gla_nw, gdn_cw, gdn_alog,
                 gdn_dtb, gdn_nw, pool_w, pool_scale, w_br_gla, w_br_gdn, w_br_pool, w_out):
    m = h.shape[0]
    s = m // bsz
    p = pmm(h, w_main, BF16)
    small = pmm(h, w_small)
    p3, small3 = p.reshape(bsz, s, MAIN_WIDTH), small.reshape(bsz, s, SMALL_WIDTH)
    n_ctx = (ROW_TILE if rows.has_ctx else 0) // CHUNK
    o_gla = [t.reshape(m, GLA_V) for t in gla_scan(p3, small3, gla_w2.astype(BF16), gla_b, n_ctx)]
    qkv = gdn_prep(p, rows, gdn_cw).reshape(bsz, s, GDN_QKV)
    o_gdn = [t.reshape(m, GDN_V) for t in gdn_scan(qkv, small3, gdn_alog, gdn_dtb, n_ctx)]
    ycat = merge_prep(p, o_gla, o_gdn, rows, out_rows, gla_nw, gdn_nw, pool_w, pool_scale)
    mrg = branch_merge(ycat, p, rows, out_rows, w_br_gla, w_br_gdn, w_br_pool)
    return pmm(mrg, w_out, BF16)


def kernel(x, c, ctx, c_ctx, ada_w, ada_b, norm1_w, norm2_w, w_in, gla_lr_w2, gla_lr_b,
           gla_norm_w, gdn_conv_w, gdn_a_log, gdn_dt_bias, gdn_norm_w, pool_w, pool_scale,
           w_br_gla, w_br_gdn, w_br_pool, w_out, ffn_up, ffn_conv, ffn_down, final_norm_w):
    B, T, _ = x.shape
    assert ctx.shape[1] == ROW_TILE and T % ROW_TILE == 0 and (T // GRID_W) % (ROW_TILE // GRID_W) == 0
    s_all = ROW_TILE + T
    all_rows = Rows(B, True, T // ROW_TILE)
    lat_rows = Rows(B, False, T // ROW_TILE)
    xa = jnp.concatenate([ctx, x], axis=1).reshape(B * s_all, D_MODEL)
    cond = jnp.concatenate([c, c_ctx[None, :], jnp.zeros((16 - B - 1, D_MODEL), F32)], axis=0)
    cond = jax.nn.silu(cond)
    mods = [(pmm(cond, ada_w[l]) + ada_b[l]).reshape(16, 6, D_MODEL) for l in range(DEPTH)]
    _, h = resid_norm(xa, all_rows, all_rows, norm1_w[0], mod=mods[0], shift_idx=0, scale_idx=1)
    x_rows = all_rows
    for l in range(DEPTH):
        last = l == DEPTH - 1
        rows = lat_rows if last else all_rows
        w_main, w_small = permute_w_in(w_in[l])
        y = hybrid_mixer(h, x_rows, rows, B, w_main, w_small, gla_lr_w2[l], gla_lr_b[l], gla_norm_w[l],
                         gdn_conv_w[l], gdn_a_log[l], gdn_dt_bias[l], gdn_norm_w[l],
                         pool_w[l], pool_scale[l], w_br_gla[l].astype(BF16),
                         w_br_gdn[l].astype(BF16), w_br_pool[l].astype(BF16), w_out[l].astype(BF16))
        xa, h2 = resid_norm(xa, x_rows, rows, norm2_w[l], resid=y, gate_mod=mods[l], gate_idx=2,
                            mod=mods[l], shift_idx=3, scale_idx=4)
        x_rows = rows
        up = pmm(h2, ffn_up[l].astype(BF16), BF16)
        gated = ffn_gate(up, rows, ffn_conv[l])
        dn = pmm(gated, ffn_down[l].astype(BF16), BF16)
        if last:
            _, out = resid_norm(xa, x_rows, rows, final_norm_w, resid=dn, gate_mod=mods[l], gate_idx=5,
                                out_dtype=F32)
            return out.reshape(B, T, D_MODEL)
        xa, h = resid_norm(xa, x_rows, rows, norm1_w[l + 1], resid=dn, gate_mod=mods[l], gate_idx=5,
                           mod=mods[l + 1], shift_idx=0, scale_idx=1)
```

```python
import functools

import jax
import jax.numpy as jnp
from jax import lax
from jax.experimental import pallas as pl
from jax.experimental.pallas import tpu as pltpu

D_MODEL = 2048
DEPTH = 4
GRID_W = 64
CHUNK = 64
N_DIR = 2
N_BRANCH = 3
N_MOD = 6
EPS = 1e-6

GLA_HEADS = 4
GLA_DK = 128
GLA_DV = 256
GLA_RANK = 16
GLA_TAU = 16.0

GDN_HEADS = 8
GDN_DK = 128
GDN_DV = 128

POOL_WINDOWS = (2, 4, 8, 16)
POOL_GROUPS = 4
POOL_GROUP = 256

D_FF = 5632

GLA_QK = GLA_HEADS * GLA_DK
GLA_V = GLA_HEADS * GLA_DV
GDN_QK = GDN_HEADS * GDN_DK
GDN_V = GDN_HEADS * GDN_DV
GDN_QKV = 2 * GDN_QK + GDN_V
POOL_WIDTH = POOL_GROUPS * POOL_GROUP
IN_SPLITS = (GLA_QK, GLA_QK, GLA_V, GLA_V, N_DIR * GLA_RANK,
             GDN_QKV, N_DIR * GDN_HEADS, N_DIR * GDN_HEADS, GDN_V,
             POOL_WIDTH, N_BRANCH * D_MODEL)
F32 = jnp.float32
BF16 = jnp.bfloat16
HIGHEST = lax.Precision.HIGHEST

V7X_VMEM_BYTES = 64 * 1024 * 1024
VMEM_LIMIT_BYTES = V7X_VMEM_BYTES * 3 // 4
MM_TILE_BUDGET_BYTES = V7X_VMEM_BYTES * 5 // 8
LANES = 128
BF16_SUBLANES = 16

SMALL_WIDTH = LANES
SMALL_COLS = N_DIR * GLA_RANK + 2 * N_DIR * GDN_HEADS
A_COL = N_DIR * GLA_RANK
BT_COL = A_COL + N_DIR * GDN_HEADS
MAIN_SPLITS = (GLA_QK, GLA_QK, GLA_V, GLA_V, GDN_QKV, GDN_V, POOL_WIDTH, N_BRANCH * D_MODEL)
MAIN_WIDTH = sum(MAIN_SPLITS)
GDN_QKV_COL = 2 * GLA_QK + 2 * GLA_V

ROW_TILE = 256


def _split_cols(t, sizes):
    parts, start = [], 0
    for size in sizes:
        parts.append(t[..., start:start + size])
        start += size
    return parts


def _params(*sem):
    return pltpu.CompilerParams(dimension_semantics=sem, vmem_limit_bytes=VMEM_LIMIT_BYTES)


def _sigmoid(x):
    return 1.0 / (1.0 + jnp.exp(-x))


def _softplus(x):
    return jnp.maximum(x, 0.0) + jnp.log(1.0 + jnp.exp(-jnp.abs(x)))


def _mm_kernel(x_ref, w_ref, o_ref):
    o_ref[...] = jnp.dot(x_ref[...].astype(BF16), w_ref[...].astype(BF16),
                         preferred_element_type=F32).astype(o_ref.dtype)


def _mm_tiles(m, k, n, x_bytes, w_bytes, o_bytes):
    best = None
    for tm in (1024, 512, 256, 128, 64, 32, 16, 8):
        if m % tm:
            continue
        for tn in (2048, 1024, 512, 256, 128):
            if n % tn:
                continue
            need = 2 * (tm * k * x_bytes + k * tn * w_bytes + tm * tn * o_bytes)
            if need > MM_TILE_BUDGET_BYTES:
                continue
            score = tm * tn / (tm + tn)
            if best is None or score > best[0]:
                best = (score, tm, tn)
    if best is None:
        raise ValueError(f"no matmul tiling for {(m, k, n)}")
    return best[1], best[2]


def pmm(x, w, out_dtype=F32):
    m, k = x.shape
    n = w.shape[1]
    tm, tn = _mm_tiles(m, k, n, x.dtype.itemsize, w.dtype.itemsize, jnp.dtype(out_dtype).itemsize)
    return pl.pallas_call(
        _mm_kernel,
        grid=(m // tm, n // tn),
        in_specs=[pl.BlockSpec((tm, k), lambda i, j: (i, 0)),
                  pl.BlockSpec((k, tn), lambda i, j: (0, j))],
        out_specs=pl.BlockSpec((tm, tn), lambda i, j: (i, j)),
        out_shape=jax.ShapeDtypeStruct((m, n), out_dtype),
        compiler_params=_params("parallel", "arbitrary"),
        name="mm",
    )(x, w)


class Rows:
    def __init__(self, batch, has_ctx, lat_tiles):
        self.batch, self.has_ctx, self.lat_tiles = batch, has_ctx, lat_tiles
        self.per_batch = lat_tiles + (1 if has_ctx else 0)
        self.n_tiles = batch * self.per_batch

    def block_of(self, i, other):
        if other.has_ctx == self.has_ctx:
            return i
        assert other.has_ctx and not self.has_ctx
        return (i // self.per_batch) * other.per_batch + 1 + i % self.per_batch

    def mod_row(self, i):
        b = i // self.per_batch
        if not self.has_ctx:
            return b
        return jnp.where(i % self.per_batch == 0, self.batch, b)

    def flags(self, i):
        r = i % self.per_batch
        if not self.has_ctx:
            return False, r == 0, r == self.per_batch - 1
        is_ctx = r == 0
        return is_ctx, is_ctx | (r == 1), is_ctx | (r == self.per_batch - 1)


def _norm_kernel(*refs, has_resid, gate_idx, has_mod, shift_idx, scale_idx, write_x):
    refs = list(refs)
    x_ref = refs.pop(0)
    x = x_ref[...]
    if has_resid:
        y_ref, gmod_ref = refs.pop(0), refs.pop(0)
        x = x + gmod_ref[0, gate_idx:gate_idx + 1, :] * y_ref[...].astype(F32)
    nw_ref = refs.pop(0)
    mod_ref = refs.pop(0) if has_mod else None
    if write_x:
        refs.pop(0)[...] = x
    h_ref = refs.pop(0)
    y = x * lax.rsqrt(jnp.mean(x * x, axis=-1, keepdims=True) + EPS) * nw_ref[...]
    if has_mod:
        y = y * (1.0 + mod_ref[0, scale_idx:scale_idx + 1, :]) + mod_ref[0, shift_idx:shift_idx + 1, :]
    h_ref[...] = y.astype(h_ref.dtype)


def resid_norm(x, x_rows, rows, norm_w, *, resid=None, gate_mod=None, gate_idx=0,
               mod=None, shift_idx=0, scale_idx=1, out_dtype=BF16):
    d = x.shape[-1]
    row = lambda i: (i, 0)
    modspec = pl.BlockSpec((1, N_MOD, d), lambda i: (rows.mod_row(i), 0, 0))
    args = [x]
    in_specs = [pl.BlockSpec((ROW_TILE, d), lambda i: (rows.block_of(i, x_rows), 0))]
    if resid is not None:
        args += [resid, gate_mod]
        in_specs += [pl.BlockSpec((ROW_TILE, d), row), modspec]
    args.append(norm_w.reshape(1, d))
    in_specs.append(pl.BlockSpec((1, d), lambda i: (0, 0)))
    if mod is not None:
        args.append(mod)
        in_specs.append(modspec)
    m = rows.n_tiles * ROW_TILE
    out_shape, out_specs = [], []
    if resid is not None:
        out_shape.append(jax.ShapeDtypeStruct((m, d), F32))
        out_specs.append(pl.BlockSpec((ROW_TILE, d), row))
    out_shape.append(jax.ShapeDtypeStruct((m, d), out_dtype))
    out_specs.append(pl.BlockSpec((ROW_TILE, d), row))
    outs = pl.pallas_call(
        functools.partial(_norm_kernel, has_resid=resid is not None, gate_idx=gate_idx,
                          has_mod=mod is not None, shift_idx=shift_idx, scale_idx=scale_idx,
                          write_x=resid is not None),
        grid=(rows.n_tiles,), in_specs=in_specs, out_specs=out_specs, out_shape=out_shape,
        compiler_params=_params("parallel"), name="resid_norm",
    )(*args)
    return (outs[0], outs[1]) if resid is not None else (None, outs[0])


PREP_SUB_COLS = 256


def _gdn_prep_kernel(x_ref, up_ref, dn_ref, w_ref, o_ref, shift_ref, *, rows):
    @pl.when(pl.program_id(0) == 0)
    def _():
        ri = lax.broadcasted_iota(jnp.int32, (ROW_TILE, ROW_TILE), 0)
        ci = lax.broadcasted_iota(jnp.int32, (ROW_TILE, ROW_TILE), 1)
        shift_ref[0] = jnp.where(ci == ri - 1, 1.0, 0.0).astype(BF16)
        shift_ref[1] = jnp.where(ci == ri + 1, 1.0, 0.0).astype(BF16)

    _, first, last = rows.flags(pl.program_id(0))
    t = lax.broadcasted_iota(jnp.int32, (ROW_TILE, 1), 0)
    for c0 in range(0, GDN_QKV, PREP_SUB_COLS):
        sl = slice(c0, c0 + PREP_SUB_COLS)
        x = x_ref[:, sl]
        above = jnp.where(first, 0.0, up_ref[BF16_SUBLANES - 1:BF16_SUBLANES, sl].astype(F32))
        below = jnp.where(last, 0.0, dn_ref[0:1, sl].astype(F32))
        prev = jnp.where(t == 0, above, jnp.dot(shift_ref[0], x, preferred_element_type=F32))
        nxt = jnp.where(t == ROW_TILE - 1, below, jnp.dot(shift_ref[1], x, preferred_element_type=F32))
        y = prev * w_ref[0:1, sl] + x.astype(F32) * w_ref[1:2, sl] + nxt * w_ref[2:3, sl]
        y = y * _sigmoid(y)
        for h0 in range(0, PREP_SUB_COLS, GDN_DK):
            seg = y[:, h0:h0 + GDN_DK]
            if c0 + h0 < 2 * GDN_QK:
                inv = lax.rsqrt(jnp.sum(seg * seg, axis=-1, keepdims=True) + EPS)
                if c0 + h0 < GDN_QK:
                    inv = inv * GDN_DK ** -0.5
                seg = seg * inv
            o_ref[:, c0 + h0:c0 + h0 + GDN_DK] = seg.astype(o_ref.dtype)


def gdn_prep(p, rows, conv_w):
    m = p.shape[0]
    cb = GDN_QKV_COL // GDN_QKV
    hb = ROW_TILE // BF16_SUBLANES
    n_hblocks = m // BF16_SUBLANES
    return pl.pallas_call(
        functools.partial(_gdn_prep_kernel, rows=rows),
        grid=(rows.n_tiles,),
        in_specs=[pl.BlockSpec((ROW_TILE, GDN_QKV), lambda i: (i, cb)),
                  pl.BlockSpec((BF16_SUBLANES, GDN_QKV), lambda i: (jnp.maximum(i * hb - 1, 0), cb)),
                  pl.BlockSpec((BF16_SUBLANES, GDN_QKV),
                               lambda i: (jnp.minimum((i + 1) * hb, n_hblocks - 1), cb)),
                  pl.BlockSpec((3, GDN_QKV), lambda i: (0, 0))],
        out_specs=pl.BlockSpec((ROW_TILE, GDN_QKV), lambda i: (i, 0)),
        out_shape=jax.ShapeDtypeStruct((m, GDN_QKV), BF16),
        scratch_shapes=[pltpu.VMEM((2, ROW_TILE, ROW_TILE), BF16)],
        compiler_params=_params("arbitrary"), name="gdn_prep",
    )(p, p, p, conv_w)


def _bdot(a, b):
    return jnp.dot(a.astype(BF16), b.astype(BF16), preferred_element_type=F32)


def _bdot_nt(a, b):
    return lax.dot_general(a.astype(BF16), b.astype(BF16), (((1,), (1,)), ((), ())),
                           preferred_element_type=F32)


def _bdot_tn(a, b):
    return lax.dot_general(a.astype(BF16), b.astype(BF16), (((0,), (0,)), ((), ())),
                           preferred_element_type=F32)


def _rev_chunk(t, n_ctx, n_all):
    return jnp.where(t < n_ctx, n_ctx - 1 - t, n_all + n_ctx - 1 - t)


def _chunk_masks(d):
    ri = lax.broadcasted_iota(jnp.int32, (CHUNK, CHUNK), 0)
    ci = lax.broadcasted_iota(jnp.int32, (CHUNK, CHUNK), 1)
    lower, upper = ci <= ri, ci >= ri
    if d == 0:
        return lower, ci < ri, lower.astype(F32), upper.astype(F32)
    return upper, ci > ri, upper.astype(F32), lower.astype(F32)


SCAN_CHUNKS = 2


def _cumsum_dot(tri, x):
    tb = tri.astype(BF16)
    hi = x.astype(BF16)
    rest = x - hi.astype(F32)
    mid = rest.astype(BF16)
    lo = (rest - mid.astype(F32)).astype(BF16)
    return (jnp.dot(tb, hi, preferred_element_type=F32) + jnp.dot(tb, mid, preferred_element_type=F32)
            + jnp.dot(tb, lo, preferred_element_type=F32))


def _gla_kernel(qf, kf, vf, sf, qr, kr, vr, sr, w2_ref, b_ref, of, orr, st_ref):
    @pl.when(pl.program_id(1) == 0)
    def _():
        st_ref[...] = jnp.zeros_like(st_ref)

    dirs = ((qf, kf, vf, sf, of), (qr, kr, vr, sr, orr))
    slots = [[] for _ in range(SCAN_CHUNKS)]
    row_slice = lambda d, slot: slice((slot if d == 0 else SCAN_CHUNKS - 1 - slot) * CHUNK,
                                      (slot if d == 0 else SCAN_CHUNKS - 1 - slot) * CHUNK + CHUNK)
    blocks = [(d, slot) for d in range(N_DIR) for slot in range(SCAN_CHUNKS)]
    zs = {(d, slot): _bdot(dirs[d][3][0, row_slice(d, slot), d * GLA_RANK:(d + 1) * GLA_RANK], w2_ref[d])
          + b_ref[d] for d, slot in blocks}
    bcums = {(d, slot): _cumsum_dot(_chunk_masks(d)[2], -_softplus(-zs[d, slot]) / GLA_TAU)
             for d, slot in blocks}
    for d, (q_ref, k_ref, v_ref, s_ref, o_ref) in enumerate(dirs):
        incl = _chunk_masks(d)[0]
        last = CHUNK - 1 if d == 0 else 0
        for slot in range(SCAN_CHUNKS):
            rs = row_slice(d, slot)
            bcum = bcums[d, slot]
            q_all, k_all, v_all = q_ref[0, rs, :].astype(F32), k_ref[0, rs, :].astype(F32), v_ref[0, rs, :]
            for h in range(GLA_HEADS):
                ks = slice(h * GLA_DK, (h + 1) * GLA_DK)
                vs = slice(h * GLA_DV, (h + 1) * GLA_DV)
                b = bcum[:, ks]
                bl = bcum[last:last + 1, ks]
                k = k_all[:, ks]
                ch = dict(d=d, h=h, vs=vs, rs=rs, o_ref=o_ref, bl=bl, v=v_all[:, vs], incl=incl)
                ch["q_dec"] = q_all[:, ks] * (GLA_DK ** -0.5) * jnp.exp(b)
                ch["k_neg"] = k * jnp.exp(-b)
                ch["k_dec"] = k * jnp.exp(bl - b)
                slots[slot].append(ch)
    chains = [ch for slot in slots for ch in slot]
    for ch in chains:
        ch["scores"] = jnp.where(ch["incl"], _bdot_nt(ch["q_dec"], ch["k_neg"]), 0.0)
    for ch in chains:
        ch["intra"] = _bdot(ch["scores"], ch["v"])
        ch["kv"] = _bdot_tn(ch["v"], ch["k_dec"])
    state = {(d, h): st_ref[d, h] for d in range(N_DIR) for h in range(GLA_HEADS)}
    for slot in slots:
        for ch in slot:
            key = ch["d"], ch["h"]
            out = ch["intra"] + _bdot_nt(ch["q_dec"], state[key])
            ch["o_ref"][0, ch["rs"], ch["vs"]] = out.astype(ch["o_ref"].dtype)
            state[key] = jnp.exp(ch["bl"]) * state[key] + ch["kv"]
    for (d, h), st in state.items():
        st_ref[d, h] = st


def gla_scan(p, small, w2, bias, n_ctx):
    bsz, s, _ = p.shape
    rows_per_step = SCAN_CHUNKS * CHUNK
    n_ctx_steps, n_steps = n_ctx // SCAN_CHUNKS, s // rows_per_step
    assert n_ctx % SCAN_CHUNKS == 0 and s % rows_per_step == 0

    def specs(rev):
        def row(t):
            return _rev_chunk(t, n_ctx_steps, n_steps) if rev else t
        return [pl.BlockSpec((1, rows_per_step, GLA_QK), lambda b, t: (b, row(t), 0)),
                pl.BlockSpec((1, rows_per_step, GLA_QK), lambda b, t: (b, row(t), 1)),
                pl.BlockSpec((1, rows_per_step, GLA_V), lambda b, t: (b, row(t), 2 * GLA_QK // GLA_V)),
                pl.BlockSpec((1, rows_per_step, SMALL_WIDTH), lambda b, t: (b, row(t), 0))]

    out_f = pl.BlockSpec((1, rows_per_step, GLA_V), lambda b, t: (b, t, 0))
    out_r = pl.BlockSpec((1, rows_per_step, GLA_V),
                         lambda b, t: (b, _rev_chunk(t, n_ctx_steps, n_steps), 0))
    return pl.pallas_call(
        _gla_kernel,
        grid=(bsz, n_steps),
        in_specs=specs(False) + specs(True) + [
            pl.BlockSpec((N_DIR, GLA_RANK, GLA_QK), lambda b, t: (0, 0, 0)),
            pl.BlockSpec((N_DIR, 1, GLA_QK), lambda b, t: (0, 0, 0))],
        out_specs=[out_f, out_r],
        out_shape=[jax.ShapeDtypeStruct((bsz, s, GLA_V), BF16)] * 2,
        scratch_shapes=[pltpu.VMEM((N_DIR, GLA_HEADS, GLA_DV, GLA_DK), F32)],
        compiler_params=_params("parallel", "arbitrary"),
        name="gla_scan",
    )(p, p, p, small, p, p, p, small, w2, bias.reshape(N_DIR, 1, GLA_QK))


def _gdn_kernel(qf, kf, vf, sf, qr, kr, vr, sr, alog_ref, dtb_ref, of, orr, s_ref):
    @pl.when(pl.program_id(1) == 0)
    def _():
        s_ref[...] = jnp.zeros_like(s_ref)

    ri = lax.broadcasted_iota(jnp.int32, (CHUNK, CHUNK), 0)
    ci = lax.broadcasted_iota(jnp.int32, (CHUNK, CHUNK), 1)
    eye = (ri == ci).astype(F32)
    same16 = (ri >> 4) == (ci >> 4)
    same32 = (ri >> 5) == (ci >> 5)
    nh = GDN_HEADS
    slots = [[] for _ in range(SCAN_CHUNKS)]
    for d, (q_ref, k_ref, v_ref, sm_ref, o_ref) in enumerate(
            ((qf, kf, vf, sf, of), (qr, kr, vr, sr, orr))):
        incl, strict, tri, tri_t = _chunk_masks(d)
        for slot in range(SCAN_CHUNKS):
            sub = slot if d == 0 else SCAN_CHUNKS - 1 - slot
            rs = slice(sub * CHUNK, (sub + 1) * CHUNK)
            sm = sm_ref[0, rs, :]
            g_all = -jnp.exp(alog_ref[...]) * _softplus(sm + dtb_ref[...])
            beta_all = _sigmoid(sm)
            gcol = g_all[:, A_COL:A_COL + N_DIR * nh]
            grow = g_all.T[A_COL:A_COL + N_DIR * nh, :]
            bcol = jnp.dot(tri, gcol, precision=HIGHEST, preferred_element_type=F32)
            brow = jnp.dot(grow, tri_t, precision=HIGHEST, preferred_element_type=F32)
            last = CHUNK - 1 if d == 0 else 0
            btot = bcol[last:last + 1, :]
            q_all = q_ref[0, rs, :].astype(F32)
            k_all = k_ref[0, rs, :].astype(F32)
            v_all = v_ref[0, rs, :].astype(F32)
            for h in range(nh):
                c = d * nh + h
                hs = slice(h * GDN_DK, (h + 1) * GDN_DK)
                ch = dict(d=d, h=h, hs=hs, rs=rs, o_ref=o_ref, incl=incl, strict=strict)
                ch["bc"] = bcol[:, c:c + 1]
                ch["br"] = brow[c:c + 1, :]
                ch["bl"] = btot[:, c:c + 1]
                ch["beta"] = beta_all[:, BT_COL + c:BT_COL + c + 1]
                ch["q"], ch["k"], ch["v"] = q_all[:, hs], k_all[:, hs], v_all[:, hs]
                slots[slot].append(ch)

    def stage_scores(chains):
        for ch in chains:
            incl = ch["incl"]
            ch["decay"] = jnp.where(incl, jnp.exp(jnp.where(incl, ch["bc"] - ch["br"], 0.0)), 0.0)
            ch["kb"] = ch["k"] * ch["beta"]
            ch["a2"] = _bdot_nt(jnp.concatenate([ch["kb"], ch["q"]], axis=0), ch["k"])

    def stage_split(chains):
        for ch in chains:
            a2 = ch.pop("a2")
            n_mat = -jnp.where(ch["strict"], a2[:CHUNK] * ch["decay"], 0.0)
            ch["attn"] = a2[CHUNK:] * ch["decay"]
            n_diag = jnp.where(same16, n_mat, 0.0)
            ch["n32"] = jnp.where(same32, n_mat, 0.0) - n_diag
            ch["n64"] = jnp.where(same32, 0.0, n_mat)
            ch["p"] = eye + n_diag
            ch["n"] = _bdot(n_diag, n_diag)

    def stage_diag(square):
        def run(chains):
            for ch in chains:
                ch["p"] = ch["p"] + _bdot(ch["p"], ch["n"])
                if square:
                    ch["n"] = _bdot(ch["n"], ch["n"])
        return run

    def stage_merge_a(off):
        def run(chains):
            for ch in chains:
                ch["x"] = _bdot(ch["p"], ch[off])
        return run

    def stage_merge_b(chains):
        for ch in chains:
            ch["p"] = ch["p"] + _bdot(ch.pop("x"), ch["p"])

    def stage_uw(chains):
        for ch in chains:
            ch["e_b"] = jnp.exp(ch["bc"])
            uw = _bdot(ch["p"], jnp.concatenate([ch["v"] * ch["beta"], ch["kb"] * ch["e_b"]], axis=1))
            ch["u"], ch["w"] = uw[:, :GDN_DV], uw[:, GDN_DV:]

    state = {(d, h): s_ref[d, h] for d in range(N_DIR) for h in range(nh)}

    def tail_read(chains):
        for ch in chains:
            s = state[ch["d"], ch["h"]]
            ch["wq"] = _bdot(jnp.concatenate([ch["w"], ch["q"] * ch["e_b"]], axis=0), s)

    def tail_update(chains):
        for ch in chains:
            wq = ch.pop("wq")
            v_new = ch["u"] - wq[:CHUNK]
            ch["o_ref"][0, ch["rs"], ch["hs"]] = (
                wq[CHUNK:] + _bdot(ch["attn"], v_new)).astype(ch["o_ref"].dtype)
            k_dec = ch["k"] * jnp.exp(ch["bl"] - ch["bc"])
            key = ch["d"], ch["h"]
            state[key] = jnp.exp(ch["bl"]) * state[key] + _bdot_tn(k_dec, v_new)

    stages = [stage_scores, stage_split, stage_diag(True), stage_diag(True), stage_diag(False),
              stage_merge_a("n32"), stage_merge_b, stage_merge_a("n64"), stage_merge_b, stage_uw]
    for stage in stages:
        stage(slots[0])
    for slot in range(1, SCAN_CHUNKS):
        for i, stage in enumerate(stages):
            stage(slots[slot])
            if i == 1:
                tail_read(slots[slot - 1])
            elif i == 4:
                tail_update(slots[slot - 1])
    tail_read(slots[-1])
    tail_update(slots[-1])
    for (d, h), s in state.items():
        s_ref[d, h] = s


def gdn_scan(qkv, small, a_log, dt_bias, n_ctx):
    bsz, s, _ = qkv.shape
    rows_per_step = SCAN_CHUNKS * CHUNK
    n_ctx_steps, n_steps = n_ctx // SCAN_CHUNKS, s // rows_per_step
    assert n_ctx % SCAN_CHUNKS == 0 and s % rows_per_step == 0
    lane_row = lambda t: jnp.zeros((1, SMALL_WIDTH), F32).at[0, A_COL:A_COL + N_DIR * GDN_HEADS].set(
        t.astype(F32).reshape(-1))

    def specs(rev):
        def row(t):
            return _rev_chunk(t, n_ctx_steps, n_steps) if rev else t
        return [pl.BlockSpec((1, rows_per_step, GDN_QK), lambda b, t: (b, row(t), 0)),
                pl.BlockSpec((1, rows_per_step, GDN_QK), lambda b, t: (b, row(t), 1)),
                pl.BlockSpec((1, rows_per_step, GDN_V), lambda b, t: (b, row(t), 2)),
                pl.BlockSpec((1, rows_per_step, SMALL_WIDTH), lambda b, t: (b, row(t), 0))]

    const = pl.BlockSpec((1, SMALL_WIDTH), lambda b, t: (0, 0))
    out_f = pl.BlockSpec((1, rows_per_step, GDN_V), lambda b, t: (b, t, 0))
    out_r = pl.BlockSpec((1, rows_per_step, GDN_V),
                         lambda b, t: (b, _rev_chunk(t, n_ctx_steps, n_steps), 0))
    return pl.pallas_call(
        _gdn_kernel,
        grid=(bsz, n_steps),
        in_specs=specs(False) + specs(True) + [const, const],
        out_specs=[out_f, out_r],
        out_shape=[jax.ShapeDtypeStruct((bsz, s, GDN_V), BF16)] * 2,
        scratch_shapes=[pltpu.VMEM((N_DIR, GDN_HEADS, GDN_DK, GDN_DV), F32)],
        compiler_params=_params("parallel", "arbitrary"),
        name="gdn_scan",
    )(qkv, qkv, qkv, small, qkv, qkv, qkv, small, lane_row(a_log), lane_row(dt_bias))


FFN_COL_TILE = D_FF // 2


FFN_EXT_ROWS = ROW_TILE + 2 * GRID_W
FFN_SUB_COLS = 256


def _ffn_gate_kernel(a_ref, up_ref, dn_ref, v_ref, w_ref, o_ref, shift_ref, *, rows):
    g, n = GRID_W, FFN_EXT_ROWS

    @pl.when((pl.program_id(0) == 0) & (pl.program_id(1) == 0))
    def _():
        ri = lax.broadcasted_iota(jnp.int32, (n, n), 0)
        ci = lax.broadcasted_iota(jnp.int32, (n, n), 1)
        for kind in range(2):
            col = (ri & (g - 1)) if kind == 0 else ri - g
            last_col = g - 1 if kind == 0 else ROW_TILE - 1
            shift_ref[kind, 0] = jnp.where((ci == ri - 1) & (col != 0), 1.0, 0.0).astype(BF16)
            shift_ref[kind, 1] = jnp.where((ci == ri + 1) & (col != last_col), 1.0, 0.0).astype(BF16)

    is_ctx, first, last = rows.flags(pl.program_id(0))
    kind = jnp.where(is_ctx, 1, 0)
    halo_zero = jnp.zeros(up_ref.shape, up_ref.dtype)
    up = jnp.where(first, halo_zero, up_ref[...])
    dn = jnp.where(last, halo_zero, dn_ref[...])
    ext_all = jnp.concatenate([up, a_ref[...], dn], axis=0)
    vert = jnp.where(is_ctx, 0.0, 1.0)
    for c0 in range(0, FFN_COL_TILE, FFN_SUB_COLS):
        cs = slice(c0, c0 + FFN_SUB_COLS)
        ext = ext_all[:, cs]
        shifted = (jnp.dot(shift_ref[kind, 0], ext, preferred_element_type=F32),
                   ext.astype(F32),
                   jnp.dot(shift_ref[kind, 1], ext, preferred_element_type=F32))
        acc = None
        for dr in (-1, 0, 1):
            for dc in (-1, 0, 1):
                val = shifted[dc + 1][g + dr * g:g + dr * g + ROW_TILE]
                wt = w_ref[3 * (dr + 1) + dc + 1:3 * (dr + 1) + dc + 2, cs]
                if dr != 0:
                    wt = wt * vert
                acc = val * wt if acc is None else acc + val * wt
        o_ref[:, cs] = (acc * _sigmoid(acc) * v_ref[:, cs].astype(F32)).astype(o_ref.dtype)


def ffn_gate(up, rows, w_conv):
    m = up.shape[0]
    nc = D_FF // FFN_COL_TILE
    gb = ROW_TILE // GRID_W
    n_gblocks = m // GRID_W
    return pl.pallas_call(
        functools.partial(_ffn_gate_kernel, rows=rows),
        grid=(rows.n_tiles, nc),
        in_specs=[pl.BlockSpec((ROW_TILE, FFN_COL_TILE), lambda i, j: (i, j)),
                  pl.BlockSpec((GRID_W, FFN_COL_TILE), lambda i, j: (jnp.maximum(i * gb - 1, 0), j)),
                  pl.BlockSpec((GRID_W, FFN_COL_TILE),
                               lambda i, j: (jnp.minimum((i + 1) * gb, n_gblocks - 1), j)),
                  pl.BlockSpec((ROW_TILE, FFN_COL_TILE), lambda i, j: (i, nc + j)),
                  pl.BlockSpec((9, FFN_COL_TILE), lambda i, j: (0, j))],
        out_specs=pl.BlockSpec((ROW_TILE, FFN_COL_TILE), lambda i, j: (i, j)),
        out_shape=jax.ShapeDtypeStruct((m, D_FF), BF16),
        scratch_shapes=[pltpu.VMEM((2, 2, FFN_EXT_ROWS, FFN_EXT_ROWS), BF16)],
        compiler_params=_params("arbitrary", "arbitrary"), name="ffn_gate",
    )(up, up, up, up, w_conv.reshape(9, D_FF))


BRANCH_COLS = GLA_V
assert GDN_V == BRANCH_COLS and POOL_WIDTH == BRANCH_COLS
BRANCH_WIDTH = N_BRANCH * BRANCH_COLS
Z_GLA_COL = 2 * GLA_QK + GLA_V
Z_GDN_COL = GDN_QKV_COL + GDN_QKV
POOL_COL = Z_GDN_COL + GDN_V
GATES_COL = POOL_COL + POOL_WIDTH
POOL_HALO = BF16_SUBLANES
assert POOL_HALO >= max(POOL_WINDOWS) // 2


def _headnorm_gate(o, z, nw, n_heads, width, o_ref, col0):
    for h in range(n_heads):
        sl = slice(h * width, (h + 1) * width)
        oh, zh = o[:, sl], z[:, sl]
        inv = lax.rsqrt(jnp.mean(oh * oh, axis=-1, keepdims=True) + EPS)
        o_ref[:, col0 + h * width:col0 + (h + 1) * width] = (
            oh * inv * nw * (zh * _sigmoid(zh))).astype(o_ref.dtype)


def _merge_prep_kernel(gf_ref, gr_ref, df_ref, dr_ref, za_ref, zb_ref, u_ref, uup_ref, udn_ref,
                       gnw_ref, dnw_ref, pw_ref, ps_ref, o_ref, win_ref, *, rows):
    halo, n = POOL_HALO, ROW_TILE + 2 * POOL_HALO

    @pl.when(pl.program_id(0) == 0)
    def _():
        ri = lax.broadcasted_iota(jnp.int32, (ROW_TILE, n), 0)
        ci = lax.broadcasted_iota(jnp.int32, (ROW_TILE, n), 1) - halo
        for gi, win in enumerate(POOL_WINDOWS):
            lo = ri - win // 2
            win_ref[gi] = jnp.where((ci >= lo) & (ci < lo + win), 1.0, 0.0).astype(BF16)

    _, first, last = rows.flags(pl.program_id(0))
    _headnorm_gate(gf_ref[...].astype(F32) + gr_ref[...].astype(F32), za_ref[...].astype(F32),
                   gnw_ref[...], GLA_HEADS, GLA_DV, o_ref, 0)
    _headnorm_gate(df_ref[...].astype(F32) + dr_ref[...].astype(F32), zb_ref[...].astype(F32),
                   dnw_ref[...], GDN_HEADS, GDN_DV, o_ref, GLA_V)
    halo_zero = jnp.zeros(uup_ref.shape, uup_ref.dtype)
    u = u_ref[...]
    ext = jnp.concatenate([jnp.where(first, halo_zero, uup_ref[...]), u,
                           jnp.where(last, halo_zero, udn_ref[...])], axis=0)
    t = lax.broadcasted_iota(jnp.int32, (ROW_TILE, 1), 0)
    for gi, win in enumerate(POOL_WINDOWS):
        sl = slice(gi * POOL_GROUP, (gi + 1) * POOL_GROUP)
        half = win // 2
        win_sum = jnp.dot(win_ref[gi], ext[:, sl], preferred_element_type=F32)
        lo_clip = jnp.where(first, jnp.maximum(half - t, 0), 0)
        hi_clip = jnp.where(last, jnp.maximum(t - half + win - ROW_TILE, 0), 0)
        cnt = (win - lo_clip - hi_clip).astype(F32)
        pg = win_sum / cnt - u[:, sl].astype(F32)
        yp = _bdot(pg, pw_ref[gi]) * ps_ref[:, sl]
        o_ref[:, GLA_V + GDN_V + gi * POOL_GROUP:GLA_V + GDN_V + (gi + 1) * POOL_GROUP] = yp.astype(o_ref.dtype)


def merge_prep(p, o_gla, o_gdn, p_rows, rows, gla_nw, gdn_nw, pool_w, pool_scale):
    hb = ROW_TILE // BF16_SUBLANES
    n_hblocks = p.shape[0] // BF16_SUBLANES
    blk = lambda i: rows.block_of(i, p_rows)
    wide = lambda col: pl.BlockSpec((ROW_TILE, BRANCH_COLS), lambda i: (blk(i), col // BRANCH_COLS))
    const = lambda shape: pl.BlockSpec(shape, lambda i: (0,) * len(shape))
    pc = POOL_COL // POOL_WIDTH
    return pl.pallas_call(
        functools.partial(_merge_prep_kernel, rows=rows),
        grid=(rows.n_tiles,),
        in_specs=[wide(0), wide(0), wide(0), wide(0), wide(Z_GLA_COL), wide(Z_GDN_COL), wide(POOL_COL),
                  pl.BlockSpec((POOL_HALO, POOL_WIDTH), lambda i: (jnp.maximum(blk(i) * hb - 1, 0), pc)),
                  pl.BlockSpec((POOL_HALO, POOL_WIDTH),
                               lambda i: (jnp.minimum((blk(i) + 1) * hb, n_hblocks - 1), pc)),
                  const((1, GLA_DV)), const((1, GDN_DV)),
                  const((POOL_GROUPS, POOL_GROUP, POOL_GROUP)), const((1, POOL_WIDTH))],
        out_specs=pl.BlockSpec((ROW_TILE, BRANCH_WIDTH), lambda i: (i, 0)),
        out_shape=jax.ShapeDtypeStruct((rows.n_tiles * ROW_TILE, BRANCH_WIDTH), BF16),
        scratch_shapes=[pltpu.VMEM((POOL_GROUPS, ROW_TILE, ROW_TILE + 2 * POOL_HALO), BF16)],
        compiler_params=_params("arbitrary"), name="merge_prep",
    )(o_gla[0], o_gla[1], o_gdn[0], o_gdn[1], p, p, p, p, p,
      gla_nw.reshape(1, GLA_DV), gdn_nw.reshape(1, GDN_DV), pool_w.astype(BF16),
      pool_scale.reshape(1, POOL_WIDTH))


def _branch_kernel(ya, yb, yc, ga, gb, gc, wa, wb, wc, o_ref):
    acc = None
    for y_ref, g_ref, w_ref in ((ya, ga, wa), (yb, gb, wb), (yc, gc, wc)):
        term = _sigmoid(g_ref[...].astype(F32)) * jnp.dot(y_ref[...], w_ref[...], preferred_element_type=F32)
        acc = term if acc is None else acc + term
    o_ref[...] = acc.astype(o_ref.dtype)


def branch_merge(ycat, p, p_rows, rows, w_gla, w_gdn, w_pool):
    blk = lambda i: rows.block_of(i, p_rows)
    ysp = lambda k: pl.BlockSpec((ROW_TILE, BRANCH_COLS), lambda i: (i, k))
    gsp = lambda k: pl.BlockSpec((ROW_TILE, D_MODEL), lambda i: (blk(i), GATES_COL // D_MODEL + k))
    wsp = pl.BlockSpec((BRANCH_COLS, D_MODEL), lambda i: (0, 0))
    return pl.pallas_call(
        _branch_kernel,
        grid=(rows.n_tiles,),
        in_specs=[ysp(0), ysp(1), ysp(2), gsp(0), gsp(1), gsp(2), wsp, wsp, wsp],
        out_specs=pl.BlockSpec((ROW_TILE, D_MODEL), lambda i: (i, 0)),
        out_shape=jax.ShapeDtypeStruct((rows.n_tiles * ROW_TILE, D_MODEL), BF16),
        compiler_params=_params("parallel"), name="branch_merge",
    )(ycat, ycat, ycat, p, p, p, w_gla, w_gdn, w_pool)


def permute_w_in(w):
    q, k, v, zg, lr, gqkv, ga, gbt, gz, pool, gates = _split_cols(w, IN_SPLITS)
    main = jnp.concatenate([q, k, v, zg, gqkv, gz, pool, gates], axis=1)
    small = jnp.concatenate([lr, ga, gbt, jnp.zeros((w.shape[0], SMALL_WIDTH - SMALL_COLS), w.dtype)], axis=1)
    return main.astype(BF16), small.astype(BF16)


def hybrid_mixer(h, rows, out_rows, bsz, w_main, w_small, gla_w2, gla_b, gla_nw, gdn_cw, gdn_alog,
                 gdn_dtb, gdn_nw, pool_w, pool_scale, w_br_gla, w_br_gdn, w_br_pool, w_out):
    m = h.shape[0]
    s = m // bsz
    p = pmm(h, w_main, BF16)
    small = pmm(h, w_small)
    p3, small3 = p.reshape(bsz, s, MAIN_WIDTH), small.reshape(bsz, s, SMALL_WIDTH)
    n_ctx = (ROW_TILE if rows.has_ctx else 0) // CHUNK
    o_gla = [t.reshape(m, GLA_V) for t in gla_scan(p3, small3, gla_w2.astype(BF16), gla_b, n_ctx)]
    qkv = gdn_prep(p, rows, gdn_cw).reshape(bsz, s, GDN_QKV)
    o_gdn = [t.reshape(m, GDN_V) for t in gdn_scan(qkv, small3, gdn_alog, gdn_dtb, n_ctx)]
    ycat = merge_prep(p, o_gla, o_gdn, rows, out_rows, gla_nw, gdn_nw, pool_w, pool_scale)
    mrg = branch_merge(ycat, p, rows, out_rows, w_br_gla, w_br_gdn, w_br_pool)
    return pmm(mrg, w_out, BF16)


def kernel(x, c, ctx, c_ctx, ada_w, ada_b, norm1_w, norm2_w, w_in, gla_lr_w2, gla_lr_b,
           gla_norm_w, gdn_conv_w, gdn_a_log, gdn_dt_bias, gdn_norm_w, pool_w, pool_scale,
           w_br_gla, w_br_gdn, w_br_pool, w_out, ffn_up, ffn_conv, ffn_down, final_norm_w):
    B, T, _ = x.shape
    assert ctx.shape[1] == ROW_TILE and T % ROW_TILE == 0 and (T // GRID_W) % (ROW_TILE // GRID_W) == 0
    s_all = ROW_TILE + T
    all_rows = Rows(B, True, T // ROW_TILE)
    lat_rows = Rows(B, False, T // ROW_TILE)
    xa = jnp.concatenate([ctx, x], axis=1).reshape(B * s_all, D_MODEL)
    n_cond = -(-(B + 1) // BF16_SUBLANES) * BF16_SUBLANES
    cond = jnp.concatenate([c, c_ctx[None, :], jnp.zeros((n_cond - B - 1, D_MODEL), F32)], axis=0)
    cond = jax.nn.silu(cond)
    mods = [(pmm(cond, ada_w[l]) + ada_b[l]).reshape(n_cond, N_MOD, D_MODEL) for l in range(DEPTH)]
    _, h = resid_norm(xa, all_rows, all_rows, norm1_w[0], mod=mods[0], shift_idx=0, scale_idx=1)
    x_rows = all_rows
    for l in range(DEPTH):
        last = l == DEPTH - 1
        rows = lat_rows if last else all_rows
        w_main, w_small = permute_w_in(w_in[l])
        y = hybrid_mixer(h, x_rows, rows, B, w_main, w_small, gla_lr_w2[l], gla_lr_b[l], gla_norm_w[l],
                         gdn_conv_w[l], gdn_a_log[l], gdn_dt_bias[l], gdn_norm_w[l],
                         pool_w[l], pool_scale[l], w_br_gla[l].astype(BF16),
                         w_br_gdn[l].astype(BF16), w_br_pool[l].astype(BF16), w_out[l].astype(BF16))
        xa, h2 = resid_norm(xa, x_rows, rows, norm2_w[l], resid=y, gate_mod=mods[l], gate_idx=2,
                            mod=mods[l], shift_idx=3, scale_idx=4)
        x_rows = rows
        up = pmm(h2, ffn_up[l].astype(BF16), BF16)
        gated = ffn_gate(up, rows, ffn_conv[l])
        dn = pmm(gated, ffn_down[l].astype(BF16), BF16)
        if last:
            _, out = resid_norm(xa, x_rows, rows, final_norm_w, resid=dn, gate_mod=mods[l], gate_idx=5,
                                out_dtype=F32)
            return out.reshape(B, T, D_MODEL)
        xa, h = resid_norm(xa, x_rows, rows, norm1_w[l + 1], resid=dn, gate_mod=mods[l], gate_idx=5,
                           mod=mods[l + 1], shift_idx=0, scale_idx=1)
```

```python
import functools

import jax
import jax.numpy as jnp
from jax import lax
from jax.experimental import pallas as pl
from jax.experimental.pallas import tpu as pltpu

D_MODEL = 2048
DEPTH = 4
GRID_W = 64
CHUNK = 64
N_DIR = 2
N_BRANCH = 3
N_MOD = 6
EPS = 1e-6

GLA_HEADS = 4
GLA_DK = 128
GLA_DV = 256
GLA_RANK = 16
GLA_TAU = 16.0

GDN_HEADS = 8
GDN_DK = 128
GDN_DV = 128

POOL_WINDOWS = (2, 4, 8, 16)
POOL_GROUPS = 4
POOL_GROUP = 256

D_FF = 5632

GLA_QK = GLA_HEADS * GLA_DK
GLA_V = GLA_HEADS * GLA_DV
GDN_QK = GDN_HEADS * GDN_DK
GDN_V = GDN_HEADS * GDN_DV
GDN_QKV = 2 * GDN_QK + GDN_V
POOL_WIDTH = POOL_GROUPS * POOL_GROUP
IN_SPLITS = (GLA_QK, GLA_QK, GLA_V, GLA_V, N_DIR * GLA_RANK,
             GDN_QKV, N_DIR * GDN_HEADS, N_DIR * GDN_HEADS, GDN_V,
             POOL_WIDTH, N_BRANCH * D_MODEL)
F32 = jnp.float32
BF16 = jnp.bfloat16
HIGHEST = lax.Precision.HIGHEST

V7X_VMEM_BYTES = 64 * 1024 * 1024
VMEM_LIMIT_BYTES = V7X_VMEM_BYTES * 3 // 4
MM_TILE_BUDGET_BYTES = V7X_VMEM_BYTES * 5 // 8
LANES = 128
BF16_SUBLANES = 16

SMALL_WIDTH = LANES
SMALL_COLS = N_DIR * GLA_RANK + 2 * N_DIR * GDN_HEADS
A_COL = N_DIR * GLA_RANK
BT_COL = A_COL + N_DIR * GDN_HEADS
MAIN_SPLITS = (GLA_QK, GLA_QK, GLA_V, GLA_V, GDN_QKV, GDN_V, POOL_WIDTH, N_BRANCH * D_MODEL)
MAIN_WIDTH = sum(MAIN_SPLITS)
GDN_QKV_COL = 2 * GLA_QK + 2 * GLA_V

ROW_TILE = 256


def _split_cols(t, sizes):
    parts, start = [], 0
    for size in sizes:
        parts.append(t[..., start:start + size])
        start += size
    return parts


def _params(*sem):
    return pltpu.CompilerParams(dimension_semantics=sem, vmem_limit_bytes=VMEM_LIMIT_BYTES)


def _sigmoid(x):
    return 1.0 / (1.0 + jnp.exp(-x))


def _softplus(x):
    return jnp.maximum(x, 0.0) + jnp.log(1.0 + jnp.exp(-jnp.abs(x)))


def _mm_kernel(x_ref, w_ref, o_ref):
    o_ref[...] = jnp.dot(x_ref[...].astype(BF16), w_ref[...].astype(BF16),
                         preferred_element_type=F32).astype(o_ref.dtype)


def _mm_tiles(m, k, n, x_bytes, w_bytes, o_bytes):
    best = None
    for tm in (1024, 512, 256, 128, 64, 32, 16, 8):
        if m % tm:
            continue
        for tn in (2048, 1024, 512, 256, 128):
            if n % tn:
                continue
            need = 2 * (tm * k * x_bytes + k * tn * w_bytes + tm * tn * o_bytes)
            if need > MM_TILE_BUDGET_BYTES:
                continue
            score = tm * tn / (tm + tn)
            if best is None or score > best[0]:
                best = (score, tm, tn)
    if best is None:
        raise ValueError(f"no matmul tiling for {(m, k, n)}")
    return best[1], best[2]


def pmm(x, w, out_dtype=F32):
    m, k = x.shape
    n = w.shape[1]
    tm, tn = _mm_tiles(m, k, n, x.dtype.itemsize, w.dtype.itemsize, jnp.dtype(out_dtype).itemsize)
    return pl.pallas_call(
        _mm_kernel,
        grid=(m // tm, n // tn),
        in_specs=[pl.BlockSpec((tm, k), lambda i, j: (i, 0)),
                  pl.BlockSpec((k, tn), lambda i, j: (0, j))],
        out_specs=pl.BlockSpec((tm, tn), lambda i, j: (i, j)),
        out_shape=jax.ShapeDtypeStruct((m, n), out_dtype),
        compiler_params=_params("parallel", "arbitrary"),
        name="mm",
    )(x, w)


def _mm_ws_kernel(x_ref, w_ref, o_ref, wb_ref):
    @pl.when(pl.program_id(1) == 0)
    def _():
        wb_ref[...] = w_ref[...].astype(BF16)

    o_ref[...] = jnp.dot(x_ref[...], wb_ref[...], preferred_element_type=F32).astype(o_ref.dtype)


def pmm_ws(x, w, out_dtype=BF16):
    m, k = x.shape
    n = w.shape[1]
    tm, tn = 1024, 1024
    assert m % tm == 0 and n % tn == 0 and x.dtype == BF16
    o_bytes = jnp.dtype(out_dtype).itemsize
    assert 2 * (tm * k * 2 + k * tn * 4 + tm * tn * o_bytes) + k * tn * 2 <= MM_TILE_BUDGET_BYTES
    return pl.pallas_call(
        _mm_ws_kernel,
        grid=(n // tn, m // tm),
        in_specs=[pl.BlockSpec((tm, k), lambda j, i: (i, 0)),
                  pl.BlockSpec((k, tn), lambda j, i: (0, j))],
        out_specs=pl.BlockSpec((tm, tn), lambda j, i: (i, j)),
        out_shape=jax.ShapeDtypeStruct((m, n), out_dtype),
        scratch_shapes=[pltpu.VMEM((k, tn), BF16)],
        compiler_params=_params("arbitrary", "arbitrary"),
        name="mm_ws",
    )(x, w)


class Rows:
    def __init__(self, batch, has_ctx, lat_tiles):
        self.batch, self.has_ctx, self.lat_tiles = batch, has_ctx, lat_tiles
        self.per_batch = lat_tiles + (1 if has_ctx else 0)
        self.n_tiles = batch * self.per_batch

    def block_of(self, i, other):
        if other.has_ctx == self.has_ctx:
            return i
        assert other.has_ctx and not self.has_ctx
        return (i // self.per_batch) * other.per_batch + 1 + i % self.per_batch

    def mod_row(self, i):
        b = i // self.per_batch
        if not self.has_ctx:
            return b
        return jnp.where(i % self.per_batch == 0, self.batch, b)

    def flags(self, i):
        r = i % self.per_batch
        if not self.has_ctx:
            return False, r == 0, r == self.per_batch - 1
        is_ctx = r == 0
        return is_ctx, is_ctx | (r == 1), is_ctx | (r == self.per_batch - 1)


def _norm_kernel(*refs, has_resid, gate_idx, has_mod, shift_idx, scale_idx, write_x):
    refs = list(refs)
    x_ref = refs.pop(0)
    x = x_ref[...]
    if has_resid:
        y_ref, gmod_ref = refs.pop(0), refs.pop(0)
        x = x + gmod_ref[0, gate_idx:gate_idx + 1, :] * y_ref[...].astype(F32)
    nw_ref = refs.pop(0)
    mod_ref = refs.pop(0) if has_mod else None
    if write_x:
        refs.pop(0)[...] = x
    h_ref = refs.pop(0)
    y = x * lax.rsqrt(jnp.mean(x * x, axis=-1, keepdims=True) + EPS) * nw_ref[...]
    if has_mod:
        y = y * (1.0 + mod_ref[0, scale_idx:scale_idx + 1, :]) + mod_ref[0, shift_idx:shift_idx + 1, :]
    h_ref[...] = y.astype(h_ref.dtype)


def resid_norm(x, x_rows, rows, norm_w, *, resid=None, gate_mod=None, gate_idx=0,
               mod=None, shift_idx=0, scale_idx=1, out_dtype=BF16):
    d = x.shape[-1]
    row = lambda i: (i, 0)
    modspec = pl.BlockSpec((1, N_MOD, d), lambda i: (rows.mod_row(i), 0, 0))
    args = [x]
    in_specs = [pl.BlockSpec((ROW_TILE, d), lambda i: (rows.block_of(i, x_rows), 0))]
    if resid is not None:
        args += [resid, gate_mod]
        in_specs += [pl.BlockSpec((ROW_TILE, d), row), modspec]
    args.append(norm_w.reshape(1, d))
    in_specs.append(pl.BlockSpec((1, d), lambda i: (0, 0)))
    if mod is not None:
        args.append(mod)
        in_specs.append(modspec)
    m = rows.n_tiles * ROW_TILE
    out_shape, out_specs = [], []
    if resid is not None:
        out_shape.append(jax.ShapeDtypeStruct((m, d), F32))
        out_specs.append(pl.BlockSpec((ROW_TILE, d), row))
    out_shape.append(jax.ShapeDtypeStruct((m, d), out_dtype))
    out_specs.append(pl.BlockSpec((ROW_TILE, d), row))
    outs = pl.pallas_call(
        functools.partial(_norm_kernel, has_resid=resid is not None, gate_idx=gate_idx,
                          has_mod=mod is not None, shift_idx=shift_idx, scale_idx=scale_idx,
                          write_x=resid is not None),
        grid=(rows.n_tiles,), in_specs=in_specs, out_specs=out_specs, out_shape=out_shape,
        compiler_params=_params("parallel"), name="resid_norm",
    )(*args)
    return (outs[0], outs[1]) if resid is not None else (None, outs[0])


PREP_SUB_COLS = 256


def _gdn_prep_kernel(x_ref, up_ref, dn_ref, w_ref, o_ref, shift_ref, *, rows):
    @pl.when(pl.program_id(0) == 0)
    def _():
        ri = lax.broadcasted_iota(jnp.int32, (ROW_TILE, ROW_TILE), 0)
        ci = lax.broadcasted_iota(jnp.int32, (ROW_TILE, ROW_TILE), 1)
        shift_ref[0] = jnp.where(ci == ri - 1, 1.0, 0.0).astype(BF16)
        shift_ref[1] = jnp.where(ci == ri + 1, 1.0, 0.0).astype(BF16)

    _, first, last = rows.flags(pl.program_id(0))
    t = lax.broadcasted_iota(jnp.int32, (ROW_TILE, 1), 0)
    for c0 in range(0, GDN_QKV, PREP_SUB_COLS):
        sl = slice(c0, c0 + PREP_SUB_COLS)
        x = x_ref[:, sl]
        above = jnp.where(first, 0.0, up_ref[BF16_SUBLANES - 1:BF16_SUBLANES, sl].astype(F32))
        below = jnp.where(last, 0.0, dn_ref[0:1, sl].astype(F32))
        prev = jnp.where(t == 0, above, jnp.dot(shift_ref[0], x, preferred_element_type=F32))
        nxt = jnp.where(t == ROW_TILE - 1, below, jnp.dot(shift_ref[1], x, preferred_element_type=F32))
        y = prev * w_ref[0:1, sl] + x.astype(F32) * w_ref[1:2, sl] + nxt * w_ref[2:3, sl]
        y = y * _sigmoid(y)
        for h0 in range(0, PREP_SUB_COLS, GDN_DK):
            seg = y[:, h0:h0 + GDN_DK]
            if c0 + h0 < 2 * GDN_QK:
                inv = lax.rsqrt(jnp.sum(seg * seg, axis=-1, keepdims=True) + EPS)
                if c0 + h0 < GDN_QK:
                    inv = inv * GDN_DK ** -0.5
                seg = seg * inv
            o_ref[:, c0 + h0:c0 + h0 + GDN_DK] = seg.astype(o_ref.dtype)


def gdn_prep(p, rows, conv_w):
    m = p.shape[0]
    cb = GDN_QKV_COL // GDN_QKV
    hb = ROW_TILE // BF16_SUBLANES
    n_hblocks = m // BF16_SUBLANES
    return pl.pallas_call(
        functools.partial(_gdn_prep_kernel, rows=rows),
        grid=(rows.n_tiles,),
        in_specs=[pl.BlockSpec((ROW_TILE, GDN_QKV), lambda i: (i, cb)),
                  pl.BlockSpec((BF16_SUBLANES, GDN_QKV), lambda i: (jnp.maximum(i * hb - 1, 0), cb)),
                  pl.BlockSpec((BF16_SUBLANES, GDN_QKV),
                               lambda i: (jnp.minimum((i + 1) * hb, n_hblocks - 1), cb)),
                  pl.BlockSpec((3, GDN_QKV), lambda i: (0, 0))],
        out_specs=pl.BlockSpec((ROW_TILE, GDN_QKV), lambda i: (i, 0)),
        out_shape=jax.ShapeDtypeStruct((m, GDN_QKV), BF16),
        scratch_shapes=[pltpu.VMEM((2, ROW_TILE, ROW_TILE), BF16)],
        compiler_params=_params("arbitrary"), name="gdn_prep",
    )(p, p, p, conv_w)


def _bdot(a, b):
    return jnp.dot(a.astype(BF16), b.astype(BF16), preferred_element_type=F32)


def _bdot_nt(a, b):
    return lax.dot_general(a.astype(BF16), b.astype(BF16), (((1,), (1,)), ((), ())),
                           preferred_element_type=F32)


def _bdot_tn(a, b):
    return lax.dot_general(a.astype(BF16), b.astype(BF16), (((0,), (0,)), ((), ())),
                           preferred_element_type=F32)


def _rev_chunk(t, n_ctx, n_all):
    return jnp.where(t < n_ctx, n_ctx - 1 - t, n_all + n_ctx - 1 - t)


def _chunk_masks(d):
    ri = lax.broadcasted_iota(jnp.int32, (CHUNK, CHUNK), 0)
    ci = lax.broadcasted_iota(jnp.int32, (CHUNK, CHUNK), 1)
    lower, upper = ci <= ri, ci >= ri
    if d == 0:
        return lower, ci < ri, lower.astype(F32), upper.astype(F32)
    return upper, ci > ri, upper.astype(F32), lower.astype(F32)


SCAN_CHUNKS = 2


def _cumsum_dot(tri, x):
    tb = tri.astype(BF16)
    hi = x.astype(BF16)
    rest = x - hi.astype(F32)
    mid = rest.astype(BF16)
    lo = (rest - mid.astype(F32)).astype(BF16)
    return (jnp.dot(tb, hi, preferred_element_type=F32) + jnp.dot(tb, mid, preferred_element_type=F32)
            + jnp.dot(tb, lo, preferred_element_type=F32))


def _gla_kernel(qf, kf, vf, sf, qr, kr, vr, sr, w2_ref, b_ref, of, orr, st_ref):
    @pl.when(pl.program_id(1) == 0)
    def _():
        st_ref[...] = jnp.zeros_like(st_ref)

    dirs = ((qf, kf, vf, sf, of), (qr, kr, vr, sr, orr))
    slots = [[] for _ in range(SCAN_CHUNKS)]
    row_slice = lambda d, slot: slice((slot if d == 0 else SCAN_CHUNKS - 1 - slot) * CHUNK,
                                      (slot if d == 0 else SCAN_CHUNKS - 1 - slot) * CHUNK + CHUNK)
    blocks = [(d, slot) for d in range(N_DIR) for slot in range(SCAN_CHUNKS)]
    zs = {(d, slot): _bdot(dirs[d][3][0, row_slice(d, slot), d * GLA_RANK:(d + 1) * GLA_RANK], w2_ref[d])
          + b_ref[d] for d, slot in blocks}
    bcums = {(d, slot): _cumsum_dot(_chunk_masks(d)[2], -_softplus(-zs[d, slot]) / GLA_TAU)
             for d, slot in blocks}
    for d, (q_ref, k_ref, v_ref, s_ref, o_ref) in enumerate(dirs):
        incl = _chunk_masks(d)[0]
        last = CHUNK - 1 if d == 0 else 0
        for slot in range(SCAN_CHUNKS):
            rs = row_slice(d, slot)
            bcum = bcums[d, slot]
            q_all, k_all, v_all = q_ref[0, rs, :].astype(F32), k_ref[0, rs, :].astype(F32), v_ref[0, rs, :]
            for h in range(GLA_HEADS):
                ks = slice(h * GLA_DK, (h + 1) * GLA_DK)
                vs = slice(h * GLA_DV, (h + 1) * GLA_DV)
                b = bcum[:, ks]
                bl = bcum[last:last + 1, ks]
                k = k_all[:, ks]
                ch = dict(d=d, h=h, vs=vs, rs=rs, o_ref=o_ref, bl=bl, v=v_all[:, vs], incl=incl)
                ch["q_dec"] = q_all[:, ks] * (GLA_DK ** -0.5) * jnp.exp(b)
                ch["k_neg"] = k * jnp.exp(-b)
                ch["k_dec"] = k * jnp.exp(bl - b)
                slots[slot].append(ch)
    chains = [ch for slot in slots for ch in slot]
    for ch in chains:
        ch["scores"] = jnp.where(ch["incl"], _bdot_nt(ch["q_dec"], ch["k_neg"]), 0.0)
    for ch in chains:
        ch["intra"] = _bdot(ch["scores"], ch["v"])
        ch["kv"] = _bdot_tn(ch["v"], ch["k_dec"])
    state = {(d, h): st_ref[d, h] for d in range(N_DIR) for h in range(GLA_HEADS)}
    for slot in slots:
        for ch in slot:
            key = ch["d"], ch["h"]
            out = ch["intra"] + _bdot_nt(ch["q_dec"], state[key])
            ch["o_ref"][0, ch["rs"], ch["vs"]] = out.astype(ch["o_ref"].dtype)
            state[key] = jnp.exp(ch["bl"]) * state[key] + ch["kv"]
    for (d, h), st in state.items():
        st_ref[d, h] = st


def gla_scan(p, small, w2, bias, n_ctx):
    bsz, s, _ = p.shape
    rows_per_step = SCAN_CHUNKS * CHUNK
    n_ctx_steps, n_steps = n_ctx // SCAN_CHUNKS, s // rows_per_step
    assert n_ctx % SCAN_CHUNKS == 0 and s % rows_per_step == 0

    def specs(rev):
        def row(t):
            return _rev_chunk(t, n_ctx_steps, n_steps) if rev else t
        return [pl.BlockSpec((1, rows_per_step, GLA_QK), lambda b, t: (b, row(t), 0)),
                pl.BlockSpec((1, rows_per_step, GLA_QK), lambda b, t: (b, row(t), 1)),
                pl.BlockSpec((1, rows_per_step, GLA_V), lambda b, t: (b, row(t), 2 * GLA_QK // GLA_V)),
                pl.BlockSpec((1, rows_per_step, SMALL_WIDTH), lambda b, t: (b, row(t), 0))]

    out_f = pl.BlockSpec((1, rows_per_step, GLA_V), lambda b, t: (b, t, 0))
    out_r = pl.BlockSpec((1, rows_per_step, GLA_V),
                         lambda b, t: (b, _rev_chunk(t, n_ctx_steps, n_steps), 0))
    return pl.pallas_call(
        _gla_kernel,
        grid=(bsz, n_steps),
        in_specs=specs(False) + specs(True) + [
            pl.BlockSpec((N_DIR, GLA_RANK, GLA_QK), lambda b, t: (0, 0, 0)),
            pl.BlockSpec((N_DIR, 1, GLA_QK), lambda b, t: (0, 0, 0))],
        out_specs=[out_f, out_r],
        out_shape=[jax.ShapeDtypeStruct((bsz, s, GLA_V), BF16)] * 2,
        scratch_shapes=[pltpu.VMEM((N_DIR, GLA_HEADS, GLA_DV, GLA_DK), F32)],
        compiler_params=_params("parallel", "arbitrary"),
        name="gla_scan",
    )(p, p, p, small, p, p, p, small, w2, bias.reshape(N_DIR, 1, GLA_QK))


def _gdn_kernel(qf, kf, vf, sf, qr, kr, vr, sr, alog_ref, dtb_ref, of, orr, s_ref):
    @pl.when(pl.program_id(1) == 0)
    def _():
        s_ref[...] = jnp.zeros_like(s_ref)

    ri = lax.broadcasted_iota(jnp.int32, (CHUNK, CHUNK), 0)
    ci = lax.broadcasted_iota(jnp.int32, (CHUNK, CHUNK), 1)
    eye = (ri == ci).astype(F32)
    same16 = (ri >> 4) == (ci >> 4)
    same32 = (ri >> 5) == (ci >> 5)
    nh = GDN_HEADS
    slots = [[] for _ in range(SCAN_CHUNKS)]
    for d, (q_ref, k_ref, v_ref, sm_ref, o_ref) in enumerate(
            ((qf, kf, vf, sf, of), (qr, kr, vr, sr, orr))):
        incl, strict, tri, tri_t = _chunk_masks(d)
        for slot in range(SCAN_CHUNKS):
            sub = slot if d == 0 else SCAN_CHUNKS - 1 - slot
            rs = slice(sub * CHUNK, (sub + 1) * CHUNK)
            sm = sm_ref[0, rs, :]
            g_all = -jnp.exp(alog_ref[...]) * _softplus(sm + dtb_ref[...])
            beta_all = _sigmoid(sm)
            gcol = g_all[:, A_COL:A_COL + N_DIR * nh]
            grow = g_all.T[A_COL:A_COL + N_DIR * nh, :]
            bcol = jnp.dot(tri, gcol, precision=HIGHEST, preferred_element_type=F32)
            brow = jnp.dot(grow, tri_t, precision=HIGHEST, preferred_element_type=F32)
            last = CHUNK - 1 if d == 0 else 0
            btot = bcol[last:last + 1, :]
            q_all = q_ref[0, rs, :].astype(F32)
            k_all = k_ref[0, rs, :].astype(F32)
            v_all = v_ref[0, rs, :].astype(F32)
            for h in range(nh):
                c = d * nh + h
                hs = slice(h * GDN_DK, (h + 1) * GDN_DK)
                ch = dict(d=d, h=h, hs=hs, rs=rs, o_ref=o_ref, incl=incl, strict=strict)
                ch["bc"] = bcol[:, c:c + 1]
                ch["br"] = brow[c:c + 1, :]
                ch["bl"] = btot[:, c:c + 1]
                ch["beta"] = beta_all[:, BT_COL + c:BT_COL + c + 1]
                ch["q"], ch["k"], ch["v"] = q_all[:, hs], k_all[:, hs], v_all[:, hs]
                slots[slot].append(ch)

    def stage_scores(chains):
        for ch in chains:
            incl = ch["incl"]
            ch["decay"] = jnp.where(incl, jnp.exp(jnp.where(incl, ch["bc"] - ch["br"], 0.0)), 0.0)
            ch["kb"] = ch["k"] * ch["beta"]
            ch["a2"] = _bdot_nt(jnp.concatenate([ch["kb"], ch["q"]], axis=0), ch["k"])

    def stage_split(chains):
        for ch in chains:
            a2 = ch.pop("a2")
            n_mat = -jnp.where(ch["strict"], a2[:CHUNK] * ch["decay"], 0.0)
            ch["attn"] = a2[CHUNK:] * ch["decay"]
            n_diag = jnp.where(same16, n_mat, 0.0)
            ch["n32"] = jnp.where(same32, n_mat, 0.0) - n_diag
            ch["n64"] = jnp.where(same32, 0.0, n_mat)
            ch["p"] = eye + n_diag
            ch["n"] = _bdot(n_diag, n_diag)

    def stage_diag(square):
        def run(chains):
            for ch in chains:
                ch["p"] = ch["p"] + _bdot(ch["p"], ch["n"])
                if square:
                    ch["n"] = _bdot(ch["n"], ch["n"])
        return run

    def stage_merge_a(off):
        def run(chains):
            for ch in chains:
                ch["x"] = _bdot(ch["p"], ch[off])
        return run

    def stage_merge_b(chains):
        for ch in chains:
            ch["p"] = ch["p"] + _bdot(ch.pop("x"), ch["p"])

    def stage_uw(chains):
        for ch in chains:
            ch["e_b"] = jnp.exp(ch["bc"])
            uw = _bdot(ch["p"], jnp.concatenate([ch["v"] * ch["beta"], ch["kb"] * ch["e_b"]], axis=1))
            ch["u"], ch["w"] = uw[:, :GDN_DV], uw[:, GDN_DV:]

    state = {(d, h): s_ref[d, h] for d in range(N_DIR) for h in range(nh)}

    def tail_read(chains):
        for ch in chains:
            s = state[ch["d"], ch["h"]]
            ch["wq"] = _bdot(jnp.concatenate([ch["w"], ch["q"] * ch["e_b"]], axis=0), s)

    def tail_update(chains):
        for ch in chains:
            wq = ch.pop("wq")
            v_new = ch["u"] - wq[:CHUNK]
            ch["o_ref"][0, ch["rs"], ch["hs"]] = (
                wq[CHUNK:] + _bdot(ch["attn"], v_new)).astype(ch["o_ref"].dtype)
            k_dec = ch["k"] * jnp.exp(ch["bl"] - ch["bc"])
            key = ch["d"], ch["h"]
            state[key] = jnp.exp(ch["bl"]) * state[key] + _bdot_tn(k_dec, v_new)

    stages = [stage_scores, stage_split, stage_diag(True), stage_diag(True), stage_diag(False),
              stage_merge_a("n32"), stage_merge_b, stage_merge_a("n64"), stage_merge_b, stage_uw]
    for stage in stages:
        stage(slots[0])
    for slot in range(1, SCAN_CHUNKS):
        for i, stage in enumerate(stages):
            stage(slots[slot])
            if i == 1:
                tail_read(slots[slot - 1])
            elif i == 4:
                tail_update(slots[slot - 1])
    tail_read(slots[-1])
    tail_update(slots[-1])
    for (d, h), s in state.items():
        s_ref[d, h] = s


def gdn_scan(qkv, small, a_log, dt_bias, n_ctx):
    bsz, s, _ = qkv.shape
    rows_per_step = SCAN_CHUNKS * CHUNK
    n_ctx_steps, n_steps = n_ctx // SCAN_CHUNKS, s // rows_per_step
    assert n_ctx % SCAN_CHUNKS == 0 and s % rows_per_step == 0
    lane_row = lambda t: jnp.zeros((1, SMALL_WIDTH), F32).at[0, A_COL:A_COL + N_DIR * GDN_HEADS].set(
        t.astype(F32).reshape(-1))

    def specs(rev):
        def row(t):
            return _rev_chunk(t, n_ctx_steps, n_steps) if rev else t
        return [pl.BlockSpec((1, rows_per_step, GDN_QK), lambda b, t: (b, row(t), 0)),
                pl.BlockSpec((1, rows_per_step, GDN_QK), lambda b, t: (b, row(t), 1)),
                pl.BlockSpec((1, rows_per_step, GDN_V), lambda b, t: (b, row(t), 2)),
                pl.BlockSpec((1, rows_per_step, SMALL_WIDTH), lambda b, t: (b, row(t), 0))]

    const = pl.BlockSpec((1, SMALL_WIDTH), lambda b, t: (0, 0))
    out_f = pl.BlockSpec((1, rows_per_step, GDN_V), lambda b, t: (b, t, 0))
    out_r = pl.BlockSpec((1, rows_per_step, GDN_V),
                         lambda b, t: (b, _rev_chunk(t, n_ctx_steps, n_steps), 0))
    return pl.pallas_call(
        _gdn_kernel,
        grid=(bsz, n_steps),
        in_specs=specs(False) + specs(True) + [const, const],
        out_specs=[out_f, out_r],
        out_shape=[jax.ShapeDtypeStruct((bsz, s, GDN_V), BF16)] * 2,
        scratch_shapes=[pltpu.VMEM((N_DIR, GDN_HEADS, GDN_DK, GDN_DV), F32)],
        compiler_params=_params("parallel", "arbitrary"),
        name="gdn_scan",
    )(qkv, qkv, qkv, small, qkv, qkv, qkv, small, lane_row(a_log), lane_row(dt_bias))


FFN_COL_TILE = D_FF // 2


FFN_EXT_ROWS = ROW_TILE + 2 * GRID_W
FFN_SUB_COLS = 256


def _ffn_gate_kernel(a_ref, up_ref, dn_ref, v_ref, w_ref, o_ref, shift_ref, *, rows):
    g, n = GRID_W, FFN_EXT_ROWS

    @pl.when((pl.program_id(0) == 0) & (pl.program_id(1) == 0))
    def _():
        ri = lax.broadcasted_iota(jnp.int32, (n, n), 0)
        ci = lax.broadcasted_iota(jnp.int32, (n, n), 1)
        for kind in range(2):
            col = (ri & (g - 1)) if kind == 0 else ri - g
            last_col = g - 1 if kind == 0 else ROW_TILE - 1
            shift_ref[kind, 0] = jnp.where((ci == ri - 1) & (col != 0), 1.0, 0.0).astype(BF16)
            shift_ref[kind, 1] = jnp.where((ci == ri + 1) & (col != last_col), 1.0, 0.0).astype(BF16)

    is_ctx, first, last = rows.flags(pl.program_id(0))
    kind = jnp.where(is_ctx, 1, 0)
    halo_zero = jnp.zeros(up_ref.shape, up_ref.dtype)
    up = jnp.where(first, halo_zero, up_ref[...])
    dn = jnp.where(last, halo_zero, dn_ref[...])
    ext_all = jnp.concatenate([up, a_ref[...], dn], axis=0)
    vert = jnp.where(is_ctx, 0.0, 1.0)
    for c0 in range(0, FFN_COL_TILE, FFN_SUB_COLS):
        cs = slice(c0, c0 + FFN_SUB_COLS)
        ext = ext_all[:, cs]
        shifted = (jnp.dot(shift_ref[kind, 0], ext, preferred_element_type=F32),
                   ext.astype(F32),
                   jnp.dot(shift_ref[kind, 1], ext, preferred_element_type=F32))
        acc = None
        for dr in (-1, 0, 1):
            for dc in (-1, 0, 1):
                val = shifted[dc + 1][g + dr * g:g + dr * g + ROW_TILE]
                wt = w_ref[3 * (dr + 1) + dc + 1:3 * (dr + 1) + dc + 2, cs]
                if dr != 0:
                    wt = wt * vert
                acc = val * wt if acc is None else acc + val * wt
        o_ref[:, cs] = (acc * _sigmoid(acc) * v_ref[:, cs].astype(F32)).astype(o_ref.dtype)


def ffn_gate(up, rows, w_conv):
    m = up.shape[0]
    nc = D_FF // FFN_COL_TILE
    gb = ROW_TILE // GRID_W
    n_gblocks = m // GRID_W
    return pl.pallas_call(
        functools.partial(_ffn_gate_kernel, rows=rows),
        grid=(rows.n_tiles, nc),
        in_specs=[pl.BlockSpec((ROW_TILE, FFN_COL_TILE), lambda i, j: (i, j)),
                  pl.BlockSpec((GRID_W, FFN_COL_TILE), lambda i, j: (jnp.maximum(i * gb - 1, 0), j)),
                  pl.BlockSpec((GRID_W, FFN_COL_TILE),
                               lambda i, j: (jnp.minimum((i + 1) * gb, n_gblocks - 1), j)),
                  pl.BlockSpec((ROW_TILE, FFN_COL_TILE), lambda i, j: (i, nc + j)),
                  pl.BlockSpec((9, FFN_COL_TILE), lambda i, j: (0, j))],
        out_specs=pl.BlockSpec((ROW_TILE, FFN_COL_TILE), lambda i, j: (i, j)),
        out_shape=jax.ShapeDtypeStruct((m, D_FF), BF16),
        scratch_shapes=[pltpu.VMEM((2, 2, FFN_EXT_ROWS, FFN_EXT_ROWS), BF16)],
        compiler_params=_params("arbitrary", "arbitrary"), name="ffn_gate",
    )(up, up, up, up, w_conv.reshape(9, D_FF))


BRANCH_COLS = GLA_V
assert GDN_V == BRANCH_COLS and POOL_WIDTH == BRANCH_COLS
BRANCH_WIDTH = N_BRANCH * BRANCH_COLS
Z_GLA_COL = 2 * GLA_QK + GLA_V
Z_GDN_COL = GDN_QKV_COL + GDN_QKV
POOL_COL = Z_GDN_COL + GDN_V
GATES_COL = POOL_COL + POOL_WIDTH
POOL_HALO = BF16_SUBLANES
assert POOL_HALO >= max(POOL_WINDOWS) // 2


def _headnorm_gate(o, z, nw, n_heads, width, o_ref, col0):
    for h in range(n_heads):
        sl = slice(h * width, (h + 1) * width)
        oh, zh = o[:, sl], z[:, sl]
        inv = lax.rsqrt(jnp.mean(oh * oh, axis=-1, keepdims=True) + EPS)
        o_ref[:, col0 + h * width:col0 + (h + 1) * width] = (
            oh * inv * nw * (zh * _sigmoid(zh))).astype(o_ref.dtype)


def _merge_prep_kernel(gf_ref, gr_ref, df_ref, dr_ref, za_ref, zb_ref, u_ref, uup_ref, udn_ref,
                       gnw_ref, dnw_ref, pw_ref, ps_ref, o_ref, win_ref, *, rows):
    halo, n = POOL_HALO, ROW_TILE + 2 * POOL_HALO

    @pl.when(pl.program_id(0) == 0)
    def _():
        ri = lax.broadcasted_iota(jnp.int32, (ROW_TILE, n), 0)
        ci = lax.broadcasted_iota(jnp.int32, (ROW_TILE, n), 1) - halo
        for gi, win in enumerate(POOL_WINDOWS):
            lo = ri - win // 2
            win_ref[gi] = jnp.where((ci >= lo) & (ci < lo + win), 1.0, 0.0).astype(BF16)

    _, first, last = rows.flags(pl.program_id(0))
    _headnorm_gate(gf_ref[...].astype(F32) + gr_ref[...].astype(F32), za_ref[...].astype(F32),
                   gnw_ref[...], GLA_HEADS, GLA_DV, o_ref, 0)
    _headnorm_gate(df_ref[...].astype(F32) + dr_ref[...].astype(F32), zb_ref[...].astype(F32),
                   dnw_ref[...], GDN_HEADS, GDN_DV, o_ref, GLA_V)
    halo_zero = jnp.zeros(uup_ref.shape, uup_ref.dtype)
    u = u_ref[...]
    ext = jnp.concatenate([jnp.where(first, halo_zero, uup_ref[...]), u,
                           jnp.where(last, halo_zero, udn_ref[...])], axis=0)
    t = lax.broadcasted_iota(jnp.int32, (ROW_TILE, 1), 0)
    for gi, win in enumerate(POOL_WINDOWS):
        sl = slice(gi * POOL_GROUP, (gi + 1) * POOL_GROUP)
        half = win // 2
        win_sum = jnp.dot(win_ref[gi], ext[:, sl], preferred_element_type=F32)
        lo_clip = jnp.where(first, jnp.maximum(half - t, 0), 0)
        hi_clip = jnp.where(last, jnp.maximum(t - half + win - ROW_TILE, 0), 0)
        cnt = (win - lo_clip - hi_clip).astype(F32)
        pg = win_sum / cnt - u[:, sl].astype(F32)
        yp = _bdot(pg, pw_ref[gi]) * ps_ref[:, sl]
        o_ref[:, GLA_V + GDN_V + gi * POOL_GROUP:GLA_V + GDN_V + (gi + 1) * POOL_GROUP] = yp.astype(o_ref.dtype)


def merge_prep(p, o_gla, o_gdn, p_rows, rows, gla_nw, gdn_nw, pool_w, pool_scale):
    hb = ROW_TILE // BF16_SUBLANES
    n_hblocks = p.shape[0] // BF16_SUBLANES
    blk = lambda i: rows.block_of(i, p_rows)
    wide = lambda col: pl.BlockSpec((ROW_TILE, BRANCH_COLS), lambda i: (blk(i), col // BRANCH_COLS))
    const = lambda shape: pl.BlockSpec(shape, lambda i: (0,) * len(shape))
    pc = POOL_COL // POOL_WIDTH
    return pl.pallas_call(
        functools.partial(_merge_prep_kernel, rows=rows),
        grid=(rows.n_tiles,),
        in_specs=[wide(0), wide(0), wide(0), wide(0), wide(Z_GLA_COL), wide(Z_GDN_COL), wide(POOL_COL),
                  pl.BlockSpec((POOL_HALO, POOL_WIDTH), lambda i: (jnp.maximum(blk(i) * hb - 1, 0), pc)),
                  pl.BlockSpec((POOL_HALO, POOL_WIDTH),
                               lambda i: (jnp.minimum((blk(i) + 1) * hb, n_hblocks - 1), pc)),
                  const((1, GLA_DV)), const((1, GDN_DV)),
                  const((POOL_GROUPS, POOL_GROUP, POOL_GROUP)), const((1, POOL_WIDTH))],
        out_specs=pl.BlockSpec((ROW_TILE, BRANCH_WIDTH), lambda i: (i, 0)),
        out_shape=jax.ShapeDtypeStruct((rows.n_tiles * ROW_TILE, BRANCH_WIDTH), BF16),
        scratch_shapes=[pltpu.VMEM((POOL_GROUPS, ROW_TILE, ROW_TILE + 2 * POOL_HALO), BF16)],
        compiler_params=_params("arbitrary"), name="merge_prep",
    )(o_gla[0], o_gla[1], o_gdn[0], o_gdn[1], p, p, p, p, p,
      gla_nw.reshape(1, GLA_DV), gdn_nw.reshape(1, GDN_DV), pool_w.astype(BF16),
      pool_scale.reshape(1, POOL_WIDTH))


def _branch_kernel(ya, yb, yc, ga, gb, gc, wa, wb, wc, o_ref):
    acc = None
    for y_ref, g_ref, w_ref in ((ya, ga, wa), (yb, gb, wb), (yc, gc, wc)):
        term = _sigmoid(g_ref[...].astype(F32)) * jnp.dot(y_ref[...], w_ref[...], preferred_element_type=F32)
        acc = term if acc is None else acc + term
    o_ref[...] = acc.astype(o_ref.dtype)


def branch_merge(ycat, p, p_rows, rows, w_gla, w_gdn, w_pool):
    blk = lambda i: rows.block_of(i, p_rows)
    ysp = lambda k: pl.BlockSpec((ROW_TILE, BRANCH_COLS), lambda i: (i, k))
    gsp = lambda k: pl.BlockSpec((ROW_TILE, D_MODEL), lambda i: (blk(i), GATES_COL // D_MODEL + k))
    wsp = pl.BlockSpec((BRANCH_COLS, D_MODEL), lambda i: (0, 0))
    return pl.pallas_call(
        _branch_kernel,
        grid=(rows.n_tiles,),
        in_specs=[ysp(0), ysp(1), ysp(2), gsp(0), gsp(1), gsp(2), wsp, wsp, wsp],
        out_specs=pl.BlockSpec((ROW_TILE, D_MODEL), lambda i: (i, 0)),
        out_shape=jax.ShapeDtypeStruct((rows.n_tiles * ROW_TILE, D_MODEL), BF16),
        compiler_params=_params("parallel"), name="branch_merge",
    )(ycat, ycat, ycat, p, p, p, w_gla, w_gdn, w_pool)


def permute_w_in(w):
    q, k, v, zg, lr, gqkv, ga, gbt, gz, pool, gates = _split_cols(w, IN_SPLITS)
    main = jnp.concatenate([q, k, v, zg, gqkv, gz, pool, gates], axis=1)
    small = jnp.concatenate([lr, ga, gbt, jnp.zeros((w.shape[0], SMALL_WIDTH - SMALL_COLS), w.dtype)], axis=1)
    return main.astype(BF16), small.astype(BF16)


def hybrid_mixer(h, rows, out_rows, bsz, w_main, w_small, gla_w2, gla_b, gla_nw, gdn_cw, gdn_alog,
                 gdn_dtb, gdn_nw, pool_w, pool_scale, w_br_gla, w_br_gdn, w_br_pool, w_out):
    m = h.shape[0]
    s = m // bsz
    p = pmm(h, w_main, BF16)
    small = pmm(h, w_small)
    p3, small3 = p.reshape(bsz, s, MAIN_WIDTH), small.reshape(bsz, s, SMALL_WIDTH)
    n_ctx = (ROW_TILE if rows.has_ctx else 0) // CHUNK
    o_gla = [t.reshape(m, GLA_V) for t in gla_scan(p3, small3, gla_w2.astype(BF16), gla_b, n_ctx)]
    qkv = gdn_prep(p, rows, gdn_cw).reshape(bsz, s, GDN_QKV)
    o_gdn = [t.reshape(m, GDN_V) for t in gdn_scan(qkv, small3, gdn_alog, gdn_dtb, n_ctx)]
    ycat = merge_prep(p, o_gla, o_gdn, rows, out_rows, gla_nw, gdn_nw, pool_w, pool_scale)
    mrg = branch_merge(ycat, p, rows, out_rows, w_br_gla, w_br_gdn, w_br_pool)
    return pmm_ws(mrg, w_out)


def kernel(x, c, ctx, c_ctx, ada_w, ada_b, norm1_w, norm2_w, w_in, gla_lr_w2, gla_lr_b,
           gla_norm_w, gdn_conv_w, gdn_a_log, gdn_dt_bias, gdn_norm_w, pool_w, pool_scale,
           w_br_gla, w_br_gdn, w_br_pool, w_out, ffn_up, ffn_conv, ffn_down, final_norm_w):
    B, T, _ = x.shape
    assert ctx.shape[1] == ROW_TILE and T % ROW_TILE == 0 and (T // GRID_W) % (ROW_TILE // GRID_W) == 0
    s_all = ROW_TILE + T
    all_rows = Rows(B, True, T // ROW_TILE)
    lat_rows = Rows(B, False, T // ROW_TILE)
    xa = jnp.concatenate([ctx, x], axis=1).reshape(B * s_all, D_MODEL)
    n_cond = -(-(B + 1) // BF16_SUBLANES) * BF16_SUBLANES
    cond = jnp.concatenate([c, c_ctx[None, :], jnp.zeros((n_cond - B - 1, D_MODEL), F32)], axis=0)
    cond = jax.nn.silu(cond)
    mods = [(pmm(cond, ada_w[l]) + ada_b[l]).reshape(n_cond, N_MOD, D_MODEL) for l in range(DEPTH)]
    _, h = resid_norm(xa, all_rows, all_rows, norm1_w[0], mod=mods[0], shift_idx=0, scale_idx=1)
    x_rows = all_rows
    for l in range(DEPTH):
        last = l == DEPTH - 1
        rows = lat_rows if last else all_rows
        w_main, w_small = permute_w_in(w_in[l])
        y = hybrid_mixer(h, x_rows, rows, B, w_main, w_small, gla_lr_w2[l], gla_lr_b[l], gla_norm_w[l],
                         gdn_conv_w[l], gdn_a_log[l], gdn_dt_bias[l], gdn_norm_w[l],
                         pool_w[l], pool_scale[l], w_br_gla[l].astype(BF16),
                         w_br_gdn[l].astype(BF16), w_br_pool[l].astype(BF16), w_out[l])
        xa, h2 = resid_norm(xa, x_rows, rows, norm2_w[l], resid=y, gate_mod=mods[l], gate_idx=2,
                            mod=mods[l], shift_idx=3, scale_idx=4)
        x_rows = rows
        up = pmm_ws(h2, ffn_up[l])
        gated = ffn_gate(up, rows, ffn_conv[l])
        dn = pmm(gated, ffn_down[l].astype(BF16), BF16)
        if last:
            _, out = resid_norm(xa, x_rows, rows, final_norm_w, resid=dn, gate_mod=mods[l], gate_idx=5,
                                out_dtype=F32)
            return out.reshape(B, T, D_MODEL)
        xa, h = resid_norm(xa, x_rows, rows, norm1_w[l + 1], resid=dn, gate_mod=mods[l], gate_idx=5,
                           mod=mods[l + 1], shift_idx=0, scale_idx=1)
```

```python
import functools

import jax
import jax.numpy as jnp
from jax import lax
from jax.experimental import pallas as pl
from jax.experimental.pallas import tpu as pltpu

D_MODEL = 2048
DEPTH = 4
GRID_W = 64
CHUNK = 64
N_DIR = 2
N_BRANCH = 3
N_MOD = 6
EPS = 1e-6

GLA_HEADS = 4
GLA_DK = 128
GLA_DV = 256
GLA_RANK = 16
GLA_TAU = 16.0

GDN_HEADS = 8
GDN_DK = 128
GDN_DV = 128

POOL_WINDOWS = (2, 4, 8, 16)
POOL_GROUPS = 4
POOL_GROUP = 256

D_FF = 5632

GLA_QK = GLA_HEADS * GLA_DK
GLA_V = GLA_HEADS * GLA_DV
GDN_QK = GDN_HEADS * GDN_DK
GDN_V = GDN_HEADS * GDN_DV
GDN_QKV = 2 * GDN_QK + GDN_V
POOL_WIDTH = POOL_GROUPS * POOL_GROUP
IN_SPLITS = (GLA_QK, GLA_QK, GLA_V, GLA_V, N_DIR * GLA_RANK,
             GDN_QKV, N_DIR * GDN_HEADS, N_DIR * GDN_HEADS, GDN_V,
             POOL_WIDTH, N_BRANCH * D_MODEL)
F32 = jnp.float32
BF16 = jnp.bfloat16
HIGHEST = lax.Precision.HIGHEST

V7X_VMEM_BYTES = 64 * 1024 * 1024
VMEM_LIMIT_BYTES = V7X_VMEM_BYTES * 3 // 4
MM_TILE_BUDGET_BYTES = V7X_VMEM_BYTES * 5 // 8
LANES = 128
BF16_SUBLANES = 16

SMALL_WIDTH = LANES
SMALL_COLS = N_DIR * GLA_RANK + 2 * N_DIR * GDN_HEADS
A_COL = N_DIR * GLA_RANK
BT_COL = A_COL + N_DIR * GDN_HEADS
MAIN_SPLITS = (GLA_QK, GLA_QK, GLA_V, GLA_V, GDN_QKV, GDN_V, POOL_WIDTH, N_BRANCH * D_MODEL)
MAIN_WIDTH = sum(MAIN_SPLITS)
GDN_QKV_COL = 2 * GLA_QK + 2 * GLA_V

ROW_TILE = 256


def _split_cols(t, sizes):
    parts, start = [], 0
    for size in sizes:
        parts.append(t[..., start:start + size])
        start += size
    return parts


def _params(*sem):
    return pltpu.CompilerParams(dimension_semantics=sem, vmem_limit_bytes=VMEM_LIMIT_BYTES)


def _sigmoid(x):
    return 1.0 / (1.0 + jnp.exp(-x))


def _softplus(x):
    return jnp.maximum(x, 0.0) + jnp.log(1.0 + jnp.exp(-jnp.abs(x)))


def _mm_kernel(x_ref, w_ref, o_ref):
    o_ref[...] = jnp.dot(x_ref[...].astype(BF16), w_ref[...].astype(BF16),
                         preferred_element_type=F32).astype(o_ref.dtype)


def _mm_tiles(m, k, n, x_bytes, w_bytes, o_bytes):
    best = None
    for tm in (1024, 512, 256, 128, 64, 32, 16, 8):
        if m % tm:
            continue
        for tn in (2048, 1024, 512, 256, 128):
            if n % tn:
                continue
            need = 2 * (tm * k * x_bytes + k * tn * w_bytes + tm * tn * o_bytes)
            if need > MM_TILE_BUDGET_BYTES:
                continue
            score = tm * tn / (tm + tn)
            if best is None or score > best[0]:
                best = (score, tm, tn)
    if best is None:
        raise ValueError(f"no matmul tiling for {(m, k, n)}")
    return best[1], best[2]


def pmm(x, w, out_dtype=F32):
    m, k = x.shape
    n = w.shape[1]
    tm, tn = _mm_tiles(m, k, n, x.dtype.itemsize, w.dtype.itemsize, jnp.dtype(out_dtype).itemsize)
    return pl.pallas_call(
        _mm_kernel,
        grid=(m // tm, n // tn),
        in_specs=[pl.BlockSpec((tm, k), lambda i, j: (i, 0)),
                  pl.BlockSpec((k, tn), lambda i, j: (0, j))],
        out_specs=pl.BlockSpec((tm, tn), lambda i, j: (i, j)),
        out_shape=jax.ShapeDtypeStruct((m, n), out_dtype),
        compiler_params=_params("parallel", "arbitrary"),
        name="mm",
    )(x, w)


class Rows:
    def __init__(self, batch, has_ctx, lat_tiles):
        self.batch, self.has_ctx, self.lat_tiles = batch, has_ctx, lat_tiles
        self.per_batch = lat_tiles + (1 if has_ctx else 0)
        self.n_tiles = batch * self.per_batch

    def block_of(self, i, other):
        if other.has_ctx == self.has_ctx:
            return i
        assert other.has_ctx and not self.has_ctx
        return (i // self.per_batch) * other.per_batch + 1 + i % self.per_batch

    def mod_row(self, i):
        b = i // self.per_batch
        if not self.has_ctx:
            return b
        return jnp.where(i % self.per_batch == 0, self.batch, b)

    def flags(self, i):
        r = i % self.per_batch
        if not self.has_ctx:
            return False, r == 0, r == self.per_batch - 1
        is_ctx = r == 0
        return is_ctx, is_ctx | (r == 1), is_ctx | (r == self.per_batch - 1)


def _norm_kernel(*refs, has_resid, gate_idx, has_mod, shift_idx, scale_idx, write_x):
    refs = list(refs)
    x_ref = refs.pop(0)
    x = x_ref[...]
    if has_resid:
        y_ref, gmod_ref = refs.pop(0), refs.pop(0)
        x = x + gmod_ref[0, gate_idx:gate_idx + 1, :] * y_ref[...].astype(F32)
    nw_ref = refs.pop(0)
    mod_ref = refs.pop(0) if has_mod else None
    if write_x:
        refs.pop(0)[...] = x
    h_ref = refs.pop(0)
    y = x * lax.rsqrt(jnp.mean(x * x, axis=-1, keepdims=True) + EPS) * nw_ref[...]
    if has_mod:
        y = y * (1.0 + mod_ref[0, scale_idx:scale_idx + 1, :]) + mod_ref[0, shift_idx:shift_idx + 1, :]
    h_ref[...] = y.astype(h_ref.dtype)


def resid_norm(x, x_rows, rows, norm_w, *, resid=None, gate_mod=None, gate_idx=0,
               mod=None, shift_idx=0, scale_idx=1, out_dtype=BF16):
    d = x.shape[-1]
    row = lambda i: (i, 0)
    modspec = pl.BlockSpec((1, N_MOD, d), lambda i: (rows.mod_row(i), 0, 0))
    args = [x]
    in_specs = [pl.BlockSpec((ROW_TILE, d), lambda i: (rows.block_of(i, x_rows), 0))]
    if resid is not None:
        args += [resid, gate_mod]
        in_specs += [pl.BlockSpec((ROW_TILE, d), row), modspec]
    args.append(norm_w.reshape(1, d))
    in_specs.append(pl.BlockSpec((1, d), lambda i: (0, 0)))
    if mod is not None:
        args.append(mod)
        in_specs.append(modspec)
    m = rows.n_tiles * ROW_TILE
    out_shape, out_specs = [], []
    if resid is not None:
        out_shape.append(jax.ShapeDtypeStruct((m, d), F32))
        out_specs.append(pl.BlockSpec((ROW_TILE, d), row))
    out_shape.append(jax.ShapeDtypeStruct((m, d), out_dtype))
    out_specs.append(pl.BlockSpec((ROW_TILE, d), row))
    outs = pl.pallas_call(
        functools.partial(_norm_kernel, has_resid=resid is not None, gate_idx=gate_idx,
                          has_mod=mod is not None, shift_idx=shift_idx, scale_idx=scale_idx,
                          write_x=resid is not None),
        grid=(rows.n_tiles,), in_specs=in_specs, out_specs=out_specs, out_shape=out_shape,
        compiler_params=_params("parallel"), name="resid_norm",
    )(*args)
    return (outs[0], outs[1]) if resid is not None else (None, outs[0])


PREP_SUB_COLS = 256


def _gdn_prep_kernel(x_ref, up_ref, dn_ref, w_ref, o_ref, shift_ref, *, rows):
    @pl.when(pl.program_id(0) == 0)
    def _():
        ri = lax.broadcasted_iota(jnp.int32, (ROW_TILE, ROW_TILE), 0)
        ci = lax.broadcasted_iota(jnp.int32, (ROW_TILE, ROW_TILE), 1)
        shift_ref[0] = jnp.where(ci == ri - 1, 1.0, 0.0).astype(BF16)
        shift_ref[1] = jnp.where(ci == ri + 1, 1.0, 0.0).astype(BF16)

    _, first, last = rows.flags(pl.program_id(0))
    t = lax.broadcasted_iota(jnp.int32, (ROW_TILE, 1), 0)
    for c0 in range(0, GDN_QKV, PREP_SUB_COLS):
        sl = slice(c0, c0 + PREP_SUB_COLS)
        x = x_ref[:, sl]
        above = jnp.where(first, 0.0, up_ref[BF16_SUBLANES - 1:BF16_SUBLANES, sl].astype(F32))
        below = jnp.where(last, 0.0, dn_ref[0:1, sl].astype(F32))
        prev = jnp.where(t == 0, above, jnp.dot(shift_ref[0], x, preferred_element_type=F32))
        nxt = jnp.where(t == ROW_TILE - 1, below, jnp.dot(shift_ref[1], x, preferred_element_type=F32))
        y = prev * w_ref[0:1, sl] + x.astype(F32) * w_ref[1:2, sl] + nxt * w_ref[2:3, sl]
        y = y * _sigmoid(y)
        for h0 in range(0, PREP_SUB_COLS, GDN_DK):
            seg = y[:, h0:h0 + GDN_DK]
            if c0 + h0 < 2 * GDN_QK:
                inv = lax.rsqrt(jnp.sum(seg * seg, axis=-1, keepdims=True) + EPS)
                if c0 + h0 < GDN_QK:
                    inv = inv * GDN_DK ** -0.5
                seg = seg * inv
            o_ref[:, c0 + h0:c0 + h0 + GDN_DK] = seg.astype(o_ref.dtype)


def gdn_prep(p, rows, conv_w):
    m = p.shape[0]
    cb = GDN_QKV_COL // GDN_QKV
    hb = ROW_TILE // BF16_SUBLANES
    n_hblocks = m // BF16_SUBLANES
    return pl.pallas_call(
        functools.partial(_gdn_prep_kernel, rows=rows),
        grid=(rows.n_tiles,),
        in_specs=[pl.BlockSpec((ROW_TILE, GDN_QKV), lambda i: (i, cb)),
                  pl.BlockSpec((BF16_SUBLANES, GDN_QKV), lambda i: (jnp.maximum(i * hb - 1, 0), cb)),
                  pl.BlockSpec((BF16_SUBLANES, GDN_QKV),
                               lambda i: (jnp.minimum((i + 1) * hb, n_hblocks - 1), cb)),
                  pl.BlockSpec((3, GDN_QKV), lambda i: (0, 0))],
        out_specs=pl.BlockSpec((ROW_TILE, GDN_QKV), lambda i: (i, 0)),
        out_shape=jax.ShapeDtypeStruct((m, GDN_QKV), BF16),
        scratch_shapes=[pltpu.VMEM((2, ROW_TILE, ROW_TILE), BF16)],
        compiler_params=_params("arbitrary"), name="gdn_prep",
    )(p, p, p, conv_w)


def _bdot(a, b):
    return jnp.dot(a.astype(BF16), b.astype(BF16), preferred_element_type=F32)


def _bdot_nt(a, b):
    return lax.dot_general(a.astype(BF16), b.astype(BF16), (((1,), (1,)), ((), ())),
                           preferred_element_type=F32)


def _bdot_tn(a, b):
    return lax.dot_general(a.astype(BF16), b.astype(BF16), (((0,), (0,)), ((), ())),
                           preferred_element_type=F32)


def _rev_chunk(t, n_ctx, n_all):
    return jnp.where(t < n_ctx, n_ctx - 1 - t, n_all + n_ctx - 1 - t)


def _chunk_masks(d):
    ri = lax.broadcasted_iota(jnp.int32, (CHUNK, CHUNK), 0)
    ci = lax.broadcasted_iota(jnp.int32, (CHUNK, CHUNK), 1)
    lower, upper = ci <= ri, ci >= ri
    if d == 0:
        return lower, ci < ri, lower.astype(F32), upper.astype(F32)
    return upper, ci > ri, upper.astype(F32), lower.astype(F32)


SCAN_CHUNKS = 2


def _cumsum_dot(tri, x):
    tb = tri.astype(BF16)
    hi = x.astype(BF16)
    rest = x - hi.astype(F32)
    mid = rest.astype(BF16)
    lo = (rest - mid.astype(F32)).astype(BF16)
    return (jnp.dot(tb, hi, preferred_element_type=F32) + jnp.dot(tb, mid, preferred_element_type=F32)
            + jnp.dot(tb, lo, preferred_element_type=F32))


def _gla_kernel(qf, kf, vf, sf, qr, kr, vr, sr, w2_ref, b_ref, of, orr, st_ref):
    @pl.when(pl.program_id(1) == 0)
    def _():
        st_ref[...] = jnp.zeros_like(st_ref)

    dirs = ((qf, kf, vf, sf, of), (qr, kr, vr, sr, orr))
    slots = [[] for _ in range(SCAN_CHUNKS)]
    row_slice = lambda d, slot: slice((slot if d == 0 else SCAN_CHUNKS - 1 - slot) * CHUNK,
                                      (slot if d == 0 else SCAN_CHUNKS - 1 - slot) * CHUNK + CHUNK)
    blocks = [(d, slot) for d in range(N_DIR) for slot in range(SCAN_CHUNKS)]
    zs = {(d, slot): _bdot(dirs[d][3][0, row_slice(d, slot), d * GLA_RANK:(d + 1) * GLA_RANK], w2_ref[d])
          + b_ref[d] for d, slot in blocks}
    bcums = {(d, slot): _cumsum_dot(_chunk_masks(d)[2], -_softplus(-zs[d, slot]) / GLA_TAU)
             for d, slot in blocks}
    for d, (q_ref, k_ref, v_ref, s_ref, o_ref) in enumerate(dirs):
        incl = _chunk_masks(d)[0]
        last = CHUNK - 1 if d == 0 else 0
        for slot in range(SCAN_CHUNKS):
            rs = row_slice(d, slot)
            bcum = bcums[d, slot]
            q_all, k_all, v_all = q_ref[0, rs, :].astype(F32), k_ref[0, rs, :].astype(F32), v_ref[0, rs, :]
            for h in range(GLA_HEADS):
                ks = slice(h * GLA_DK, (h + 1) * GLA_DK)
                vs = slice(h * GLA_DV, (h + 1) * GLA_DV)
                b = bcum[:, ks]
                bl = bcum[last:last + 1, ks]
                k = k_all[:, ks]
                ch = dict(d=d, h=h, vs=vs, rs=rs, o_ref=o_ref, bl=bl, v=v_all[:, vs], incl=incl)
                ch["q_dec"] = q_all[:, ks] * (GLA_DK ** -0.5) * jnp.exp(b)
                ch["k_neg"] = k * jnp.exp(-b)
                ch["k_dec"] = k * jnp.exp(bl - b)
                slots[slot].append(ch)
    chains = [ch for slot in slots for ch in slot]
    for ch in chains:
        ch["scores"] = jnp.where(ch["incl"], _bdot_nt(ch["q_dec"], ch["k_neg"]), 0.0)
    for ch in chains:
        ch["intra"] = _bdot(ch["scores"], ch["v"])
        ch["kv"] = _bdot_tn(ch["v"], ch["k_dec"])
    state = {(d, h): st_ref[d, h] for d in range(N_DIR) for h in range(GLA_HEADS)}
    for slot in slots:
        for ch in slot:
            key = ch["d"], ch["h"]
            out = ch["intra"] + _bdot_nt(ch["q_dec"], state[key])
            ch["o_ref"][0, ch["rs"], ch["vs"]] = out.astype(ch["o_ref"].dtype)
            state[key] = jnp.exp(ch["bl"]) * state[key] + ch["kv"]
    for (d, h), st in state.items():
        st_ref[d, h] = st


def gla_scan(p, small, w2, bias, n_ctx):
    bsz, s, _ = p.shape
    rows_per_step = SCAN_CHUNKS * CHUNK
    n_ctx_steps, n_steps = n_ctx // SCAN_CHUNKS, s // rows_per_step
    assert n_ctx % SCAN_CHUNKS == 0 and s % rows_per_step == 0

    def specs(rev):
        def row(t):
            return _rev_chunk(t, n_ctx_steps, n_steps) if rev else t
        return [pl.BlockSpec((1, rows_per_step, GLA_QK), lambda b, t: (b, row(t), 0)),
                pl.BlockSpec((1, rows_per_step, GLA_QK), lambda b, t: (b, row(t), 1)),
                pl.BlockSpec((1, rows_per_step, GLA_V), lambda b, t: (b, row(t), 2 * GLA_QK // GLA_V)),
                pl.BlockSpec((1, rows_per_step, SMALL_WIDTH), lambda b, t: (b, row(t), 0))]

    out_f = pl.BlockSpec((1, rows_per_step, GLA_V), lambda b, t: (b, t, 0))
    out_r = pl.BlockSpec((1, rows_per_step, GLA_V),
                         lambda b, t: (b, _rev_chunk(t, n_ctx_steps, n_steps), 0))
    return pl.pallas_call(
        _gla_kernel,
        grid=(bsz, n_steps),
        in_specs=specs(False) + specs(True) + [
            pl.BlockSpec((N_DIR, GLA_RANK, GLA_QK), lambda b, t: (0, 0, 0)),
            pl.BlockSpec((N_DIR, 1, GLA_QK), lambda b, t: (0, 0, 0))],
        out_specs=[out_f, out_r],
        out_shape=[jax.ShapeDtypeStruct((bsz, s, GLA_V), BF16)] * 2,
        scratch_shapes=[pltpu.VMEM((N_DIR, GLA_HEADS, GLA_DV, GLA_DK), F32)],
        compiler_params=_params("parallel", "arbitrary"),
        name="gla_scan",
    )(p, p, p, small, p, p, p, small, w2, bias.reshape(N_DIR, 1, GLA_QK))


def _gdn_kernel(qf, kf, vf, sf, qr, kr, vr, sr, alog_ref, dtb_ref, of, orr, s_ref):
    @pl.when(pl.program_id(1) == 0)
    def _():
        s_ref[...] = jnp.zeros_like(s_ref)

    ri = lax.broadcasted_iota(jnp.int32, (CHUNK, CHUNK), 0)
    ci = lax.broadcasted_iota(jnp.int32, (CHUNK, CHUNK), 1)
    eye = (ri == ci).astype(F32)
    same16 = (ri >> 4) == (ci >> 4)
    same32 = (ri >> 5) == (ci >> 5)
    nh = GDN_HEADS
    slots = [[] for _ in range(SCAN_CHUNKS)]
    for d, (q_ref, k_ref, v_ref, sm_ref, o_ref) in enumerate(
            ((qf, kf, vf, sf, of), (qr, kr, vr, sr, orr))):
        incl, strict, tri, tri_t = _chunk_masks(d)
        for slot in range(SCAN_CHUNKS):
            sub = slot if d == 0 else SCAN_CHUNKS - 1 - slot
            rs = slice(sub * CHUNK, (sub + 1) * CHUNK)
            sm = sm_ref[0, rs, :]
            g_all = -jnp.exp(alog_ref[...]) * _softplus(sm + dtb_ref[...])
            beta_all = _sigmoid(sm)
            gcol = g_all[:, A_COL:A_COL + N_DIR * nh]
            grow = g_all.T[A_COL:A_COL + N_DIR * nh, :]
            bcol = jnp.dot(tri, gcol, precision=HIGHEST, preferred_element_type=F32)
            brow = jnp.dot(grow, tri_t, precision=HIGHEST, preferred_element_type=F32)
            last = CHUNK - 1 if d == 0 else 0
            btot = bcol[last:last + 1, :]
            q_all = q_ref[0, rs, :].astype(F32)
            k_all = k_ref[0, rs, :].astype(F32)
            v_all = v_ref[0, rs, :].astype(F32)
            for h in range(nh):
                c = d * nh + h
                hs = slice(h * GDN_DK, (h + 1) * GDN_DK)
                ch = dict(d=d, h=h, hs=hs, rs=rs, o_ref=o_ref, incl=incl, strict=strict)
                ch["bc"] = bcol[:, c:c + 1]
                ch["br"] = brow[c:c + 1, :]
                ch["bl"] = btot[:, c:c + 1]
                ch["beta"] = beta_all[:, BT_COL + c:BT_COL + c + 1]
                ch["q"], ch["k"], ch["v"] = q_all[:, hs], k_all[:, hs], v_all[:, hs]
                slots[slot].append(ch)

    def stage_scores(chains):
        for ch in chains:
            incl = ch["incl"]
            ch["decay"] = jnp.where(incl, jnp.exp(jnp.where(incl, ch["bc"] - ch["br"], 0.0)), 0.0)
            ch["kb"] = ch["k"] * ch["beta"]
            ch["a2"] = _bdot_nt(jnp.concatenate([ch["kb"], ch["q"]], axis=0), ch["k"])

    def stage_split(chains):
        for ch in chains:
            a2 = ch.pop("a2")
            n_mat = -jnp.where(ch["strict"], a2[:CHUNK] * ch["decay"], 0.0)
            ch["attn"] = a2[CHUNK:] * ch["decay"]
            n_diag = jnp.where(same16, n_mat, 0.0)
            ch["n32"] = jnp.where(same32, n_mat, 0.0) - n_diag
            ch["n64"] = jnp.where(same32, 0.0, n_mat)
            ch["p"] = eye + n_diag
            ch["n"] = _bdot(n_diag, n_diag)

    def stage_diag(square):
        def run(chains):
            for ch in chains:
                ch["p"] = ch["p"] + _bdot(ch["p"], ch["n"])
                if square:
                    ch["n"] = _bdot(ch["n"], ch["n"])
        return run

    def stage_merge_a(off):
        def run(chains):
            for ch in chains:
                ch["x"] = _bdot(ch["p"], ch[off])
        return run

    def stage_merge_b(chains):
        for ch in chains:
            ch["p"] = ch["p"] + _bdot(ch.pop("x"), ch["p"])

    def stage_uw(chains):
        for ch in chains:
            ch["e_b"] = jnp.exp(ch["bc"])
            uw = _bdot(ch["p"], jnp.concatenate([ch["v"] * ch["beta"], ch["kb"] * ch["e_b"]], axis=1))
            ch["u"], ch["w"] = uw[:, :GDN_DV], uw[:, GDN_DV:]

    state = {(d, h): s_ref[d, h] for d in range(N_DIR) for h in range(nh)}

    def tail_read(chains):
        for ch in chains:
            s = state[ch["d"], ch["h"]]
            ch["wq"] = _bdot(jnp.concatenate([ch["w"], ch["q"] * ch["e_b"]], axis=0), s)

    def tail_update(chains):
        for ch in chains:
            wq = ch.pop("wq")
            v_new = ch["u"] - wq[:CHUNK]
            ch["o_ref"][0, ch["rs"], ch["hs"]] = (
                wq[CHUNK:] + _bdot(ch["attn"], v_new)).astype(ch["o_ref"].dtype)
            k_dec = ch["k"] * jnp.exp(ch["bl"] - ch["bc"])
            key = ch["d"], ch["h"]
            state[key] = jnp.exp(ch["bl"]) * state[key] + _bdot_tn(k_dec, v_new)

    stages = [stage_scores, stage_split, stage_diag(True), stage_diag(True), stage_diag(False),
              stage_merge_a("n32"), stage_merge_b, stage_merge_a("n64"), stage_merge_b, stage_uw]
    for stage in stages:
        stage(slots[0])
    for slot in range(1, SCAN_CHUNKS):
        for i, stage in enumerate(stages):
            stage(slots[slot])
            if i == 1:
                tail_read(slots[slot - 1])
            elif i == 4:
                tail_update(slots[slot - 1])
    tail_read(slots[-1])
    tail_update(slots[-1])
    for (d, h), s in state.items():
        s_ref[d, h] = s


def gdn_scan(qkv, small, a_log, dt_bias, n_ctx):
    bsz, s, _ = qkv.shape
    rows_per_step = SCAN_CHUNKS * CHUNK
    n_ctx_steps, n_steps = n_ctx // SCAN_CHUNKS, s // rows_per_step
    assert n_ctx % SCAN_CHUNKS == 0 and s % rows_per_step == 0
    lane_row = lambda t: jnp.zeros((1, SMALL_WIDTH), F32).at[0, A_COL:A_COL + N_DIR * GDN_HEADS].set(
        t.astype(F32).reshape(-1))

    def specs(rev):
        def row(t):
            return _rev_chunk(t, n_ctx_steps, n_steps) if rev else t
        return [pl.BlockSpec((1, rows_per_step, GDN_QK), lambda b, t: (b, row(t), 0)),
                pl.BlockSpec((1, rows_per_step, GDN_QK), lambda b, t: (b, row(t), 1)),
                pl.BlockSpec((1, rows_per_step, GDN_V), lambda b, t: (b, row(t), 2)),
                pl.BlockSpec((1, rows_per_step, SMALL_WIDTH), lambda b, t: (b, row(t), 0))]

    const = pl.BlockSpec((1, SMALL_WIDTH), lambda b, t: (0, 0))
    out_f = pl.BlockSpec((1, rows_per_step, GDN_V), lambda b, t: (b, t, 0))
    out_r = pl.BlockSpec((1, rows_per_step, GDN_V),
                         lambda b, t: (b, _rev_chunk(t, n_ctx_steps, n_steps), 0))
    return pl.pallas_call(
        _gdn_kernel,
        grid=(bsz, n_steps),
        in_specs=specs(False) + specs(True) + [const, const],
        out_specs=[out_f, out_r],
        out_shape=[jax.ShapeDtypeStruct((bsz, s, GDN_V), BF16)] * 2,
        scratch_shapes=[pltpu.VMEM((N_DIR, GDN_HEADS, GDN_DK, GDN_DV), F32)],
        compiler_params=_params("parallel", "arbitrary"),
        name="gdn_scan",
    )(qkv, qkv, qkv, small, qkv, qkv, qkv, small, lane_row(a_log), lane_row(dt_bias))


FFN_COL_TILE = D_FF // 2


FFN_EXT_ROWS = ROW_TILE + 2 * GRID_W
FFN_SUB_COLS = 256


def _ffn_gate_kernel(a_ref, up_ref, dn_ref, v_ref, w_ref, o_ref, shift_ref, *, rows):
    g, n = GRID_W, FFN_EXT_ROWS

    @pl.when((pl.program_id(0) == 0) & (pl.program_id(1) == 0))
    def _():
        ri = lax.broadcasted_iota(jnp.int32, (n, n), 0)
        ci = lax.broadcasted_iota(jnp.int32, (n, n), 1)
        for kind in range(2):
            col = (ri & (g - 1)) if kind == 0 else ri - g
            last_col = g - 1 if kind == 0 else ROW_TILE - 1
            shift_ref[kind, 0] = jnp.where((ci == ri - 1) & (col != 0), 1.0, 0.0).astype(BF16)
            shift_ref[kind, 1] = jnp.where((ci == ri + 1) & (col != last_col), 1.0, 0.0).astype(BF16)

    is_ctx, first, last = rows.flags(pl.program_id(0))
    kind = jnp.where(is_ctx, 1, 0)
    halo_zero = jnp.zeros(up_ref.shape, up_ref.dtype)
    up = jnp.where(first, halo_zero, up_ref[...])
    dn = jnp.where(last, halo_zero, dn_ref[...])
    ext_all = jnp.concatenate([up, a_ref[...], dn], axis=0)
    vert = jnp.where(is_ctx, 0.0, 1.0)
    for c0 in range(0, FFN_COL_TILE, FFN_SUB_COLS):
        cs = slice(c0, c0 + FFN_SUB_COLS)
        ext = ext_all[:, cs]
        shifted = (jnp.dot(shift_ref[kind, 0], ext, preferred_element_type=F32),
                   ext.astype(F32),
                   jnp.dot(shift_ref[kind, 1], ext, preferred_element_type=F32))
        acc = None
        for dr in (-1, 0, 1):
            for dc in (-1, 0, 1):
                val = shifted[dc + 1][g + dr * g:g + dr * g + ROW_TILE]
                wt = w_ref[3 * (dr + 1) + dc + 1:3 * (dr + 1) + dc + 2, cs]
                if dr != 0:
                    wt = wt * vert
                acc = val * wt if acc is None else acc + val * wt
        o_ref[:, cs] = (acc * _sigmoid(acc) * v_ref[:, cs].astype(F32)).astype(o_ref.dtype)


def ffn_gate(up, rows, w_conv):
    m = up.shape[0]
    nc = D_FF // FFN_COL_TILE
    gb = ROW_TILE // GRID_W
    n_gblocks = m // GRID_W
    return pl.pallas_call(
        functools.partial(_ffn_gate_kernel, rows=rows),
        grid=(rows.n_tiles, nc),
        in_specs=[pl.BlockSpec((ROW_TILE, FFN_COL_TILE), lambda i, j: (i, j)),
                  pl.BlockSpec((GRID_W, FFN_COL_TILE), lambda i, j: (jnp.maximum(i * gb - 1, 0), j)),
                  pl.BlockSpec((GRID_W, FFN_COL_TILE),
                               lambda i, j: (jnp.minimum((i + 1) * gb, n_gblocks - 1), j)),
                  pl.BlockSpec((ROW_TILE, FFN_COL_TILE), lambda i, j: (i, nc + j)),
                  pl.BlockSpec((9, FFN_COL_TILE), lambda i, j: (0, j))],
        out_specs=pl.BlockSpec((ROW_TILE, FFN_COL_TILE), lambda i, j: (i, j)),
        out_shape=jax.ShapeDtypeStruct((m, D_FF), BF16),
        scratch_shapes=[pltpu.VMEM((2, 2, FFN_EXT_ROWS, FFN_EXT_ROWS), BF16)],
        compiler_params=_params("arbitrary", "arbitrary"), name="ffn_gate",
    )(up, up, up, up, w_conv.reshape(9, D_FF))


FUSED_ROWS = 3 * ROW_TILE
FUSED_COLS = 512
FUSED_SUB_COLS = 256
FUSED_PAD = 8


def _ffn_up_gate_kernel(x_ref, xu_ref, xd_ref, wa_ref, wv_ref, wc_ref, o_ref, ext_ref, *, seq_rows):
    g, tm, pad = GRID_W, FUSED_ROWS, FUSED_PAD
    n = tm + 2 * g
    x, xu, xd = x_ref[...], xu_ref[...], xd_ref[...]
    r0 = (pl.program_id(0) % (seq_rows // tm)) * tm
    rbe = r0 - g + lax.broadcasted_iota(jnp.int32, (n, 1), 0)
    ctx_e = (rbe >= 0) & (rbe < ROW_TILE)
    col_e = jnp.where(ctx_e, rbe, (rbe - ROW_TILE) & (g - 1))
    keep_left = (col_e != jnp.where(ctx_e, ROW_TILE - 1, g - 1)).astype(F32)
    keep_right = (col_e != 0).astype(F32)
    rb = r0 + lax.broadcasted_iota(jnp.int32, (tm, 1), 0)
    above_ok = (rb >= ROW_TILE + g).astype(F32)
    below_ok = ((rb >= ROW_TILE) & (rb < seq_rows - g)).astype(F32)
    zeros = jnp.zeros((2 * pad, FUSED_COLS), F32)
    for k in (0, 2):
        ext_ref[k, 0:2 * pad] = zeros
        ext_ref[k, n:n + 2 * pad] = zeros
    subs = [slice(c0, c0 + FUSED_SUB_COLS) for c0 in range(0, FUSED_COLS, FUSED_SUB_COLS)]
    for cs in subs:
        wa = wa_ref[:, cs]
        a_ext = jnp.concatenate([jnp.dot(xu, wa, preferred_element_type=F32),
                                 jnp.dot(x, wa, preferred_element_type=F32),
                                 jnp.dot(xd, wa, preferred_element_type=F32)], axis=0)
        ext_ref[0, pl.ds(pad + 1, n), cs] = a_ext * keep_left
        ext_ref[1, pl.ds(pad, n), cs] = a_ext
        ext_ref[2, pl.ds(pad - 1, n), cs] = a_ext * keep_right
    for cs in subs:
        v = jnp.dot(x, wv_ref[:, cs], preferred_element_type=F32)
        rows = []
        for dr in (-1, 0, 1):
            part = None
            for dc in (-1, 0, 1):
                tap = 3 * (dr + 1) + dc + 1
                term = ext_ref[dc + 1, pl.ds(pad + g + dr * g, tm), cs] * wc_ref[tap:tap + 1, cs]
                part = term if part is None else part + term
            rows.append(part)
        acc = rows[1] + above_ok * rows[0] + below_ok * rows[2]
        o_ref[:, cs] = (acc * _sigmoid(acc) * v).astype(o_ref.dtype)


def ffn_up_gate(h, w_up, w_conv, seq_rows):
    m, d = h.shape
    assert m % FUSED_ROWS == 0 and seq_rows % FUSED_ROWS == 0 and D_FF % FUSED_COLS == 0
    nc = D_FF // FUSED_COLS
    gb = FUSED_ROWS // GRID_W
    n_gblocks = m // GRID_W
    return pl.pallas_call(
        functools.partial(_ffn_up_gate_kernel, seq_rows=seq_rows),
        grid=(m // FUSED_ROWS, nc),
        in_specs=[pl.BlockSpec((FUSED_ROWS, d), lambda i, j: (i, 0)),
                  pl.BlockSpec((GRID_W, d), lambda i, j: (jnp.maximum(i * gb - 1, 0), 0)),
                  pl.BlockSpec((GRID_W, d), lambda i, j: (jnp.minimum((i + 1) * gb, n_gblocks - 1), 0)),
                  pl.BlockSpec((d, FUSED_COLS), lambda i, j: (0, j)),
                  pl.BlockSpec((d, FUSED_COLS), lambda i, j: (0, nc + j)),
                  pl.BlockSpec((9, FUSED_COLS), lambda i, j: (0, j))],
        out_specs=pl.BlockSpec((FUSED_ROWS, FUSED_COLS), lambda i, j: (i, j)),
        out_shape=jax.ShapeDtypeStruct((m, D_FF), BF16),
        scratch_shapes=[pltpu.VMEM((3, FUSED_ROWS + 2 * GRID_W + 2 * FUSED_PAD, FUSED_COLS), F32)],
        compiler_params=_params("parallel", "arbitrary"), name="ffn_up_gate",
    )(h, h, h, w_up, w_up, w_conv.reshape(9, D_FF))


BRANCH_COLS = GLA_V
assert GDN_V == BRANCH_COLS and POOL_WIDTH == BRANCH_COLS
BRANCH_WIDTH = N_BRANCH * BRANCH_COLS
Z_GLA_COL = 2 * GLA_QK + GLA_V
Z_GDN_COL = GDN_QKV_COL + GDN_QKV
POOL_COL = Z_GDN_COL + GDN_V
GATES_COL = POOL_COL + POOL_WIDTH
POOL_HALO = BF16_SUBLANES
assert POOL_HALO >= max(POOL_WINDOWS) // 2


def _headnorm_gate(o, z, nw, n_heads, width, o_ref, col0):
    for h in range(n_heads):
        sl = slice(h * width, (h + 1) * width)
        oh, zh = o[:, sl], z[:, sl]
        inv = lax.rsqrt(jnp.mean(oh * oh, axis=-1, keepdims=True) + EPS)
        o_ref[:, col0 + h * width:col0 + (h + 1) * width] = (
            oh * inv * nw * (zh * _sigmoid(zh))).astype(o_ref.dtype)


def _merge_prep_kernel(gf_ref, gr_ref, df_ref, dr_ref, za_ref, zb_ref, u_ref, uup_ref, udn_ref,
                       gnw_ref, dnw_ref, pw_ref, ps_ref, o_ref, win_ref, *, rows):
    halo, n = POOL_HALO, ROW_TILE + 2 * POOL_HALO

    @pl.when(pl.program_id(0) == 0)
    def _():
        ri = lax.broadcasted_iota(jnp.int32, (ROW_TILE, n), 0)
        ci = lax.broadcasted_iota(jnp.int32, (ROW_TILE, n), 1) - halo
        for gi, win in enumerate(POOL_WINDOWS):
            lo = ri - win // 2
            win_ref[gi] = jnp.where((ci >= lo) & (ci < lo + win), 1.0, 0.0).astype(BF16)

    _, first, last = rows.flags(pl.program_id(0))
    _headnorm_gate(gf_ref[...].astype(F32) + gr_ref[...].astype(F32), za_ref[...].astype(F32),
                   gnw_ref[...], GLA_HEADS, GLA_DV, o_ref, 0)
    _headnorm_gate(df_ref[...].astype(F32) + dr_ref[...].astype(F32), zb_ref[...].astype(F32),
                   dnw_ref[...], GDN_HEADS, GDN_DV, o_ref, GLA_V)
    halo_zero = jnp.zeros(uup_ref.shape, uup_ref.dtype)
    u = u_ref[...]
    ext = jnp.concatenate([jnp.where(first, halo_zero, uup_ref[...]), u,
                           jnp.where(last, halo_zero, udn_ref[...])], axis=0)
    t = lax.broadcasted_iota(jnp.int32, (ROW_TILE, 1), 0)
    for gi, win in enumerate(POOL_WINDOWS):
        sl = slice(gi * POOL_GROUP, (gi + 1) * POOL_GROUP)
        half = win // 2
        win_sum = jnp.dot(win_ref[gi], ext[:, sl], preferred_element_type=F32)
        lo_clip = jnp.where(first, jnp.maximum(half - t, 0), 0)
        hi_clip = jnp.where(last, jnp.maximum(t - half + win - ROW_TILE, 0), 0)
        cnt = (win - lo_clip - hi_clip).astype(F32)
        pg = win_sum / cnt - u[:, sl].astype(F32)
        yp = _bdot(pg, pw_ref[gi]) * ps_ref[:, sl]
        o_ref[:, GLA_V + GDN_V + gi * POOL_GROUP:GLA_V + GDN_V + (gi + 1) * POOL_GROUP] = yp.astype(o_ref.dtype)


def merge_prep(p, o_gla, o_gdn, p_rows, rows, gla_nw, gdn_nw, pool_w, pool_scale):
    hb = ROW_TILE // BF16_SUBLANES
    n_hblocks = p.shape[0] // BF16_SUBLANES
    blk = lambda i: rows.block_of(i, p_rows)
    wide = lambda col: pl.BlockSpec((ROW_TILE, BRANCH_COLS), lambda i: (blk(i), col // BRANCH_COLS))
    const = lambda shape: pl.BlockSpec(shape, lambda i: (0,) * len(shape))
    pc = POOL_COL // POOL_WIDTH
    return pl.pallas_call(
        functools.partial(_merge_prep_kernel, rows=rows),
        grid=(rows.n_tiles,),
        in_specs=[wide(0), wide(0), wide(0), wide(0), wide(Z_GLA_COL), wide(Z_GDN_COL), wide(POOL_COL),
                  pl.BlockSpec((POOL_HALO, POOL_WIDTH), lambda i: (jnp.maximum(blk(i) * hb - 1, 0), pc)),
                  pl.BlockSpec((POOL_HALO, POOL_WIDTH),
                               lambda i: (jnp.minimum((blk(i) + 1) * hb, n_hblocks - 1), pc)),
                  const((1, GLA_DV)), const((1, GDN_DV)),
                  const((POOL_GROUPS, POOL_GROUP, POOL_GROUP)), const((1, POOL_WIDTH))],
        out_specs=pl.BlockSpec((ROW_TILE, BRANCH_WIDTH), lambda i: (i, 0)),
        out_shape=jax.ShapeDtypeStruct((rows.n_tiles * ROW_TILE, BRANCH_WIDTH), BF16),
        scratch_shapes=[pltpu.VMEM((POOL_GROUPS, ROW_TILE, ROW_TILE + 2 * POOL_HALO), BF16)],
        compiler_params=_params("arbitrary"), name="merge_prep",
    )(o_gla[0], o_gla[1], o_gdn[0], o_gdn[1], p, p, p, p, p,
      gla_nw.reshape(1, GLA_DV), gdn_nw.reshape(1, GDN_DV), pool_w.astype(BF16),
      pool_scale.reshape(1, POOL_WIDTH))


def _branch_kernel(ya, yb, yc, ga, gb, gc, wa, wb, wc, o_ref):
    acc = None
    for y_ref, g_ref, w_ref in ((ya, ga, wa), (yb, gb, wb), (yc, gc, wc)):
        term = _sigmoid(g_ref[...].astype(F32)) * jnp.dot(y_ref[...], w_ref[...], preferred_element_type=F32)
        acc = term if acc is None else acc + term
    o_ref[...] = acc.astype(o_ref.dtype)


def branch_merge(ycat, p, p_rows, rows, w_gla, w_gdn, w_pool):
    blk = lambda i: rows.block_of(i, p_rows)
    ysp = lambda k: pl.BlockSpec((ROW_TILE, BRANCH_COLS), lambda i: (i, k))
    gsp = lambda k: pl.BlockSpec((ROW_TILE, D_MODEL), lambda i: (blk(i), GATES_COL // D_MODEL + k))
    wsp = pl.BlockSpec((BRANCH_COLS, D_MODEL), lambda i: (0, 0))
    return pl.pallas_call(
        _branch_kernel,
        grid=(rows.n_tiles,),
        in_specs=[ysp(0), ysp(1), ysp(2), gsp(0), gsp(1), gsp(2), wsp, wsp, wsp],
        out_specs=pl.BlockSpec((ROW_TILE, D_MODEL), lambda i: (i, 0)),
        out_shape=jax.ShapeDtypeStruct((rows.n_tiles * ROW_TILE, D_MODEL), BF16),
        compiler_params=_params("parallel"), name="branch_merge",
    )(ycat, ycat, ycat, p, p, p, w_gla, w_gdn, w_pool)


def permute_w_in(w):
    q, k, v, zg, lr, gqkv, ga, gbt, gz, pool, gates = _split_cols(w, IN_SPLITS)
    main = jnp.concatenate([q, k, v, zg, gqkv, gz, pool, gates], axis=1)
    small = jnp.concatenate([lr, ga, gbt, jnp.zeros((w.shape[0], SMALL_WIDTH - SMALL_COLS), w.dtype)], axis=1)
    return main.astype(BF16), small.astype(BF16)


def hybrid_mixer(h, rows, out_rows, bsz, w_main, w_small, gla_w2, gla_b, gla_nw, gdn_cw, gdn_alog,
                 gdn_dtb, gdn_nw, pool_w, pool_scale, w_br_gla, w_br_gdn, w_br_pool, w_out):
    m = h.shape[0]
    s = m // bsz
    p = pmm(h, w_main, BF16)
    small = pmm(h, w_small)
    p3, small3 = p.reshape(bsz, s, MAIN_WIDTH), small.reshape(bsz, s, SMALL_WIDTH)
    n_ctx = (ROW_TILE if rows.has_ctx else 0) // CHUNK
    o_gla = [t.reshape(m, GLA_V) for t in gla_scan(p3, small3, gla_w2.astype(BF16), gla_b, n_ctx)]
    qkv = gdn_prep(p, rows, gdn_cw).reshape(bsz, s, GDN_QKV)
    o_gdn = [t.reshape(m, GDN_V) for t in gdn_scan(qkv, small3, gdn_alog, gdn_dtb, n_ctx)]
    ycat = merge_prep(p, o_gla, o_gdn, rows, out_rows, gla_nw, gdn_nw, pool_w, pool_scale)
    mrg = branch_merge(ycat, p, rows, out_rows, w_br_gla, w_br_gdn, w_br_pool)
    return pmm(mrg, w_out, BF16)


def kernel(x, c, ctx, c_ctx, ada_w, ada_b, norm1_w, norm2_w, w_in, gla_lr_w2, gla_lr_b,
           gla_norm_w, gdn_conv_w, gdn_a_log, gdn_dt_bias, gdn_norm_w, pool_w, pool_scale,
           w_br_gla, w_br_gdn, w_br_pool, w_out, ffn_up, ffn_conv, ffn_down, final_norm_w):
    B, T, _ = x.shape
    assert ctx.shape[1] == ROW_TILE and T % ROW_TILE == 0 and (T // GRID_W) % (ROW_TILE // GRID_W) == 0
    s_all = ROW_TILE + T
    all_rows = Rows(B, True, T // ROW_TILE)
    lat_rows = Rows(B, False, T // ROW_TILE)
    xa = jnp.concatenate([ctx, x], axis=1).reshape(B * s_all, D_MODEL)
    n_cond = -(-(B + 1) // BF16_SUBLANES) * BF16_SUBLANES
    cond = jnp.concatenate([c, c_ctx[None, :], jnp.zeros((n_cond - B - 1, D_MODEL), F32)], axis=0)
    cond = jax.nn.silu(cond)
    mods = [(pmm(cond, ada_w[l]) + ada_b[l]).reshape(n_cond, N_MOD, D_MODEL) for l in range(DEPTH)]
    _, h = resid_norm(xa, all_rows, all_rows, norm1_w[0], mod=mods[0], shift_idx=0, scale_idx=1)
    x_rows = all_rows
    for l in range(DEPTH):
        last = l == DEPTH - 1
        rows = lat_rows if last else all_rows
        w_main, w_small = permute_w_in(w_in[l])
        y = hybrid_mixer(h, x_rows, rows, B, w_main, w_small, gla_lr_w2[l], gla_lr_b[l], gla_norm_w[l],
                         gdn_conv_w[l], gdn_a_log[l], gdn_dt_bias[l], gdn_norm_w[l],
                         pool_w[l], pool_scale[l], w_br_gla[l].astype(BF16),
                         w_br_gdn[l].astype(BF16), w_br_pool[l].astype(BF16), w_out[l].astype(BF16))
        xa, h2 = resid_norm(xa, x_rows, rows, norm2_w[l], resid=y, gate_mod=mods[l], gate_idx=2,
                            mod=mods[l], shift_idx=3, scale_idx=4)
        x_rows = rows
        if rows.has_ctx:
            gated = ffn_up_gate(h2, ffn_up[l].astype(BF16), ffn_conv[l], s_all)
        else:
            gated = ffn_gate(pmm(h2, ffn_up[l].astype(BF16), BF16), rows, ffn_conv[l])
        dn = pmm(gated, ffn_down[l].astype(BF16), BF16)
        if last:
            _, out = resid_norm(xa, x_rows, rows, final_norm_w, resid=dn, gate_mod=mods[l], gate_idx=5,
                                out_dtype=F32)
            return out.reshape(B, T, D_MODEL)
        xa, h = resid_norm(xa, x_rows, rows, norm1_w[l + 1], resid=dn, gate_mod=mods[l], gate_idx=5,
                           mod=mods[l + 1], shift_idx=0, scale_idx=1)
```

```python
import functools

import jax
import jax.numpy as jnp
from jax import lax
from jax.experimental import pallas as pl
from jax.experimental.pallas import tpu as pltpu

D_MODEL = 2048
DEPTH = 4
GRID_W = 64
CHUNK = 64
N_DIR = 2
N_BRANCH = 3
N_MOD = 6
EPS = 1e-6

GLA_HEADS = 4
GLA_DK = 128
GLA_DV = 256
GLA_RANK = 16
GLA_TAU = 16.0

GDN_HEADS = 8
GDN_DK = 128
GDN_DV = 128

POOL_WINDOWS = (2, 4, 8, 16)
POOL_GROUPS = 4
POOL_GROUP = 256

D_FF = 5632

GLA_QK = GLA_HEADS * GLA_DK
GLA_V = GLA_HEADS * GLA_DV
GDN_QK = GDN_HEADS * GDN_DK
GDN_V = GDN_HEADS * GDN_DV
GDN_QKV = 2 * GDN_QK + GDN_V
POOL_WIDTH = POOL_GROUPS * POOL_GROUP
IN_SPLITS = (GLA_QK, GLA_QK, GLA_V, GLA_V, N_DIR * GLA_RANK,
             GDN_QKV, N_DIR * GDN_HEADS, N_DIR * GDN_HEADS, GDN_V,
             POOL_WIDTH, N_BRANCH * D_MODEL)
F32 = jnp.float32
BF16 = jnp.bfloat16
HIGHEST = lax.Precision.HIGHEST

V7X_VMEM_BYTES = 64 * 1024 * 1024
VMEM_LIMIT_BYTES = V7X_VMEM_BYTES * 3 // 4
MM_TILE_BUDGET_BYTES = V7X_VMEM_BYTES * 5 // 8
LANES = 128
BF16_SUBLANES = 16

SMALL_WIDTH = LANES
SMALL_COLS = N_DIR * GLA_RANK + 2 * N_DIR * GDN_HEADS
A_COL = N_DIR * GLA_RANK
BT_COL = A_COL + N_DIR * GDN_HEADS
MAIN_SPLITS = (GLA_QK, GLA_QK, GLA_V, GLA_V, GDN_QKV, GDN_V, POOL_WIDTH, N_BRANCH * D_MODEL)
MAIN_WIDTH = sum(MAIN_SPLITS)
GDN_QKV_COL = 2 * GLA_QK + 2 * GLA_V

ROW_TILE = 256


def _split_cols(t, sizes):
    parts, start = [], 0
    for size in sizes:
        parts.append(t[..., start:start + size])
        start += size
    return parts


def _params(*sem):
    return pltpu.CompilerParams(dimension_semantics=sem, vmem_limit_bytes=VMEM_LIMIT_BYTES)


def _sigmoid(x):
    return 1.0 / (1.0 + jnp.exp(-x))


def _softplus(x):
    return jnp.maximum(x, 0.0) + jnp.log(1.0 + jnp.exp(-jnp.abs(x)))


def _mm_kernel(x_ref, w_ref, o_ref):
    o_ref[...] = jnp.dot(x_ref[...].astype(BF16), w_ref[...].astype(BF16),
                         preferred_element_type=F32).astype(o_ref.dtype)


def _mm_tiles(m, k, n, x_bytes, w_bytes, o_bytes):
    best = None
    for tm in (1024, 512, 256, 128, 64, 32, 16, 8):
        if m % tm:
            continue
        for tn in (2048, 1024, 512, 256, 128):
            if n % tn:
                continue
            need = 2 * (tm * k * x_bytes + k * tn * w_bytes + tm * tn * o_bytes)
            if need > MM_TILE_BUDGET_BYTES:
                continue
            score = tm * tn / (tm + tn)
            if best is None or score > best[0]:
                best = (score, tm, tn)
    if best is None:
        raise ValueError(f"no matmul tiling for {(m, k, n)}")
    return best[1], best[2]


def pmm(x, w, out_dtype=F32):
    m, k = x.shape
    n = w.shape[1]
    tm, tn = _mm_tiles(m, k, n, x.dtype.itemsize, w.dtype.itemsize, jnp.dtype(out_dtype).itemsize)
    return pl.pallas_call(
        _mm_kernel,
        grid=(m // tm, n // tn),
        in_specs=[pl.BlockSpec((tm, k), lambda i, j: (i, 0)),
                  pl.BlockSpec((k, tn), lambda i, j: (0, j))],
        out_specs=pl.BlockSpec((tm, tn), lambda i, j: (i, j)),
        out_shape=jax.ShapeDtypeStruct((m, n), out_dtype),
        compiler_params=_params("parallel", "arbitrary"),
        name="mm",
    )(x, w)


class Rows:
    def __init__(self, batch, has_ctx, lat_tiles):
        self.batch, self.has_ctx, self.lat_tiles = batch, has_ctx, lat_tiles
        self.per_batch = lat_tiles + (1 if has_ctx else 0)
        self.n_tiles = batch * self.per_batch

    def block_of(self, i, other):
        if other.has_ctx == self.has_ctx:
            return i
        assert other.has_ctx and not self.has_ctx
        return (i // self.per_batch) * other.per_batch + 1 + i % self.per_batch

    def mod_row(self, i):
        b = i // self.per_batch
        if not self.has_ctx:
            return b
        return jnp.where(i % self.per_batch == 0, self.batch, b)

    def flags(self, i):
        r = i % self.per_batch
        if not self.has_ctx:
            return False, r == 0, r == self.per_batch - 1
        is_ctx = r == 0
        return is_ctx, is_ctx | (r == 1), is_ctx | (r == self.per_batch - 1)


def _norm_kernel(*refs, has_resid, gate_idx, has_mod, shift_idx, scale_idx, write_x):
    refs = list(refs)
    x_ref = refs.pop(0)
    x = x_ref[...]
    if has_resid:
        y_ref, gmod_ref = refs.pop(0), refs.pop(0)
        x = x + gmod_ref[0, gate_idx:gate_idx + 1, :] * y_ref[...].astype(F32)
    nw_ref = refs.pop(0)
    mod_ref = refs.pop(0) if has_mod else None
    if write_x:
        refs.pop(0)[...] = x
    h_ref = refs.pop(0)
    y = x * lax.rsqrt(jnp.mean(x * x, axis=-1, keepdims=True) + EPS) * nw_ref[...]
    if has_mod:
        y = y * (1.0 + mod_ref[0, scale_idx:scale_idx + 1, :]) + mod_ref[0, shift_idx:shift_idx + 1, :]
    h_ref[...] = y.astype(h_ref.dtype)


def resid_norm(x, x_rows, rows, norm_w, *, resid=None, gate_mod=None, gate_idx=0,
               mod=None, shift_idx=0, scale_idx=1, out_dtype=BF16):
    d = x.shape[-1]
    row = lambda i: (i, 0)
    modspec = pl.BlockSpec((1, N_MOD, d), lambda i: (rows.mod_row(i), 0, 0))
    args = [x]
    in_specs = [pl.BlockSpec((ROW_TILE, d), lambda i: (rows.block_of(i, x_rows), 0))]
    if resid is not None:
        args += [resid, gate_mod]
        in_specs += [pl.BlockSpec((ROW_TILE, d), row), modspec]
    args.append(norm_w.reshape(1, d))
    in_specs.append(pl.BlockSpec((1, d), lambda i: (0, 0)))
    if mod is not None:
        args.append(mod)
        in_specs.append(modspec)
    m = rows.n_tiles * ROW_TILE
    out_shape, out_specs = [], []
    if resid is not None:
        out_shape.append(jax.ShapeDtypeStruct((m, d), F32))
        out_specs.append(pl.BlockSpec((ROW_TILE, d), row))
    out_shape.append(jax.ShapeDtypeStruct((m, d), out_dtype))
    out_specs.append(pl.BlockSpec((ROW_TILE, d), row))
    outs = pl.pallas_call(
        functools.partial(_norm_kernel, has_resid=resid is not None, gate_idx=gate_idx,
                          has_mod=mod is not None, shift_idx=shift_idx, scale_idx=scale_idx,
                          write_x=resid is not None),
        grid=(rows.n_tiles,), in_specs=in_specs, out_specs=out_specs, out_shape=out_shape,
        compiler_params=_params("parallel"), name="resid_norm",
    )(*args)
    return (outs[0], outs[1]) if resid is not None else (None, outs[0])


PREP_SUB_COLS = 256


def _gdn_prep_kernel(x_ref, up_ref, dn_ref, w_ref, o_ref, shift_ref, *, rows):
    @pl.when(pl.program_id(0) == 0)
    def _():
        ri = lax.broadcasted_iota(jnp.int32, (ROW_TILE, ROW_TILE), 0)
        ci = lax.broadcasted_iota(jnp.int32, (ROW_TILE, ROW_TILE), 1)
        shift_ref[0] = jnp.where(ci == ri - 1, 1.0, 0.0).astype(BF16)
        shift_ref[1] = jnp.where(ci == ri + 1, 1.0, 0.0).astype(BF16)

    _, first, last = rows.flags(pl.program_id(0))
    t = lax.broadcasted_iota(jnp.int32, (ROW_TILE, 1), 0)
    for c0 in range(0, GDN_QKV, PREP_SUB_COLS):
        sl = slice(c0, c0 + PREP_SUB_COLS)
        x = x_ref[:, sl]
        above = jnp.where(first, 0.0, up_ref[BF16_SUBLANES - 1:BF16_SUBLANES, sl].astype(F32))
        below = jnp.where(last, 0.0, dn_ref[0:1, sl].astype(F32))
        prev = jnp.where(t == 0, above, jnp.dot(shift_ref[0], x, preferred_element_type=F32))
        nxt = jnp.where(t == ROW_TILE - 1, below, jnp.dot(shift_ref[1], x, preferred_element_type=F32))
        y = prev * w_ref[0:1, sl] + x.astype(F32) * w_ref[1:2, sl] + nxt * w_ref[2:3, sl]
        y = y * _sigmoid(y)
        for h0 in range(0, PREP_SUB_COLS, GDN_DK):
            seg = y[:, h0:h0 + GDN_DK]
            if c0 + h0 < 2 * GDN_QK:
                inv = lax.rsqrt(jnp.sum(seg * seg, axis=-1, keepdims=True) + EPS)
                if c0 + h0 < GDN_QK:
                    inv = inv * GDN_DK ** -0.5
                seg = seg * inv
            o_ref[:, c0 + h0:c0 + h0 + GDN_DK] = seg.astype(o_ref.dtype)


def gdn_prep(p, rows, conv_w):
    m = p.shape[0]
    cb = GDN_QKV_COL // GDN_QKV
    hb = ROW_TILE // BF16_SUBLANES
    n_hblocks = m // BF16_SUBLANES
    return pl.pallas_call(
        functools.partial(_gdn_prep_kernel, rows=rows),
        grid=(rows.n_tiles,),
        in_specs=[pl.BlockSpec((ROW_TILE, GDN_QKV), lambda i: (i, cb)),
                  pl.BlockSpec((BF16_SUBLANES, GDN_QKV), lambda i: (jnp.maximum(i * hb - 1, 0), cb)),
                  pl.BlockSpec((BF16_SUBLANES, GDN_QKV),
                               lambda i: (jnp.minimum((i + 1) * hb, n_hblocks - 1), cb)),
                  pl.BlockSpec((3, GDN_QKV), lambda i: (0, 0))],
        out_specs=pl.BlockSpec((ROW_TILE, GDN_QKV), lambda i: (i, 0)),
        out_shape=jax.ShapeDtypeStruct((m, GDN_QKV), BF16),
        scratch_shapes=[pltpu.VMEM((2, ROW_TILE, ROW_TILE), BF16)],
        compiler_params=_params("arbitrary"), name="gdn_prep",
    )(p, p, p, conv_w)


def _bdot(a, b):
    return jnp.dot(a.astype(BF16), b.astype(BF16), preferred_element_type=F32)


def _bdot_nt(a, b):
    return lax.dot_general(a.astype(BF16), b.astype(BF16), (((1,), (1,)), ((), ())),
                           preferred_element_type=F32)


def _bdot_tn(a, b):
    return lax.dot_general(a.astype(BF16), b.astype(BF16), (((0,), (0,)), ((), ())),
                           preferred_element_type=F32)


def _rev_chunk(t, n_ctx, n_all):
    return jnp.where(t < n_ctx, n_ctx - 1 - t, n_all + n_ctx - 1 - t)


def _chunk_masks(d):
    ri = lax.broadcasted_iota(jnp.int32, (CHUNK, CHUNK), 0)
    ci = lax.broadcasted_iota(jnp.int32, (CHUNK, CHUNK), 1)
    lower, upper = ci <= ri, ci >= ri
    if d == 0:
        return lower, ci < ri, lower.astype(F32), upper.astype(F32)
    return upper, ci > ri, upper.astype(F32), lower.astype(F32)


SCAN_CHUNKS = 2


def _cumsum_dot(tri, x):
    tb = tri.astype(BF16)
    hi = x.astype(BF16)
    rest = x - hi.astype(F32)
    mid = rest.astype(BF16)
    lo = (rest - mid.astype(F32)).astype(BF16)
    return (jnp.dot(tb, hi, preferred_element_type=F32) + jnp.dot(tb, mid, preferred_element_type=F32)
            + jnp.dot(tb, lo, preferred_element_type=F32))


def _gla_kernel(qf, kf, vf, sf, qr, kr, vr, sr, w2_ref, b_ref, of, orr, st_ref):
    @pl.when(pl.program_id(1) == 0)
    def _():
        st_ref[...] = jnp.zeros_like(st_ref)

    dirs = ((qf, kf, vf, sf, of), (qr, kr, vr, sr, orr))
    slots = [[] for _ in range(SCAN_CHUNKS)]
    row_slice = lambda d, slot: slice((slot if d == 0 else SCAN_CHUNKS - 1 - slot) * CHUNK,
                                      (slot if d == 0 else SCAN_CHUNKS - 1 - slot) * CHUNK + CHUNK)
    blocks = [(d, slot) for d in range(N_DIR) for slot in range(SCAN_CHUNKS)]
    zs = {(d, slot): _bdot(dirs[d][3][0, row_slice(d, slot), d * GLA_RANK:(d + 1) * GLA_RANK], w2_ref[d])
          + b_ref[d] for d, slot in blocks}
    bcums = {(d, slot): _cumsum_dot(_chunk_masks(d)[2], -_softplus(-zs[d, slot]) / GLA_TAU)
             for d, slot in blocks}
    for d, (q_ref, k_ref, v_ref, s_ref, o_ref) in enumerate(dirs):
        incl = _chunk_masks(d)[0]
        last = CHUNK - 1 if d == 0 else 0
        for slot in range(SCAN_CHUNKS):
            rs = row_slice(d, slot)
            bcum = bcums[d, slot]
            q_all, k_all, v_all = q_ref[0, rs, :].astype(F32), k_ref[0, rs, :].astype(F32), v_ref[0, rs, :]
            for h in range(GLA_HEADS):
                ks = slice(h * GLA_DK, (h + 1) * GLA_DK)
                vs = slice(h * GLA_DV, (h + 1) * GLA_DV)
                b = bcum[:, ks]
                bl = bcum[last:last + 1, ks]
                k = k_all[:, ks]
                ch = dict(d=d, h=h, vs=vs, rs=rs, o_ref=o_ref, bl=bl, v=v_all[:, vs], incl=incl)
                ch["q_dec"] = q_all[:, ks] * (GLA_DK ** -0.5) * jnp.exp(b)
                ch["k_neg"] = k * jnp.exp(-b)
                ch["k_dec"] = k * jnp.exp(bl - b)
                slots[slot].append(ch)
    chains = [ch for slot in slots for ch in slot]
    for ch in chains:
        ch["scores"] = jnp.where(ch["incl"], _bdot_nt(ch["q_dec"], ch["k_neg"]), 0.0)
    for ch in chains:
        ch["intra"] = _bdot(ch["scores"], ch["v"])
        ch["kv"] = _bdot_tn(ch["v"], ch["k_dec"])
    state = {(d, h): st_ref[d, h] for d in range(N_DIR) for h in range(GLA_HEADS)}
    for slot in slots:
        for ch in slot:
            key = ch["d"], ch["h"]
            out = ch["intra"] + _bdot_nt(ch["q_dec"], state[key])
            ch["o_ref"][0, ch["rs"], ch["vs"]] = out.astype(ch["o_ref"].dtype)
            state[key] = jnp.exp(ch["bl"]) * state[key] + ch["kv"]
    for (d, h), st in state.items():
        st_ref[d, h] = st


def gla_scan(p, small, w2, bias, n_ctx):
    bsz, s, _ = p.shape
    rows_per_step = SCAN_CHUNKS * CHUNK
    n_ctx_steps, n_steps = n_ctx // SCAN_CHUNKS, s // rows_per_step
    assert n_ctx % SCAN_CHUNKS == 0 and s % rows_per_step == 0

    def specs(rev):
        def row(t):
            return _rev_chunk(t, n_ctx_steps, n_steps) if rev else t
        return [pl.BlockSpec((1, rows_per_step, GLA_QK), lambda b, t: (b, row(t), 0)),
                pl.BlockSpec((1, rows_per_step, GLA_QK), lambda b, t: (b, row(t), 1)),
                pl.BlockSpec((1, rows_per_step, GLA_V), lambda b, t: (b, row(t), 2 * GLA_QK // GLA_V)),
                pl.BlockSpec((1, rows_per_step, SMALL_WIDTH), lambda b, t: (b, row(t), 0))]

    out_f = pl.BlockSpec((1, rows_per_step, GLA_V), lambda b, t: (b, t, 0))
    out_r = pl.BlockSpec((1, rows_per_step, GLA_V),
                         lambda b, t: (b, _rev_chunk(t, n_ctx_steps, n_steps), 0))
    return pl.pallas_call(
        _gla_kernel,
        grid=(bsz, n_steps),
        in_specs=specs(False) + specs(True) + [
            pl.BlockSpec((N_DIR, GLA_RANK, GLA_QK), lambda b, t: (0, 0, 0)),
            pl.BlockSpec((N_DIR, 1, GLA_QK), lambda b, t: (0, 0, 0))],
        out_specs=[out_f, out_r],
        out_shape=[jax.ShapeDtypeStruct((bsz, s, GLA_V), BF16)] * 2,
        scratch_shapes=[pltpu.VMEM((N_DIR, GLA_HEADS, GLA_DV, GLA_DK), F32)],
        compiler_params=_params("parallel", "arbitrary"),
        name="gla_scan",
    )(p, p, p, small, p, p, p, small, w2, bias.reshape(N_DIR, 1, GLA_QK))


def _gdn_kernel(qf, kf, vf, sf, qr, kr, vr, sr, alog_ref, dtb_ref, of, orr, s_ref):
    @pl.when(pl.program_id(1) == 0)
    def _():
        s_ref[...] = jnp.zeros_like(s_ref)

    ri = lax.broadcasted_iota(jnp.int32, (CHUNK, CHUNK), 0)
    ci = lax.broadcasted_iota(jnp.int32, (CHUNK, CHUNK), 1)
    eye = (ri == ci).astype(F32)
    same16 = (ri >> 4) == (ci >> 4)
    same32 = (ri >> 5) == (ci >> 5)
    nh = GDN_HEADS
    slots = [[] for _ in range(SCAN_CHUNKS)]
    for d, (q_ref, k_ref, v_ref, sm_ref, o_ref) in enumerate(
            ((qf, kf, vf, sf, of), (qr, kr, vr, sr, orr))):
        incl, strict, tri, tri_t = _chunk_masks(d)
        for slot in range(SCAN_CHUNKS):
            sub = slot if d == 0 else SCAN_CHUNKS - 1 - slot
            rs = slice(sub * CHUNK, (sub + 1) * CHUNK)
            sm = sm_ref[0, rs, :]
            g_all = -jnp.exp(alog_ref[...]) * _softplus(sm + dtb_ref[...])
            beta_all = _sigmoid(sm)
            gcol = g_all[:, A_COL:A_COL + N_DIR * nh]
            grow = g_all.T[A_COL:A_COL + N_DIR * nh, :]
            bcol = jnp.dot(tri, gcol, precision=HIGHEST, preferred_element_type=F32)
            brow = jnp.dot(grow, tri_t, precision=HIGHEST, preferred_element_type=F32)
            last = CHUNK - 1 if d == 0 else 0
            btot = bcol[last:last + 1, :]
            q_all = q_ref[0, rs, :].astype(F32)
            k_all = k_ref[0, rs, :].astype(F32)
            v_all = v_ref[0, rs, :].astype(F32)
            for h in range(nh):
                c = d * nh + h
                hs = slice(h * GDN_DK, (h + 1) * GDN_DK)
                ch = dict(d=d, h=h, hs=hs, rs=rs, o_ref=o_ref, incl=incl, strict=strict)
                ch["bc"] = bcol[:, c:c + 1]
                ch["br"] = brow[c:c + 1, :]
                ch["bl"] = btot[:, c:c + 1]
                ch["beta"] = beta_all[:, BT_COL + c:BT_COL + c + 1]
                ch["q"], ch["k"], ch["v"] = q_all[:, hs], k_all[:, hs], v_all[:, hs]
                slots[slot].append(ch)

    def stage_scores(chains):
        for ch in chains:
            incl = ch["incl"]
            ch["decay"] = jnp.where(incl, jnp.exp(jnp.where(incl, ch["bc"] - ch["br"], 0.0)), 0.0)
            ch["kb"] = ch["k"] * ch["beta"]
            ch["a2"] = _bdot_nt(jnp.concatenate([ch["kb"], ch["q"]], axis=0), ch["k"])

    def stage_split(chains):
        for ch in chains:
            a2 = ch.pop("a2")
            n_mat = -jnp.where(ch["strict"], a2[:CHUNK] * ch["decay"], 0.0)
            ch["attn"] = a2[CHUNK:] * ch["decay"]
            n_diag = jnp.where(same16, n_mat, 0.0)
            ch["n32"] = jnp.where(same32, n_mat, 0.0) - n_diag
            ch["n64"] = jnp.where(same32, 0.0, n_mat)
            ch["p"] = eye + n_diag
            ch["n"] = _bdot(n_diag, n_diag)

    def stage_diag(square):
        def run(chains):
            for ch in chains:
                ch["p"] = ch["p"] + _bdot(ch["p"], ch["n"])
                if square:
                    ch["n"] = _bdot(ch["n"], ch["n"])
        return run

    def stage_merge_a(off):
        def run(chains):
            for ch in chains:
                ch["x"] = _bdot(ch["p"], ch[off])
        return run

    def stage_merge_b(chains):
        for ch in chains:
            ch["p"] = ch["p"] + _bdot(ch.pop("x"), ch["p"])

    def stage_uw(chains):
        for ch in chains:
            ch["e_b"] = jnp.exp(ch["bc"])
            uw = _bdot(ch["p"], jnp.concatenate([ch["v"] * ch["beta"], ch["kb"] * ch["e_b"]], axis=1))
            ch["u"], ch["w"] = uw[:, :GDN_DV], uw[:, GDN_DV:]

    state = {(d, h): s_ref[d, h] for d in range(N_DIR) for h in range(nh)}

    def tail_read(chains):
        for ch in chains:
            s = state[ch["d"], ch["h"]]
            ch["wq"] = _bdot(jnp.concatenate([ch["w"], ch["q"] * ch["e_b"]], axis=0), s)

    def tail_update(chains):
        for ch in chains:
            wq = ch.pop("wq")
            v_new = ch["u"] - wq[:CHUNK]
            ch["o_ref"][0, ch["rs"], ch["hs"]] = (
                wq[CHUNK:] + _bdot(ch["attn"], v_new)).astype(ch["o_ref"].dtype)
            k_dec = ch["k"] * jnp.exp(ch["bl"] - ch["bc"])
            key = ch["d"], ch["h"]
            state[key] = jnp.exp(ch["bl"]) * state[key] + _bdot_tn(k_dec, v_new)

    stages = [stage_scores, stage_split, stage_diag(True), stage_diag(True), stage_diag(False),
              stage_merge_a("n32"), stage_merge_b, stage_merge_a("n64"), stage_merge_b, stage_uw]
    for stage in stages:
        stage(slots[0])
    for slot in range(1, SCAN_CHUNKS):
        for i, stage in enumerate(stages):
            stage(slots[slot])
            if i == 1:
                tail_read(slots[slot - 1])
            elif i == 4:
                tail_update(slots[slot - 1])
    tail_read(slots[-1])
    tail_update(slots[-1])
    for (d, h), s in state.items():
        s_ref[d, h] = s


def gdn_scan(qkv, small, a_log, dt_bias, n_ctx):
    bsz, s, _ = qkv.shape
    rows_per_step = SCAN_CHUNKS * CHUNK
    n_ctx_steps, n_steps = n_ctx // SCAN_CHUNKS, s // rows_per_step
    assert n_ctx % SCAN_CHUNKS == 0 and s % rows_per_step == 0
    lane_row = lambda t: jnp.zeros((1, SMALL_WIDTH), F32).at[0, A_COL:A_COL + N_DIR * GDN_HEADS].set(
        t.astype(F32).reshape(-1))

    def specs(rev):
        def row(t):
            return _rev_chunk(t, n_ctx_steps, n_steps) if rev else t
        return [pl.BlockSpec((1, rows_per_step, GDN_QK), lambda b, t: (b, row(t), 0)),
                pl.BlockSpec((1, rows_per_step, GDN_QK), lambda b, t: (b, row(t), 1)),
                pl.BlockSpec((1, rows_per_step, GDN_V), lambda b, t: (b, row(t), 2)),
                pl.BlockSpec((1, rows_per_step, SMALL_WIDTH), lambda b, t: (b, row(t), 0))]

    const = pl.BlockSpec((1, SMALL_WIDTH), lambda b, t: (0, 0))
    out_f = pl.BlockSpec((1, rows_per_step, GDN_V), lambda b, t: (b, t, 0))
    out_r = pl.BlockSpec((1, rows_per_step, GDN_V),
                         lambda b, t: (b, _rev_chunk(t, n_ctx_steps, n_steps), 0))
    return pl.pallas_call(
        _gdn_kernel,
        grid=(bsz, n_steps),
        in_specs=specs(False) + specs(True) + [const, const],
        out_specs=[out_f, out_r],
        out_shape=[jax.ShapeDtypeStruct((bsz, s, GDN_V), BF16)] * 2,
        scratch_shapes=[pltpu.VMEM((N_DIR, GDN_HEADS, GDN_DK, GDN_DV), F32)],
        compiler_params=_params("parallel", "arbitrary"),
        name="gdn_scan",
    )(qkv, qkv, qkv, small, qkv, qkv, qkv, small, lane_row(a_log), lane_row(dt_bias))


FUSED_COLS = 512
FUSED_SUB_COLS = 256
FUSED_PAD = 8


def _ffn_up_gate_kernel(x_ref, xu_ref, xd_ref, wa_ref, wv_ref, wc_ref, o_ref, ext_ref, *,
                        seq_rows, ctx_rows):
    g, tm, pad = GRID_W, x_ref.shape[0], FUSED_PAD
    n = tm + 2 * g
    x, xu, xd = x_ref[...], xu_ref[...], xd_ref[...]
    r0 = (pl.program_id(0) % (seq_rows // tm)) * tm
    rbe = r0 - g + lax.broadcasted_iota(jnp.int32, (n, 1), 0)
    ctx_e = (rbe >= 0) & (rbe < ctx_rows)
    col_e = jnp.where(ctx_e, rbe, (rbe - ctx_rows) & (g - 1))
    keep_left = (col_e != jnp.where(ctx_e, ctx_rows - 1, g - 1)).astype(F32)
    keep_right = (col_e != 0).astype(F32)
    rb = r0 + lax.broadcasted_iota(jnp.int32, (tm, 1), 0)
    above_ok = (rb >= ctx_rows + g).astype(F32)
    below_ok = ((rb >= ctx_rows) & (rb < seq_rows - g)).astype(F32)
    zeros = jnp.zeros((2 * pad, FUSED_COLS), F32)
    for k in (0, 2):
        ext_ref[k, 0:2 * pad] = zeros
        ext_ref[k, n:n + 2 * pad] = zeros
    subs = [slice(c0, c0 + FUSED_SUB_COLS) for c0 in range(0, FUSED_COLS, FUSED_SUB_COLS)]
    for cs in subs:
        wa = wa_ref[:, cs]
        a_ext = jnp.concatenate([jnp.dot(xu, wa, preferred_element_type=F32),
                                 jnp.dot(x, wa, preferred_element_type=F32),
                                 jnp.dot(xd, wa, preferred_element_type=F32)], axis=0)
        ext_ref[0, pl.ds(pad + 1, n), cs] = a_ext * keep_left
        ext_ref[1, pl.ds(pad, n), cs] = a_ext
        ext_ref[2, pl.ds(pad - 1, n), cs] = a_ext * keep_right
    for cs in subs:
        v = jnp.dot(x, wv_ref[:, cs], preferred_element_type=F32)
        rows = []
        for dr in (-1, 0, 1):
            part = None
            for dc in (-1, 0, 1):
                tap = 3 * (dr + 1) + dc + 1
                term = ext_ref[dc + 1, pl.ds(pad + g + dr * g, tm), cs] * wc_ref[tap:tap + 1, cs]
                part = term if part is None else part + term
            rows.append(part)
        acc = rows[1] + above_ok * rows[0] + below_ok * rows[2]
        o_ref[:, cs] = (acc * _sigmoid(acc) * v).astype(o_ref.dtype)


def ffn_up_gate(h, w_up, w_conv, seq_rows, ctx_rows):
    m, d = h.shape
    tile_rows = next(t for t in range(1024, 0, -GRID_W) if seq_rows % t == 0)
    assert m % seq_rows == 0 and D_FF % FUSED_COLS == 0 and ctx_rows % GRID_W == 0
    nc = D_FF // FUSED_COLS
    gb = tile_rows // GRID_W
    n_gblocks = m // GRID_W
    return pl.pallas_call(
        functools.partial(_ffn_up_gate_kernel, seq_rows=seq_rows, ctx_rows=ctx_rows),
        grid=(m // tile_rows, nc),
        in_specs=[pl.BlockSpec((tile_rows, d), lambda i, j: (i, 0)),
                  pl.BlockSpec((GRID_W, d), lambda i, j: (jnp.maximum(i * gb - 1, 0), 0)),
                  pl.BlockSpec((GRID_W, d), lambda i, j: (jnp.minimum((i + 1) * gb, n_gblocks - 1), 0)),
                  pl.BlockSpec((d, FUSED_COLS), lambda i, j: (0, j)),
                  pl.BlockSpec((d, FUSED_COLS), lambda i, j: (0, nc + j)),
                  pl.BlockSpec((9, FUSED_COLS), lambda i, j: (0, j))],
        out_specs=pl.BlockSpec((tile_rows, FUSED_COLS), lambda i, j: (i, j)),
        out_shape=jax.ShapeDtypeStruct((m, D_FF), BF16),
        scratch_shapes=[pltpu.VMEM((3, tile_rows + 2 * GRID_W + 2 * FUSED_PAD, FUSED_COLS), F32)],
        compiler_params=_params("parallel", "arbitrary"), name="ffn_up_gate",
    )(h, h, h, w_up, w_up, w_conv.reshape(9, D_FF))


BRANCH_COLS = GLA_V
assert GDN_V == BRANCH_COLS and POOL_WIDTH == BRANCH_COLS
BRANCH_WIDTH = N_BRANCH * BRANCH_COLS
Z_GLA_COL = 2 * GLA_QK + GLA_V
Z_GDN_COL = GDN_QKV_COL + GDN_QKV
POOL_COL = Z_GDN_COL + GDN_V
GATES_COL = POOL_COL + POOL_WIDTH
POOL_HALO = BF16_SUBLANES
assert POOL_HALO >= max(POOL_WINDOWS) // 2


def _headnorm_gate(o, z, nw, n_heads, width, o_ref, col0):
    for h in range(n_heads):
        sl = slice(h * width, (h + 1) * width)
        oh, zh = o[:, sl], z[:, sl]
        inv = lax.rsqrt(jnp.mean(oh * oh, axis=-1, keepdims=True) + EPS)
        o_ref[:, col0 + h * width:col0 + (h + 1) * width] = (
            oh * inv * nw * (zh * _sigmoid(zh))).astype(o_ref.dtype)


def _merge_prep_kernel(gf_ref, gr_ref, df_ref, dr_ref, za_ref, zb_ref, u_ref, uup_ref, udn_ref,
                       gnw_ref, dnw_ref, pw_ref, ps_ref, o_ref, win_ref, *, rows):
    halo, n = POOL_HALO, ROW_TILE + 2 * POOL_HALO

    @pl.when(pl.program_id(0) == 0)
    def _():
        ri = lax.broadcasted_iota(jnp.int32, (ROW_TILE, n), 0)
        ci = lax.broadcasted_iota(jnp.int32, (ROW_TILE, n), 1) - halo
        for gi, win in enumerate(POOL_WINDOWS):
            lo = ri - win // 2
            win_ref[gi] = jnp.where((ci >= lo) & (ci < lo + win), 1.0, 0.0).astype(BF16)

    _, first, last = rows.flags(pl.program_id(0))
    _headnorm_gate(gf_ref[...].astype(F32) + gr_ref[...].astype(F32), za_ref[...].astype(F32),
                   gnw_ref[...], GLA_HEADS, GLA_DV, o_ref, 0)
    _headnorm_gate(df_ref[...].astype(F32) + dr_ref[...].astype(F32), zb_ref[...].astype(F32),
                   dnw_ref[...], GDN_HEADS, GDN_DV, o_ref, GLA_V)
    halo_zero = jnp.zeros(uup_ref.shape, uup_ref.dtype)
    u = u_ref[...]
    ext = jnp.concatenate([jnp.where(first, halo_zero, uup_ref[...]), u,
                           jnp.where(last, halo_zero, udn_ref[...])], axis=0)
    t = lax.broadcasted_iota(jnp.int32, (ROW_TILE, 1), 0)
    for gi, win in enumerate(POOL_WINDOWS):
        sl = slice(gi * POOL_GROUP, (gi + 1) * POOL_GROUP)
        half = win // 2
        win_sum = jnp.dot(win_ref[gi], ext[:, sl], preferred_element_type=F32)
        lo_clip = jnp.where(first, jnp.maximum(half - t, 0), 0)
        hi_clip = jnp.where(last, jnp.maximum(t - half + win - ROW_TILE, 0), 0)
        cnt = (win - lo_clip - hi_clip).astype(F32)
        pg = win_sum / cnt - u[:, sl].astype(F32)
        yp = _bdot(pg, pw_ref[gi]) * ps_ref[:, sl]
        o_ref[:, GLA_V + GDN_V + gi * POOL_GROUP:GLA_V + GDN_V + (gi + 1) * POOL_GROUP] = yp.astype(o_ref.dtype)


def merge_prep(p, o_gla, o_gdn, p_rows, rows, gla_nw, gdn_nw, pool_w, pool_scale):
    hb = ROW_TILE // BF16_SUBLANES
    n_hblocks = p.shape[0] // BF16_SUBLANES
    blk = lambda i: rows.block_of(i, p_rows)
    wide = lambda col: pl.BlockSpec((ROW_TILE, BRANCH_COLS), lambda i: (blk(i), col // BRANCH_COLS))
    const = lambda shape: pl.BlockSpec(shape, lambda i: (0,) * len(shape))
    pc = POOL_COL // POOL_WIDTH
    return pl.pallas_call(
        functools.partial(_merge_prep_kernel, rows=rows),
        grid=(rows.n_tiles,),
        in_specs=[wide(0), wide(0), wide(0), wide(0), wide(Z_GLA_COL), wide(Z_GDN_COL), wide(POOL_COL),
                  pl.BlockSpec((POOL_HALO, POOL_WIDTH), lambda i: (jnp.maximum(blk(i) * hb - 1, 0), pc)),
                  pl.BlockSpec((POOL_HALO, POOL_WIDTH),
                               lambda i: (jnp.minimum((blk(i) + 1) * hb, n_hblocks - 1), pc)),
                  const((1, GLA_DV)), const((1, GDN_DV)),
                  const((POOL_GROUPS, POOL_GROUP, POOL_GROUP)), const((1, POOL_WIDTH))],
        out_specs=pl.BlockSpec((ROW_TILE, BRANCH_WIDTH), lambda i: (i, 0)),
        out_shape=jax.ShapeDtypeStruct((rows.n_tiles * ROW_TILE, BRANCH_WIDTH), BF16),
        scratch_shapes=[pltpu.VMEM((POOL_GROUPS, ROW_TILE, ROW_TILE + 2 * POOL_HALO), BF16)],
        compiler_params=_params("arbitrary"), name="merge_prep",
    )(o_gla[0], o_gla[1], o_gdn[0], o_gdn[1], p, p, p, p, p,
      gla_nw.reshape(1, GLA_DV), gdn_nw.reshape(1, GDN_DV), pool_w.astype(BF16),
      pool_scale.reshape(1, POOL_WIDTH))


def _branch_kernel(ya, yb, yc, ga, gb, gc, wa, wb, wc, o_ref):
    acc = None
    for y_ref, g_ref, w_ref in ((ya, ga, wa), (yb, gb, wb), (yc, gc, wc)):
        term = _sigmoid(g_ref[...].astype(F32)) * jnp.dot(y_ref[...], w_ref[...], preferred_element_type=F32)
        acc = term if acc is None else acc + term
    o_ref[...] = acc.astype(o_ref.dtype)


def branch_merge(ycat, p, p_rows, rows, w_gla, w_gdn, w_pool):
    blk = lambda i: rows.block_of(i, p_rows)
    ysp = lambda k: pl.BlockSpec((ROW_TILE, BRANCH_COLS), lambda i: (i, k))
    gsp = lambda k: pl.BlockSpec((ROW_TILE, D_MODEL), lambda i: (blk(i), GATES_COL // D_MODEL + k))
    wsp = pl.BlockSpec((BRANCH_COLS, D_MODEL), lambda i: (0, 0))
    return pl.pallas_call(
        _branch_kernel,
        grid=(rows.n_tiles,),
        in_specs=[ysp(0), ysp(1), ysp(2), gsp(0), gsp(1), gsp(2), wsp, wsp, wsp],
        out_specs=pl.BlockSpec((ROW_TILE, D_MODEL), lambda i: (i, 0)),
        out_shape=jax.ShapeDtypeStruct((rows.n_tiles * ROW_TILE, D_MODEL), BF16),
        compiler_params=_params("parallel"), name="branch_merge",
    )(ycat, ycat, ycat, p, p, p, w_gla, w_gdn, w_pool)


def permute_w_in(w):
    q, k, v, zg, lr, gqkv, ga, gbt, gz, pool, gates = _split_cols(w, IN_SPLITS)
    main = jnp.concatenate([q, k, v, zg, gqkv, gz, pool, gates], axis=1)
    small = jnp.concatenate([lr, ga, gbt, jnp.zeros((w.shape[0], SMALL_WIDTH - SMALL_COLS), w.dtype)], axis=1)
    return main.astype(BF16), small.astype(BF16)


def hybrid_mixer(h, rows, out_rows, bsz, w_main, w_small, gla_w2, gla_b, gla_nw, gdn_cw, gdn_alog,
                 gdn_dtb, gdn_nw, pool_w, pool_scale, w_br_gla, w_br_gdn, w_br_pool, w_out):
    m = h.shape[0]
    s = m // bsz
    p = pmm(h, w_main, BF16)
    small = pmm(h, w_small)
    p3, small3 = p.reshape(bsz, s, MAIN_WIDTH), small.reshape(bsz, s, SMALL_WIDTH)
    n_ctx = (ROW_TILE if rows.has_ctx else 0) // CHUNK
    o_gla = [t.reshape(m, GLA_V) for t in gla_scan(p3, small3, gla_w2.astype(BF16), gla_b, n_ctx)]
    qkv = gdn_prep(p, rows, gdn_cw).reshape(bsz, s, GDN_QKV)
    o_gdn = [t.reshape(m, GDN_V) for t in gdn_scan(qkv, small3, gdn_alog, gdn_dtb, n_ctx)]
    ycat = merge_prep(p, o_gla, o_gdn, rows, out_rows, gla_nw, gdn_nw, pool_w, pool_scale)
    mrg = branch_merge(ycat, p, rows, out_rows, w_br_gla, w_br_gdn, w_br_pool)
    return pmm(mrg, w_out, BF16)


def kernel(x, c, ctx, c_ctx, ada_w, ada_b, norm1_w, norm2_w, w_in, gla_lr_w2, gla_lr_b,
           gla_norm_w, gdn_conv_w, gdn_a_log, gdn_dt_bias, gdn_norm_w, pool_w, pool_scale,
           w_br_gla, w_br_gdn, w_br_pool, w_out, ffn_up, ffn_conv, ffn_down, final_norm_w):
    B, T, _ = x.shape
    assert ctx.shape[1] == ROW_TILE and T % ROW_TILE == 0 and (T // GRID_W) % (ROW_TILE // GRID_W) == 0
    s_all = ROW_TILE + T
    all_rows = Rows(B, True, T // ROW_TILE)
    lat_rows = Rows(B, False, T // ROW_TILE)
    xa = jnp.concatenate([ctx, x], axis=1).reshape(B * s_all, D_MODEL)
    n_cond = -(-(B + 1) // BF16_SUBLANES) * BF16_SUBLANES
    cond = jnp.concatenate([c, c_ctx[None, :], jnp.zeros((n_cond - B - 1, D_MODEL), F32)], axis=0)
    cond = jax.nn.silu(cond)
    mods = [(pmm(cond, ada_w[l]) + ada_b[l]).reshape(n_cond, N_MOD, D_MODEL) for l in range(DEPTH)]
    _, h = resid_norm(xa, all_rows, all_rows, norm1_w[0], mod=mods[0], shift_idx=0, scale_idx=1)
    x_rows = all_rows
    for l in range(DEPTH):
        last = l == DEPTH - 1
        rows = lat_rows if last else all_rows
        w_main, w_small = permute_w_in(w_in[l])
        y = hybrid_mixer(h, x_rows, rows, B, w_main, w_small, gla_lr_w2[l], gla_lr_b[l], gla_norm_w[l],
                         gdn_conv_w[l], gdn_a_log[l], gdn_dt_bias[l], gdn_norm_w[l],
                         pool_w[l], pool_scale[l], w_br_gla[l].astype(BF16),
                         w_br_gdn[l].astype(BF16), w_br_pool[l].astype(BF16), w_out[l].astype(BF16))
        xa, h2 = resid_norm(xa, x_rows, rows, norm2_w[l], resid=y, gate_mod=mods[l], gate_idx=2,
                            mod=mods[l], shift_idx=3, scale_idx=4)
        x_rows = rows
        ctx_rows = ROW_TILE if rows.has_ctx else 0
        gated = ffn_up_gate(h2, ffn_up[l].astype(BF16), ffn_conv[l], ctx_rows + T, ctx_rows)
        dn = pmm(gated, ffn_down[l].astype(BF16), BF16)
        if last:
            _, out = resid_norm(xa, x_rows, rows, final_norm_w, resid=dn, gate_mod=mods[l], gate_idx=5,
                                out_dtype=F32)
            return out.reshape(B, T, D_MODEL)
        xa, h = resid_norm(xa, x_rows, rows, norm1_w[l + 1], resid=dn, gate_mod=mods[l], gate_idx=5,
                           mod=mods[l + 1], shift_idx=0, scale_idx=1)
```

```python
import functools

import jax
import jax.numpy as jnp
from jax import lax
from jax.experimental import pallas as pl
from jax.experimental.pallas import tpu as pltpu

D_MODEL = 2048
DEPTH = 4
GRID_W = 64
CHUNK = 64
N_DIR = 2
N_BRANCH = 3
N_MOD = 6
EPS = 1e-6

GLA_HEADS = 4
GLA_DK = 128
GLA_DV = 256
GLA_RANK = 16
GLA_TAU = 16.0

GDN_HEADS = 8
GDN_DK = 128
GDN_DV = 128

POOL_WINDOWS = (2, 4, 8, 16)
POOL_GROUPS = 4
POOL_GROUP = 256

D_FF = 5632

GLA_QK = GLA_HEADS * GLA_DK
GLA_V = GLA_HEADS * GLA_DV
GDN_QK = GDN_HEADS * GDN_DK
GDN_V = GDN_HEADS * GDN_DV
GDN_QKV = 2 * GDN_QK + GDN_V
POOL_WIDTH = POOL_GROUPS * POOL_GROUP
IN_SPLITS = (GLA_QK, GLA_QK, GLA_V, GLA_V, N_DIR * GLA_RANK,
             GDN_QKV, N_DIR * GDN_HEADS, N_DIR * GDN_HEADS, GDN_V,
             POOL_WIDTH, N_BRANCH * D_MODEL)
F32 = jnp.float32
BF16 = jnp.bfloat16
HIGHEST = lax.Precision.HIGHEST

V7X_VMEM_BYTES = 64 * 1024 * 1024
VMEM_LIMIT_BYTES = V7X_VMEM_BYTES * 3 // 4
MM_TILE_BUDGET_BYTES = V7X_VMEM_BYTES * 5 // 8
LANES = 128
BF16_SUBLANES = 16

SMALL_WIDTH = LANES
SMALL_COLS = N_DIR * GLA_RANK + 2 * N_DIR * GDN_HEADS
A_COL = N_DIR * GLA_RANK
BT_COL = A_COL + N_DIR * GDN_HEADS
MAIN_SPLITS = (GLA_QK, GLA_QK, GLA_V, GLA_V, GDN_QKV, GDN_V, POOL_WIDTH, N_BRANCH * D_MODEL)
MAIN_WIDTH = sum(MAIN_SPLITS)
GDN_QKV_COL = 2 * GLA_QK + 2 * GLA_V

ROW_TILE = 256


def _split_cols(t, sizes):
    parts, start = [], 0
    for size in sizes:
        parts.append(t[..., start:start + size])
        start += size
    return parts


def _params(*sem):
    return pltpu.CompilerParams(dimension_semantics=sem, vmem_limit_bytes=VMEM_LIMIT_BYTES)


def _sigmoid(x):
    return 1.0 / (1.0 + jnp.exp(-x))


def _softplus(x):
    return jnp.maximum(x, 0.0) + jnp.log(1.0 + jnp.exp(-jnp.abs(x)))


def _mm_kernel(x_ref, w_ref, o_ref):
    o_ref[...] = jnp.dot(x_ref[...].astype(BF16), w_ref[...].astype(BF16),
                         preferred_element_type=F32).astype(o_ref.dtype)


def _mm_tiles(m, k, n, x_bytes, w_bytes, o_bytes):
    best = None
    for tm in (1024, 512, 256, 128, 64, 32, 16, 8):
        if m % tm:
            continue
        for tn in (2048, 1024, 512, 256, 128):
            if n % tn:
                continue
            need = 2 * (tm * k * x_bytes + k * tn * w_bytes + tm * tn * o_bytes)
            if need > MM_TILE_BUDGET_BYTES:
                continue
            score = tm * tn / (tm + tn)
            if best is None or score > best[0]:
                best = (score, tm, tn)
    if best is None:
        raise ValueError(f"no matmul tiling for {(m, k, n)}")
    return best[1], best[2]


def pmm(x, w, out_dtype=F32):
    m, k = x.shape
    n = w.shape[1]
    tm, tn = _mm_tiles(m, k, n, x.dtype.itemsize, w.dtype.itemsize, jnp.dtype(out_dtype).itemsize)
    return pl.pallas_call(
        _mm_kernel,
        grid=(m // tm, n // tn),
        in_specs=[pl.BlockSpec((tm, k), lambda i, j: (i, 0)),
                  pl.BlockSpec((k, tn), lambda i, j: (0, j))],
        out_specs=pl.BlockSpec((tm, tn), lambda i, j: (i, j)),
        out_shape=jax.ShapeDtypeStruct((m, n), out_dtype),
        compiler_params=_params("parallel", "arbitrary"),
        name="mm",
    )(x, w)


class Rows:
    def __init__(self, batch, has_ctx, lat_tiles):
        self.batch, self.has_ctx, self.lat_tiles = batch, has_ctx, lat_tiles
        self.per_batch = lat_tiles + (1 if has_ctx else 0)
        self.n_tiles = batch * self.per_batch

    def block_of(self, i, other):
        if other.has_ctx == self.has_ctx:
            return i
        assert other.has_ctx and not self.has_ctx
        return (i // self.per_batch) * other.per_batch + 1 + i % self.per_batch

    def mod_row(self, i):
        b = i // self.per_batch
        if not self.has_ctx:
            return b
        return jnp.where(i % self.per_batch == 0, self.batch, b)

    def flags(self, i):
        r = i % self.per_batch
        if not self.has_ctx:
            return False, r == 0, r == self.per_batch - 1
        is_ctx = r == 0
        return is_ctx, is_ctx | (r == 1), is_ctx | (r == self.per_batch - 1)


def _norm_kernel(*refs, has_resid, gate_idx, has_mod, shift_idx, scale_idx, write_x):
    refs = list(refs)
    x_ref = refs.pop(0)
    x = x_ref[...]
    if has_resid:
        y_ref, gmod_ref = refs.pop(0), refs.pop(0)
        x = x + gmod_ref[0, gate_idx:gate_idx + 1, :] * y_ref[...].astype(F32)
    nw_ref = refs.pop(0)
    mod_ref = refs.pop(0) if has_mod else None
    if write_x:
        refs.pop(0)[...] = x
    h_ref = refs.pop(0)
    y = x * lax.rsqrt(jnp.mean(x * x, axis=-1, keepdims=True) + EPS) * nw_ref[...]
    if has_mod:
        y = y * (1.0 + mod_ref[0, scale_idx:scale_idx + 1, :]) + mod_ref[0, shift_idx:shift_idx + 1, :]
    h_ref[...] = y.astype(h_ref.dtype)


def resid_norm(x, x_rows, rows, norm_w, *, resid=None, gate_mod=None, gate_idx=0,
               mod=None, shift_idx=0, scale_idx=1, out_dtype=BF16):
    d = x.shape[-1]
    row = lambda i: (i, 0)
    modspec = pl.BlockSpec((1, N_MOD, d), lambda i: (rows.mod_row(i), 0, 0))
    args = [x]
    in_specs = [pl.BlockSpec((ROW_TILE, d), lambda i: (rows.block_of(i, x_rows), 0))]
    if resid is not None:
        args += [resid, gate_mod]
        in_specs += [pl.BlockSpec((ROW_TILE, d), row), modspec]
    args.append(norm_w.reshape(1, d))
    in_specs.append(pl.BlockSpec((1, d), lambda i: (0, 0)))
    if mod is not None:
        args.append(mod)
        in_specs.append(modspec)
    m = rows.n_tiles * ROW_TILE
    out_shape, out_specs = [], []
    if resid is not None:
        out_shape.append(jax.ShapeDtypeStruct((m, d), F32))
        out_specs.append(pl.BlockSpec((ROW_TILE, d), row))
    out_shape.append(jax.ShapeDtypeStruct((m, d), out_dtype))
    out_specs.append(pl.BlockSpec((ROW_TILE, d), row))
    outs = pl.pallas_call(
        functools.partial(_norm_kernel, has_resid=resid is not None, gate_idx=gate_idx,
                          has_mod=mod is not None, shift_idx=shift_idx, scale_idx=scale_idx,
                          write_x=resid is not None),
        grid=(rows.n_tiles,), in_specs=in_specs, out_specs=out_specs, out_shape=out_shape,
        compiler_params=_params("parallel"), name="resid_norm",
    )(*args)
    return (outs[0], outs[1]) if resid is not None else (None, outs[0])


PREP_SUB_COLS = 256


def _gdn_prep_kernel(x_ref, up_ref, dn_ref, w_ref, o_ref, shift_ref, *, rows):
    @pl.when(pl.program_id(0) == 0)
    def _():
        ri = lax.broadcasted_iota(jnp.int32, (ROW_TILE, ROW_TILE), 0)
        ci = lax.broadcasted_iota(jnp.int32, (ROW_TILE, ROW_TILE), 1)
        shift_ref[0] = jnp.where(ci == ri - 1, 1.0, 0.0).astype(BF16)
        shift_ref[1] = jnp.where(ci == ri + 1, 1.0, 0.0).astype(BF16)

    _, first, last = rows.flags(pl.program_id(0))
    t = lax.broadcasted_iota(jnp.int32, (ROW_TILE, 1), 0)
    for c0 in range(0, GDN_QKV, PREP_SUB_COLS):
        sl = slice(c0, c0 + PREP_SUB_COLS)
        x = x_ref[:, sl]
        above = jnp.where(first, 0.0, up_ref[BF16_SUBLANES - 1:BF16_SUBLANES, sl].astype(F32))
        below = jnp.where(last, 0.0, dn_ref[0:1, sl].astype(F32))
        prev = jnp.where(t == 0, above, jnp.dot(shift_ref[0], x, preferred_element_type=F32))
        nxt = jnp.where(t == ROW_TILE - 1, below, jnp.dot(shift_ref[1], x, preferred_element_type=F32))
        y = prev * w_ref[0:1, sl] + x.astype(F32) * w_ref[1:2, sl] + nxt * w_ref[2:3, sl]
        y = y * _sigmoid(y)
        for h0 in range(0, PREP_SUB_COLS, GDN_DK):
            seg = y[:, h0:h0 + GDN_DK]
            if c0 + h0 < 2 * GDN_QK:
                inv = lax.rsqrt(jnp.sum(seg * seg, axis=-1, keepdims=True) + EPS)
                if c0 + h0 < GDN_QK:
                    inv = inv * GDN_DK ** -0.5
                seg = seg * inv
            o_ref[:, c0 + h0:c0 + h0 + GDN_DK] = seg.astype(o_ref.dtype)


def gdn_prep(p, rows, conv_w):
    m = p.shape[0]
    cb = GDN_QKV_COL // GDN_QKV
    hb = ROW_TILE // BF16_SUBLANES
    n_hblocks = m // BF16_SUBLANES
    return pl.pallas_call(
        functools.partial(_gdn_prep_kernel, rows=rows),
        grid=(rows.n_tiles,),
        in_specs=[pl.BlockSpec((ROW_TILE, GDN_QKV), lambda i: (i, cb)),
                  pl.BlockSpec((BF16_SUBLANES, GDN_QKV), lambda i: (jnp.maximum(i * hb - 1, 0), cb)),
                  pl.BlockSpec((BF16_SUBLANES, GDN_QKV),
                               lambda i: (jnp.minimum((i + 1) * hb, n_hblocks - 1), cb)),
                  pl.BlockSpec((3, GDN_QKV), lambda i: (0, 0))],
        out_specs=pl.BlockSpec((ROW_TILE, GDN_QKV), lambda i: (i, 0)),
        out_shape=jax.ShapeDtypeStruct((m, GDN_QKV), BF16),
        scratch_shapes=[pltpu.VMEM((2, ROW_TILE, ROW_TILE), BF16)],
        compiler_params=_params("arbitrary"), name="gdn_prep",
    )(p, p, p, conv_w)


def _bdot(a, b):
    return jnp.dot(a.astype(BF16), b.astype(BF16), preferred_element_type=F32)


def _bdot_nt(a, b):
    return lax.dot_general(a.astype(BF16), b.astype(BF16), (((1,), (1,)), ((), ())),
                           preferred_element_type=F32)


def _bdot_tn(a, b):
    return lax.dot_general(a.astype(BF16), b.astype(BF16), (((0,), (0,)), ((), ())),
                           preferred_element_type=F32)


def _rev_chunk(t, n_ctx, n_all):
    return jnp.where(t < n_ctx, n_ctx - 1 - t, n_all + n_ctx - 1 - t)


def _chunk_masks(d):
    ri = lax.broadcasted_iota(jnp.int32, (CHUNK, CHUNK), 0)
    ci = lax.broadcasted_iota(jnp.int32, (CHUNK, CHUNK), 1)
    lower, upper = ci <= ri, ci >= ri
    if d == 0:
        return lower, ci < ri, lower.astype(F32), upper.astype(F32)
    return upper, ci > ri, upper.astype(F32), lower.astype(F32)


SCAN_CHUNKS = 2


def _cumsum_dot(tri, x):
    tb = tri.astype(BF16)
    hi = x.astype(BF16)
    rest = x - hi.astype(F32)
    mid = rest.astype(BF16)
    lo = (rest - mid.astype(F32)).astype(BF16)
    return (jnp.dot(tb, hi, preferred_element_type=F32) + jnp.dot(tb, mid, preferred_element_type=F32)
            + jnp.dot(tb, lo, preferred_element_type=F32))


def _gla_kernel(qf, kf, vf, sf, qr, kr, vr, sr, w2_ref, b_ref, of, orr, st_ref):
    @pl.when(pl.program_id(1) == 0)
    def _():
        st_ref[...] = jnp.zeros_like(st_ref)

    dirs = ((qf, kf, vf, sf, of), (qr, kr, vr, sr, orr))
    slots = [[] for _ in range(SCAN_CHUNKS)]
    row_slice = lambda d, slot: slice((slot if d == 0 else SCAN_CHUNKS - 1 - slot) * CHUNK,
                                      (slot if d == 0 else SCAN_CHUNKS - 1 - slot) * CHUNK + CHUNK)
    blocks = [(d, slot) for d in range(N_DIR) for slot in range(SCAN_CHUNKS)]
    zs = {(d, slot): _bdot(dirs[d][3][0, row_slice(d, slot), d * GLA_RANK:(d + 1) * GLA_RANK], w2_ref[d])
          + b_ref[d] for d, slot in blocks}
    bcums = {(d, slot): _cumsum_dot(_chunk_masks(d)[2], -_softplus(-zs[d, slot]) / GLA_TAU)
             for d, slot in blocks}
    for d, (q_ref, k_ref, v_ref, s_ref, o_ref) in enumerate(dirs):
        incl = _chunk_masks(d)[0]
        last = CHUNK - 1 if d == 0 else 0
        for slot in range(SCAN_CHUNKS):
            rs = row_slice(d, slot)
            bcum = bcums[d, slot]
            q_all, k_all, v_all = q_ref[0, rs, :].astype(F32), k_ref[0, rs, :].astype(F32), v_ref[0, rs, :]
            for h in range(GLA_HEADS):
                ks = slice(h * GLA_DK, (h + 1) * GLA_DK)
                vs = slice(h * GLA_DV, (h + 1) * GLA_DV)
                b = bcum[:, ks]
                bl = bcum[last:last + 1, ks]
                k = k_all[:, ks]
                ch = dict(d=d, h=h, vs=vs, rs=rs, o_ref=o_ref, bl=bl, v=v_all[:, vs], incl=incl)
                ch["q_dec"] = q_all[:, ks] * (GLA_DK ** -0.5) * jnp.exp(b)
                ch["k_neg"] = k * jnp.exp(-b)
                ch["k_dec"] = k * jnp.exp(bl - b)
                slots[slot].append(ch)
    chains = [ch for slot in slots for ch in slot]
    for ch in chains:
        ch["scores"] = jnp.where(ch["incl"], _bdot_nt(ch["q_dec"], ch["k_neg"]), 0.0)
    for ch in chains:
        ch["intra"] = _bdot(ch["scores"], ch["v"])
        ch["kv"] = _bdot_tn(ch["v"], ch["k_dec"])
    state = {(d, h): st_ref[d, h] for d in range(N_DIR) for h in range(GLA_HEADS)}
    for slot in slots:
        for ch in slot:
            key = ch["d"], ch["h"]
            out = ch["intra"] + _bdot_nt(ch["q_dec"], state[key])
            ch["o_ref"][0, ch["rs"], ch["vs"]] = out.astype(ch["o_ref"].dtype)
            state[key] = jnp.exp(ch["bl"]) * state[key] + ch["kv"]
    for (d, h), st in state.items():
        st_ref[d, h] = st


def gla_scan(p, small, w2, bias, n_ctx):
    bsz, s, _ = p.shape
    rows_per_step = SCAN_CHUNKS * CHUNK
    n_ctx_steps, n_steps = n_ctx // SCAN_CHUNKS, s // rows_per_step
    assert n_ctx % SCAN_CHUNKS == 0 and s % rows_per_step == 0

    def specs(rev):
        def row(t):
            return _rev_chunk(t, n_ctx_steps, n_steps) if rev else t
        return [pl.BlockSpec((1, rows_per_step, GLA_QK), lambda b, t: (b, row(t), 0)),
                pl.BlockSpec((1, rows_per_step, GLA_QK), lambda b, t: (b, row(t), 1)),
                pl.BlockSpec((1, rows_per_step, GLA_V), lambda b, t: (b, row(t), 2 * GLA_QK // GLA_V)),
                pl.BlockSpec((1, rows_per_step, SMALL_WIDTH), lambda b, t: (b, row(t), 0))]

    out_f = pl.BlockSpec((1, rows_per_step, GLA_V), lambda b, t: (b, t, 0))
    out_r = pl.BlockSpec((1, rows_per_step, GLA_V),
                         lambda b, t: (b, _rev_chunk(t, n_ctx_steps, n_steps), 0))
    return pl.pallas_call(
        _gla_kernel,
        grid=(bsz, n_steps),
        in_specs=specs(False) + specs(True) + [
            pl.BlockSpec((N_DIR, GLA_RANK, GLA_QK), lambda b, t: (0, 0, 0)),
            pl.BlockSpec((N_DIR, 1, GLA_QK), lambda b, t: (0, 0, 0))],
        out_specs=[out_f, out_r],
        out_shape=[jax.ShapeDtypeStruct((bsz, s, GLA_V), BF16)] * 2,
        scratch_shapes=[pltpu.VMEM((N_DIR, GLA_HEADS, GLA_DV, GLA_DK), F32)],
        compiler_params=_params("parallel", "arbitrary"),
        name="gla_scan",
    )(p, p, p, small, p, p, p, small, w2, bias.reshape(N_DIR, 1, GLA_QK))


def _gdn_kernel(qf, kf, vf, sf, qr, kr, vr, sr, alog_ref, dtb_ref, of, orr, s_ref):
    @pl.when(pl.program_id(1) == 0)
    def _():
        s_ref[...] = jnp.zeros_like(s_ref)

    ri = lax.broadcasted_iota(jnp.int32, (CHUNK, CHUNK), 0)
    ci = lax.broadcasted_iota(jnp.int32, (CHUNK, CHUNK), 1)
    eye = (ri == ci).astype(F32)
    same16 = (ri >> 4) == (ci >> 4)
    same32 = (ri >> 5) == (ci >> 5)
    nh = GDN_HEADS
    slots = [[] for _ in range(SCAN_CHUNKS)]
    for d, (q_ref, k_ref, v_ref, sm_ref, o_ref) in enumerate(
            ((qf, kf, vf, sf, of), (qr, kr, vr, sr, orr))):
        incl, strict, tri, tri_t = _chunk_masks(d)
        for slot in range(SCAN_CHUNKS):
            sub = slot if d == 0 else SCAN_CHUNKS - 1 - slot
            rs = slice(sub * CHUNK, (sub + 1) * CHUNK)
            sm = sm_ref[0, rs, :]
            g_all = -jnp.exp(alog_ref[...]) * _softplus(sm + dtb_ref[...])
            beta_all = _sigmoid(sm)
            gcol = g_all[:, A_COL:A_COL + N_DIR * nh]
            grow = g_all.T[A_COL:A_COL + N_DIR * nh, :]
            bcol = jnp.dot(tri, gcol, precision=HIGHEST, preferred_element_type=F32)
            brow = jnp.dot(grow, tri_t, precision=HIGHEST, preferred_element_type=F32)
            last = CHUNK - 1 if d == 0 else 0
            btot = bcol[last:last + 1, :]
            q_all = q_ref[0, rs, :].astype(F32)
            k_all = k_ref[0, rs, :].astype(F32)
            v_all = v_ref[0, rs, :].astype(F32)
            for h in range(nh):
                c = d * nh + h
                hs = slice(h * GDN_DK, (h + 1) * GDN_DK)
                ch = dict(d=d, h=h, hs=hs, rs=rs, o_ref=o_ref, incl=incl, strict=strict)
                ch["bc"] = bcol[:, c:c + 1]
                ch["br"] = brow[c:c + 1, :]
                ch["bl"] = btot[:, c:c + 1]
                ch["beta"] = beta_all[:, BT_COL + c:BT_COL + c + 1]
                ch["q"], ch["k"], ch["v"] = q_all[:, hs], k_all[:, hs], v_all[:, hs]
                slots[slot].append(ch)

    def stage_scores(chains):
        for ch in chains:
            incl = ch["incl"]
            ch["decay"] = jnp.where(incl, jnp.exp(jnp.where(incl, ch["bc"] - ch["br"], 0.0)), 0.0)
            ch["kb"] = ch["k"] * ch["beta"]
            ch["a2"] = _bdot_nt(jnp.concatenate([ch["kb"], ch["q"]], axis=0), ch["k"])

    def stage_split(chains):
        for ch in chains:
            a2 = ch.pop("a2")
            n_mat = -jnp.where(ch["strict"], a2[:CHUNK] * ch["decay"], 0.0)
            ch["attn"] = a2[CHUNK:] * ch["decay"]
            n_diag = jnp.where(same16, n_mat, 0.0)
            ch["n32"] = jnp.where(same32, n_mat, 0.0) - n_diag
            ch["n64"] = jnp.where(same32, 0.0, n_mat)
            ch["p"] = eye + n_diag
            ch["n"] = _bdot(n_diag, n_diag)

    def stage_diag(square):
        def run(chains):
            for ch in chains:
                ch["p"] = ch["p"] + _bdot(ch["p"], ch["n"])
                if square:
                    ch["n"] = _bdot(ch["n"], ch["n"])
        return run

    def stage_merge_a(off):
        def run(chains):
            for ch in chains:
                ch["x"] = _bdot(ch["p"], ch[off])
        return run

    def stage_merge_b(chains):
        for ch in chains:
            ch["p"] = ch["p"] + _bdot(ch.pop("x"), ch["p"])

    def stage_uw(chains):
        for ch in chains:
            ch["e_b"] = jnp.exp(ch["bc"])
            uw = _bdot(ch["p"], jnp.concatenate([ch["v"] * ch["beta"], ch["kb"] * ch["e_b"]], axis=1))
            ch["u"], ch["w"] = uw[:, :GDN_DV], uw[:, GDN_DV:]

    state = {(d, h): s_ref[d, h] for d in range(N_DIR) for h in range(nh)}

    def tail_read(chains):
        for ch in chains:
            s = state[ch["d"], ch["h"]]
            ch["wq"] = _bdot(jnp.concatenate([ch["w"], ch["q"] * ch["e_b"]], axis=0), s)

    def tail_update(chains):
        for ch in chains:
            wq = ch.pop("wq")
            v_new = ch["u"] - wq[:CHUNK]
            ch["o_ref"][0, ch["rs"], ch["hs"]] = (
                wq[CHUNK:] + _bdot(ch["attn"], v_new)).astype(ch["o_ref"].dtype)
            k_dec = ch["k"] * jnp.exp(ch["bl"] - ch["bc"])
            key = ch["d"], ch["h"]
            state[key] = jnp.exp(ch["bl"]) * state[key] + _bdot_tn(k_dec, v_new)

    stages = [stage_scores, stage_split, stage_diag(True), stage_diag(True), stage_diag(False),
              stage_merge_a("n32"), stage_merge_b, stage_merge_a("n64"), stage_merge_b, stage_uw]
    for stage in stages:
        stage(slots[0])
    for slot in range(1, SCAN_CHUNKS):
        for i, stage in enumerate(stages):
            stage(slots[slot])
            if i == 1:
                tail_read(slots[slot - 1])
            elif i == 4:
                tail_update(slots[slot - 1])
    tail_read(slots[-1])
    tail_update(slots[-1])
    for (d, h), s in state.items():
        s_ref[d, h] = s


def gdn_scan(qkv, small, a_log, dt_bias, n_ctx):
    bsz, s, _ = qkv.shape
    rows_per_step = SCAN_CHUNKS * CHUNK
    n_ctx_steps, n_steps = n_ctx // SCAN_CHUNKS, s // rows_per_step
    assert n_ctx % SCAN_CHUNKS == 0 and s % rows_per_step == 0
    lane_row = lambda t: jnp.zeros((1, SMALL_WIDTH), F32).at[0, A_COL:A_COL + N_DIR * GDN_HEADS].set(
        t.astype(F32).reshape(-1))

    def specs(rev):
        def row(t):
            return _rev_chunk(t, n_ctx_steps, n_steps) if rev else t
        return [pl.BlockSpec((1, rows_per_step, GDN_QK), lambda b, t: (b, row(t), 0)),
                pl.BlockSpec((1, rows_per_step, GDN_QK), lambda b, t: (b, row(t), 1)),
                pl.BlockSpec((1, rows_per_step, GDN_V), lambda b, t: (b, row(t), 2)),
                pl.BlockSpec((1, rows_per_step, SMALL_WIDTH), lambda b, t: (b, row(t), 0))]

    const = pl.BlockSpec((1, SMALL_WIDTH), lambda b, t: (0, 0))
    out_f = pl.BlockSpec((1, rows_per_step, GDN_V), lambda b, t: (b, t, 0))
    out_r = pl.BlockSpec((1, rows_per_step, GDN_V),
                         lambda b, t: (b, _rev_chunk(t, n_ctx_steps, n_steps), 0))
    return pl.pallas_call(
        _gdn_kernel,
        grid=(bsz, n_steps),
        in_specs=specs(False) + specs(True) + [const, const],
        out_specs=[out_f, out_r],
        out_shape=[jax.ShapeDtypeStruct((bsz, s, GDN_V), BF16)] * 2,
        scratch_shapes=[pltpu.VMEM((N_DIR, GDN_HEADS, GDN_DK, GDN_DV), F32)],
        compiler_params=_params("parallel", "arbitrary"),
        name="gdn_scan",
    )(qkv, qkv, qkv, small, qkv, qkv, qkv, small, lane_row(a_log), lane_row(dt_bias))


FUSED_MAX_ROWS = 18 * GRID_W
FUSED_COLS = 512
FUSED_SUB_COLS = 256
FUSED_PAD = 8


def _ffn_up_gate_kernel(x_ref, xu_ref, xd_ref, wa_ref, wv_ref, wc_ref, o_ref, ext_ref, *,
                        seq_rows, ctx_rows):
    g, tm, pad = GRID_W, x_ref.shape[0], FUSED_PAD
    n = tm + 2 * g
    x, xu, xd = x_ref[...], xu_ref[...], xd_ref[...]
    r0 = (pl.program_id(0) % (seq_rows // tm)) * tm
    rbe = r0 - g + lax.broadcasted_iota(jnp.int32, (n, 1), 0)
    ctx_e = (rbe >= 0) & (rbe < ctx_rows)
    col_e = jnp.where(ctx_e, rbe, (rbe - ctx_rows) & (g - 1))
    keep_left = (col_e != jnp.where(ctx_e, ctx_rows - 1, g - 1)).astype(F32)
    keep_right = (col_e != 0).astype(F32)
    rb = r0 + lax.broadcasted_iota(jnp.int32, (tm, 1), 0)
    above_ok = (rb >= ctx_rows + g).astype(F32)
    below_ok = ((rb >= ctx_rows) & (rb < seq_rows - g)).astype(F32)
    zeros = jnp.zeros((2 * pad, FUSED_COLS), F32)
    for k in (0, 2):
        ext_ref[k, 0:2 * pad] = zeros
        ext_ref[k, n:n + 2 * pad] = zeros
    subs = [slice(c0, c0 + FUSED_SUB_COLS) for c0 in range(0, FUSED_COLS, FUSED_SUB_COLS)]
    for cs in subs:
        wa = wa_ref[:, cs]
        a_ext = jnp.concatenate([jnp.dot(xu, wa, preferred_element_type=F32),
                                 jnp.dot(x, wa, preferred_element_type=F32),
                                 jnp.dot(xd, wa, preferred_element_type=F32)], axis=0)
        ext_ref[0, pl.ds(pad + 1, n), cs] = a_ext * keep_left
        ext_ref[1, pl.ds(pad, n), cs] = a_ext
        ext_ref[2, pl.ds(pad - 1, n), cs] = a_ext * keep_right
    for cs in subs:
        v = jnp.dot(x, wv_ref[:, cs], preferred_element_type=F32)
        rows = []
        for dr in (-1, 0, 1):
            part = None
            for dc in (-1, 0, 1):
                tap = 3 * (dr + 1) + dc + 1
                term = ext_ref[dc + 1, pl.ds(pad + g + dr * g, tm), cs] * wc_ref[tap:tap + 1, cs]
                part = term if part is None else part + term
            rows.append(part)
        acc = rows[1] + above_ok * rows[0] + below_ok * rows[2]
        o_ref[:, cs] = (acc * _sigmoid(acc) * v).astype(o_ref.dtype)


def ffn_up_gate(h, w_up, w_conv, seq_rows, ctx_rows):
    m, d = h.shape
    tile_rows = next(t for t in range(FUSED_MAX_ROWS, 0, -GRID_W) if seq_rows % t == 0)
    assert m % seq_rows == 0 and D_FF % FUSED_COLS == 0 and ctx_rows % GRID_W == 0
    nc = D_FF // FUSED_COLS
    gb = tile_rows // GRID_W
    n_gblocks = m // GRID_W
    return pl.pallas_call(
        functools.partial(_ffn_up_gate_kernel, seq_rows=seq_rows, ctx_rows=ctx_rows),
        grid=(m // tile_rows, nc),
        in_specs=[pl.BlockSpec((tile_rows, d), lambda i, j: (i, 0)),
                  pl.BlockSpec((GRID_W, d), lambda i, j: (jnp.maximum(i * gb - 1, 0), 0)),
                  pl.BlockSpec((GRID_W, d), lambda i, j: (jnp.minimum((i + 1) * gb, n_gblocks - 1), 0)),
                  pl.BlockSpec((d, FUSED_COLS), lambda i, j: (0, j)),
                  pl.BlockSpec((d, FUSED_COLS), lambda i, j: (0, nc + j)),
                  pl.BlockSpec((9, FUSED_COLS), lambda i, j: (0, j))],
        out_specs=pl.BlockSpec((tile_rows, FUSED_COLS), lambda i, j: (i, j)),
        out_shape=jax.ShapeDtypeStruct((m, D_FF), BF16),
        scratch_shapes=[pltpu.VMEM((3, tile_rows + 2 * GRID_W + 2 * FUSED_PAD, FUSED_COLS), F32)],
        compiler_params=_params("parallel", "arbitrary"), name="ffn_up_gate",
    )(h, h, h, w_up, w_up, w_conv.reshape(9, D_FF))


BRANCH_COLS = GLA_V
assert GDN_V == BRANCH_COLS and POOL_WIDTH == BRANCH_COLS
BRANCH_WIDTH = N_BRANCH * BRANCH_COLS
Z_GLA_COL = 2 * GLA_QK + GLA_V
Z_GDN_COL = GDN_QKV_COL + GDN_QKV
POOL_COL = Z_GDN_COL + GDN_V
GATES_COL = POOL_COL + POOL_WIDTH
POOL_HALO = BF16_SUBLANES
assert POOL_HALO >= max(POOL_WINDOWS) // 2


def _headnorm_gate(o, z, nw, n_heads, width, o_ref, col0):
    for h in range(n_heads):
        sl = slice(h * width, (h + 1) * width)
        oh, zh = o[:, sl], z[:, sl]
        inv = lax.rsqrt(jnp.mean(oh * oh, axis=-1, keepdims=True) + EPS)
        o_ref[:, col0 + h * width:col0 + (h + 1) * width] = (
            oh * inv * nw * (zh * _sigmoid(zh))).astype(o_ref.dtype)


def _merge_prep_kernel(gf_ref, gr_ref, df_ref, dr_ref, za_ref, zb_ref, u_ref, uup_ref, udn_ref,
                       gnw_ref, dnw_ref, pw_ref, ps_ref, o_ref, win_ref, *, rows):
    halo, n = POOL_HALO, ROW_TILE + 2 * POOL_HALO

    @pl.when(pl.program_id(0) == 0)
    def _():
        ri = lax.broadcasted_iota(jnp.int32, (ROW_TILE, n), 0)
        ci = lax.broadcasted_iota(jnp.int32, (ROW_TILE, n), 1) - halo
        for gi, win in enumerate(POOL_WINDOWS):
            lo = ri - win // 2
            win_ref[gi] = jnp.where((ci >= lo) & (ci < lo + win), 1.0, 0.0).astype(BF16)

    _, first, last = rows.flags(pl.program_id(0))
    _headnorm_gate(gf_ref[...].astype(F32) + gr_ref[...].astype(F32), za_ref[...].astype(F32),
                   gnw_ref[...], GLA_HEADS, GLA_DV, o_ref, 0)
    _headnorm_gate(df_ref[...].astype(F32) + dr_ref[...].astype(F32), zb_ref[...].astype(F32),
                   dnw_ref[...], GDN_HEADS, GDN_DV, o_ref, GLA_V)
    halo_zero = jnp.zeros(uup_ref.shape, uup_ref.dtype)
    u = u_ref[...]
    ext = jnp.concatenate([jnp.where(first, halo_zero, uup_ref[...]), u,
                           jnp.where(last, halo_zero, udn_ref[...])], axis=0)
    t = lax.broadcasted_iota(jnp.int32, (ROW_TILE, 1), 0)
    for gi, win in enumerate(POOL_WINDOWS):
        sl = slice(gi * POOL_GROUP, (gi + 1) * POOL_GROUP)
        half = win // 2
        win_sum = jnp.dot(win_ref[gi], ext[:, sl], preferred_element_type=F32)
        lo_clip = jnp.where(first, jnp.maximum(half - t, 0), 0)
        hi_clip = jnp.where(last, jnp.maximum(t - half + win - ROW_TILE, 0), 0)
        cnt = (win - lo_clip - hi_clip).astype(F32)
        pg = win_sum / cnt - u[:, sl].astype(F32)
        yp = _bdot(pg, pw_ref[gi]) * ps_ref[:, sl]
        o_ref[:, GLA_V + GDN_V + gi * POOL_GROUP:GLA_V + GDN_V + (gi + 1) * POOL_GROUP] = yp.astype(o_ref.dtype)


def merge_prep(p, o_gla, o_gdn, p_rows, rows, gla_nw, gdn_nw, pool_w, pool_scale):
    hb = ROW_TILE // BF16_SUBLANES
    n_hblocks = p.shape[0] // BF16_SUBLANES
    blk = lambda i: rows.block_of(i, p_rows)
    wide = lambda col: pl.BlockSpec((ROW_TILE, BRANCH_COLS), lambda i: (blk(i), col // BRANCH_COLS))
    const = lambda shape: pl.BlockSpec(shape, lambda i: (0,) * len(shape))
    pc = POOL_COL // POOL_WIDTH
    return pl.pallas_call(
        functools.partial(_merge_prep_kernel, rows=rows),
        grid=(rows.n_tiles,),
        in_specs=[wide(0), wide(0), wide(0), wide(0), wide(Z_GLA_COL), wide(Z_GDN_COL), wide(POOL_COL),
                  pl.BlockSpec((POOL_HALO, POOL_WIDTH), lambda i: (jnp.maximum(blk(i) * hb - 1, 0), pc)),
                  pl.BlockSpec((POOL_HALO, POOL_WIDTH),
                               lambda i: (jnp.minimum((blk(i) + 1) * hb, n_hblocks - 1), pc)),
                  const((1, GLA_DV)), const((1, GDN_DV)),
                  const((POOL_GROUPS, POOL_GROUP, POOL_GROUP)), const((1, POOL_WIDTH))],
        out_specs=pl.BlockSpec((ROW_TILE, BRANCH_WIDTH), lambda i: (i, 0)),
        out_shape=jax.ShapeDtypeStruct((rows.n_tiles * ROW_TILE, BRANCH_WIDTH), BF16),
        scratch_shapes=[pltpu.VMEM((POOL_GROUPS, ROW_TILE, ROW_TILE + 2 * POOL_HALO), BF16)],
        compiler_params=_params("arbitrary"), name="merge_prep",
    )(o_gla[0], o_gla[1], o_gdn[0], o_gdn[1], p, p, p, p, p,
      gla_nw.reshape(1, GLA_DV), gdn_nw.reshape(1, GDN_DV), pool_w.astype(BF16),
      pool_scale.reshape(1, POOL_WIDTH))


def _branch_kernel(ya, yb, yc, ga, gb, gc, wa, wb, wc, o_ref):
    acc = None
    for y_ref, g_ref, w_ref in ((ya, ga, wa), (yb, gb, wb), (yc, gc, wc)):
        term = _sigmoid(g_ref[...].astype(F32)) * jnp.dot(y_ref[...], w_ref[...], preferred_element_type=F32)
        acc = term if acc is None else acc + term
    o_ref[...] = acc.astype(o_ref.dtype)


def branch_merge(ycat, p, p_rows, rows, w_gla, w_gdn, w_pool):
    blk = lambda i: rows.block_of(i, p_rows)
    ysp = lambda k: pl.BlockSpec((ROW_TILE, BRANCH_COLS), lambda i: (i, k))
    gsp = lambda k: pl.BlockSpec((ROW_TILE, D_MODEL), lambda i: (blk(i), GATES_COL // D_MODEL + k))
    wsp = pl.BlockSpec((BRANCH_COLS, D_MODEL), lambda i: (0, 0))
    return pl.pallas_call(
        _branch_kernel,
        grid=(rows.n_tiles,),
        in_specs=[ysp(0), ysp(1), ysp(2), gsp(0), gsp(1), gsp(2), wsp, wsp, wsp],
        out_specs=pl.BlockSpec((ROW_TILE, D_MODEL), lambda i: (i, 0)),
        out_shape=jax.ShapeDtypeStruct((rows.n_tiles * ROW_TILE, D_MODEL), BF16),
        compiler_params=_params("parallel"), name="branch_merge",
    )(ycat, ycat, ycat, p, p, p, w_gla, w_gdn, w_pool)


def permute_w_in(w):
    q, k, v, zg, lr, gqkv, ga, gbt, gz, pool, gates = _split_cols(w, IN_SPLITS)
    main = jnp.concatenate([q, k, v, zg, gqkv, gz, pool, gates], axis=1)
    small = jnp.concatenate([lr, ga, gbt, jnp.zeros((w.shape[0], SMALL_WIDTH - SMALL_COLS), w.dtype)], axis=1)
    return main.astype(BF16), small.astype(BF16)


def hybrid_mixer(h, rows, out_rows, bsz, w_main, w_small, gla_w2, gla_b, gla_nw, gdn_cw, gdn_alog,
                 gdn_dtb, gdn_nw, pool_w, pool_scale, w_br_gla, w_br_gdn, w_br_pool, w_out):
    m = h.shape[0]
    s = m // bsz
    p = pmm(h, w_main, BF16)
    small = pmm(h, w_small)
    p3, small3 = p.reshape(bsz, s, MAIN_WIDTH), small.reshape(bsz, s, SMALL_WIDTH)
    n_ctx = (ROW_TILE if rows.has_ctx else 0) // CHUNK
    o_gla = [t.reshape(m, GLA_V) for t in gla_scan(p3, small3, gla_w2.astype(BF16), gla_b, n_ctx)]
    qkv = gdn_prep(p, rows, gdn_cw).reshape(bsz, s, GDN_QKV)
    o_gdn = [t.reshape(m, GDN_V) for t in gdn_scan(qkv, small3, gdn_alog, gdn_dtb, n_ctx)]
    ycat = merge_prep(p, o_gla, o_gdn, rows, out_rows, gla_nw, gdn_nw, pool_w, pool_scale)
    mrg = branch_merge(ycat, p, rows, out_rows, w_br_gla, w_br_gdn, w_br_pool)
    return pmm(mrg, w_out, BF16)


def kernel(x, c, ctx, c_ctx, ada_w, ada_b, norm1_w, norm2_w, w_in, gla_lr_w2, gla_lr_b,
           gla_norm_w, gdn_conv_w, gdn_a_log, gdn_dt_bias, gdn_norm_w, pool_w, pool_scale,
           w_br_gla, w_br_gdn, w_br_pool, w_out, ffn_up, ffn_conv, ffn_down, final_norm_w):
    B, T, _ = x.shape
    assert ctx.shape[1] == ROW_TILE and T % ROW_TILE == 0 and (T // GRID_W) % (ROW_TILE // GRID_W) == 0
    s_all = ROW_TILE + T
    all_rows = Rows(B, True, T // ROW_TILE)
    lat_rows = Rows(B, False, T // ROW_TILE)
    xa = jnp.concatenate([ctx, x], axis=1).reshape(B * s_all, D_MODEL)
    n_cond = -(-(B + 1) // BF16_SUBLANES) * BF16_SUBLANES
    cond = jnp.concatenate([c, c_ctx[None, :], jnp.zeros((n_cond - B - 1, D_MODEL), F32)], axis=0)
    cond = jax.nn.silu(cond)
    mods = [(pmm(cond, ada_w[l]) + ada_b[l]).reshape(n_cond, N_MOD, D_MODEL) for l in range(DEPTH)]
    _, h = resid_norm(xa, all_rows, all_rows, norm1_w[0], mod=mods[0], shift_idx=0, scale_idx=1)
    x_rows = all_rows
    for l in range(DEPTH):
        last = l == DEPTH - 1
        rows = lat_rows if last else all_rows
        w_main, w_small = permute_w_in(w_in[l])
        y = hybrid_mixer(h, x_rows, rows, B, w_main, w_small, gla_lr_w2[l], gla_lr_b[l], gla_norm_w[l],
                         gdn_conv_w[l], gdn_a_log[l], gdn_dt_bias[l], gdn_norm_w[l],
                         pool_w[l], pool_scale[l], w_br_gla[l].astype(BF16),
                         w_br_gdn[l].astype(BF16), w_br_pool[l].astype(BF16), w_out[l].astype(BF16))
        xa, h2 = resid_norm(xa, x_rows, rows, norm2_w[l], resid=y, gate_mod=mods[l], gate_idx=2,
                            mod=mods[l], shift_idx=3, scale_idx=4)
        x_rows = rows
        ctx_rows = ROW_TILE if rows.has_ctx else 0
        gated = ffn_up_gate(h2, ffn_up[l].astype(BF16), ffn_conv[l], ctx_rows + T, ctx_rows)
        dn = pmm(gated, ffn_down[l].astype(BF16), BF16)
        if last:
            _, out = resid_norm(xa, x_rows, rows, final_norm_w, resid=dn, gate_mod=mods[l], gate_idx=5,
                                out_dtype=F32)
            return out.reshape(B, T, D_MODEL)
        xa, h = resid_norm(xa, x_rows, rows, norm1_w[l + 1], resid=dn, gate_mod=mods[l], gate_idx=5,
                           mod=mods[l + 1], shift_idx=0, scale_idx=1)
```
